```python
import math
import functools
import jax
import jax.numpy as jnp
from jax import lax
import numpy as np

D_MODEL = 1024
BATCH = 4
SEQ = 8192
DEPTH = 1
DEC_BATCH = 32
DEC_SEQ = 1
PAST_LEN = 16384
PAGE_SIZE = 128

D_MIX = D_MODEL
D_SSM = D_MIX // 2
SSM_GROUP = 16
N_SSM_GROUPS = D_SSM // SSM_GROUP
SSM_STATE = 64
SSM_CHUNK = 128
D_ATTN = D_MIX - D_SSM
ATTN_HEAD_DIM = 64
N_ATTN_HEADS = D_ATTN // ATTN_HEAD_DIM
MOBA_BLOCK = 256
MOBA_TOPK = 3
MOBA_QCHUNK = 32
N_MEM = 256
N_XHEADS = 4
XHEAD_DIM = D_MODEL // N_XHEADS
D_FF = ((8 * D_MODEL // 3 + 255) // 256) * 256
D_PROJ = D_SSM + 3 * D_ATTN
RMS_EPS = 1e-6
NEG_INF = -1e30

kernel_name = 'hymba_s5_moba_macaron_decode_step'


def _rmsnorm(x, g):
    x32 = x.astype(jnp.float32)
    y = x32 * lax.rsqrt(jnp.mean(x32 * x32, axis=-1, keepdims=True) + RMS_EPS)
    return (y * g.astype(jnp.float32)).astype(x.dtype)


def _swiglu(h, w1, w3, w2):
    return (jax.nn.silu(h @ w1) * (h @ w3)) @ w2


def _alibi_slopes(n_heads):
    return jnp.exp2(-8.0 * (jnp.arange(n_heads, dtype=jnp.float32) + 1.0) / n_heads)


def _s5_discretise(a_re, a_im, log_dt, b_re, b_im):
    a_re = a_re.astype(jnp.float32)
    a_im = a_im.astype(jnp.float32)
    dt = jnp.exp(log_dt.astype(jnp.float32))[:, None]
    mag = jnp.exp(a_re * dt)
    abar_re = mag * jnp.cos(a_im * dt)
    abar_im = mag * jnp.sin(a_im * dt)
    den = a_re * a_re + a_im * a_im
    nr = abar_re - 1.0
    f_re = (nr * a_re + abar_im * a_im) / den
    f_im = (abar_im * a_re - nr * a_im) / den
    b_re = b_re.astype(jnp.float32)
    b_im = b_im.astype(jnp.float32)
    bbar_re = f_re[..., None] * b_re - f_im[..., None] * b_im
    bbar_im = f_re[..., None] * b_im + f_im[..., None] * b_re
    return abar_re, abar_im, bbar_re, bbar_im


def _s5_combine(e1, e2):
    a1r, a1i, b1r, b1i = e1
    a2r, a2i, b2r, b2i = e2
    return (a2r * a1r - a2i * a1i, a2r * a1i + a2i * a1r,
            a2r * b1r - a2i * b1i + b2r, a2r * b1i + a2i * b1r + b2i)


def _s5_scan_block(bu_re, bu_im, h0_re, h0_im, abar_re, abar_im):
    shape = bu_re.shape
    ar = jnp.broadcast_to(abar_re, shape)
    ai = jnp.broadcast_to(abar_im, shape)
    cr, ci, hr, hi = lax.associative_scan(_s5_combine, (ar, ai, bu_re, bu_im), axis=1)
    hr = hr + cr * h0_re[:, None] - ci * h0_im[:, None]
    hi = hi + cr * h0_im[:, None] + ci * h0_re[:, None]
    return hr, hi


def _s5_mixer(u, h0_re, h0_im, w):
    n, l, _ = u.shape
    abar_re, abar_im, bbar_re, bbar_im = _s5_discretise(
        w['ssm_a_re'], w['ssm_a_im'], w['ssm_log_dt'], w['ssm_b_re'], w['ssm_b_im'])
    ug = u.astype(jnp.float32).reshape(n, l, N_SSM_GROUPS, SSM_GROUP)
    bu_re = jnp.einsum('nlgc,gpc->nlgp', ug, bbar_re)
    bu_im = jnp.einsum('nlgc,gpc->nlgp', ug, bbar_im)
    h0_re = h0_re.astype(jnp.float32)
    h0_im = h0_im.astype(jnp.float32)
    if l > SSM_CHUNK and l % SSM_CHUNK == 0:
        n_chunks = l // SSM_CHUNK

        def to_chunks(a):
            return a.reshape(n, n_chunks, SSM_CHUNK, N_SSM_GROUPS, SSM_STATE).swapaxes(0, 1)

        def step(carry, blk):
            hr, hi = _s5_scan_block(blk[0], blk[1], carry[0], carry[1], abar_re, abar_im)
            return (hr[:, -1], hi[:, -1]), (hr, hi)

        (last_re, last_im), (h_re, h_im) = lax.scan(
            step, (h0_re, h0_im), (to_chunks(bu_re), to_chunks(bu_im)))
        h_re = h_re.swapaxes(0, 1).reshape(n, l, N_SSM_GROUPS, SSM_STATE)
        h_im = h_im.swapaxes(0, 1).reshape(n, l, N_SSM_GROUPS, SSM_STATE)
    else:
        h_re, h_im = _s5_scan_block(bu_re, bu_im, h0_re, h0_im, abar_re, abar_im)
        last_re, last_im = h_re[:, -1], h_im[:, -1]
    c_re = w['ssm_c_re'].astype(jnp.float32)
    c_im = w['ssm_c_im'].astype(jnp.float32)
    d = w['ssm_d'].astype(jnp.float32).reshape(N_SSM_GROUPS, SSM_GROUP)
    y = (jnp.einsum('nlgp,gcp->nlgc', h_re, c_re) - jnp.einsum('nlgp,gcp->nlgc', h_im, c_im)
         + d * ug).reshape(n, l, D_SSM)
    z = jax.nn.gelu(y)
    out = z * jax.nn.sigmoid(z @ w['w_glu'].astype(jnp.float32) + w['b_glu'].astype(jnp.float32))
    return out.astype(u.dtype), last_re, last_im


def _moba_prompt(q, k, v):
    b, l, h, dh = q.shape
    nb = -(-l // MOBA_BLOCK)
    ksel = min(MOBA_TOPK, nb)
    pad = nb * MOBA_BLOCK - l
    kp = jnp.pad(k, ((0, 0), (0, pad), (0, 0), (0, 0)))
    vp = jnp.pad(v, ((0, 0), (0, pad), (0, 0), (0, 0)))
    kb = kp.reshape(b, nb, MOBA_BLOCK, h, dh)
    means = kb.astype(jnp.float32).sum(axis=2) / MOBA_BLOCK
    kb = kb.transpose(0, 3, 1, 2, 4)
    vb = vp.reshape(b, nb, MOBA_BLOCK, h, dh).transpose(0, 3, 1, 2, 4)
    slopes = _alibi_slopes(h)
    scale = dh ** -0.5
    b_ix = jnp.arange(b)[:, None, None, None]
    h_ix = jnp.arange(h)[None, :, None, None]
    r = jnp.arange(MOBA_BLOCK)

    def chunk(c):
        start = c * MOBA_QCHUNK
        qf = lax.dynamic_slice_in_dim(q, start, MOBA_QCHUNK, axis=1).astype(jnp.float32)
        t = start + jnp.arange(MOBA_QCHUNK)
        j = start // MOBA_BLOCK
        gate = jnp.einsum('bqhd,bnhd->bhqn', qf, means)
        gate = jnp.where(jnp.arange(nb) < j, gate, NEG_INF)
        _, idx = lax.top_k(gate, ksel)
        valid = idx < j
        k_sel = kb[b_ix, h_ix, idx].astype(jnp.float32)
        v_sel = vb[b_ix, h_ix, idx].astype(jnp.float32)
        pos = idx[..., None] * MOBA_BLOCK + r
        dist = (t[:, None, None] - pos).astype(jnp.float32)
        lp = (jnp.einsum('bqhd,bhqrkd->bhqrk', qf, k_sel) * scale
              - slopes[:, None, None, None] * dist)
        lp = jnp.where(valid[..., None], lp, NEG_INF).reshape(b, h, MOBA_QCHUNK, ksel * MOBA_BLOCK)
        k_own = lax.dynamic_slice_in_dim(kp, j * MOBA_BLOCK, MOBA_BLOCK, axis=1).astype(jnp.float32)
        v_own = lax.dynamic_slice_in_dim(vp, j * MOBA_BLOCK, MOBA_BLOCK, axis=1).astype(jnp.float32)
        d_own = (t[:, None] - (j * MOBA_BLOCK + r)[None, :]).astype(jnp.float32)
        lo = jnp.einsum('bqhd,bkhd->bhqk', qf, k_own) * scale - slopes[:, None, None] * d_own
        lo = jnp.where(d_own >= 0, lo, NEG_INF)
        p = jax.nn.softmax(jnp.concatenate([lp, lo], axis=-1), axis=-1)
        p_past, p_own = p[..., :ksel * MOBA_BLOCK], p[..., ksel * MOBA_BLOCK:]
        out = (jnp.einsum('bhqr,bhqrd->bqhd', p_past,
                          v_sel.reshape(b, h, MOBA_QCHUNK, ksel * MOBA_BLOCK, dh))
               + jnp.einsum('bhqk,bkhd->bqhd', p_own, v_own))
        return out.astype(q.dtype)

    outs = lax.map(chunk, jnp.arange(l // MOBA_QCHUNK))
    return outs.transpose(1, 0, 2, 3, 4).reshape(b, l, h, dh)


def _gather_paged_rows(pos, new_rows, pool, page_table, layer):
    n, h = pos.shape[:2]
    lq = new_rows.shape[1]
    s = jnp.clip(pos, 0, PAST_LEN + lq - 1)
    b_ix = jnp.arange(n)[:, None, None, None]
    h_ix = jnp.arange(h)[None, :, None, None]
    phys = page_table[b_ix, jnp.minimum(s, PAST_LEN - 1) // PAGE_SIZE]
    from_cache = pool[layer, phys, s % PAGE_SIZE, h_ix]
    from_new = new_rows[b_ix, jnp.clip(s - PAST_LEN, 0, lq - 1), h_ix]
    return jnp.where((s < PAST_LEN)[..., None], from_cache, from_new).astype(jnp.float32)


def _moba_sample(q, k_new, v_new, cache_k, cache_v, page_table, layer):
    n, lq, h, dh = q.shape
    total = PAST_LEN + lq
    nb = -(-total // MOBA_BLOCK)
    ksel = min(MOBA_TOPK, nb)
    n_pages = PAST_LEN // PAGE_SIZE
    page_sums = cache_k.astype(jnp.float32).sum(axis=2)[layer]
    seq_page_sums = page_sums[page_table]
    page_block = (jnp.arange(n_pages) * PAGE_SIZE) // MOBA_BLOCK
    t = PAST_LEN + jnp.arange(lq)
    sums = (jax.ops.segment_sum(seq_page_sums.swapaxes(0, 1), page_block, num_segments=nb)
            + jax.ops.segment_sum(k_new.astype(jnp.float32).swapaxes(0, 1), t // MOBA_BLOCK,
                                  num_segments=nb))
    means = sums.swapaxes(0, 1) / MOBA_BLOCK
    qf = q.astype(jnp.float32)
    j = t // MOBA_BLOCK
    gate = jnp.einsum('bqhd,bnhd->bhqn', qf, means)
    gate = jnp.where(jnp.arange(nb)[None, :] < j[:, None], gate, NEG_INF)
    _, idx = lax.top_k(gate, ksel)
    r = jnp.arange(MOBA_BLOCK)
    pos_p = (idx[..., None] * MOBA_BLOCK + r).reshape(n, h, lq, ksel * MOBA_BLOCK)
    valid_p = jnp.broadcast_to((idx < j[:, None])[..., None],
                               (n, h, lq, ksel, MOBA_BLOCK)).reshape(n, h, lq, ksel * MOBA_BLOCK)
    pos_o = jnp.broadcast_to(j[:, None] * MOBA_BLOCK + r, (n, h, lq, MOBA_BLOCK))
    valid_o = pos_o <= t[:, None]
    pos = jnp.concatenate([pos_p, pos_o], axis=-1)
    valid = jnp.concatenate([valid_p, valid_o], axis=-1)
    k_rows = _gather_paged_rows(pos, k_new, cache_k, page_table, layer)
    v_rows = _gather_paged_rows(pos, v_new, cache_v, page_table, layer)
    dist = (t[:, None] - pos).astype(jnp.float32)
    logits = (jnp.einsum('bqhd,bhqrd->bhqr', qf, k_rows) * (dh ** -0.5)
              - _alibi_slopes(h)[:, None, None] * dist)
    p = jax.nn.softmax(jnp.where(valid, logits, NEG_INF), axis=-1)
    return jnp.einsum('bhqr,bhqrd->bqhd', p, v_rows).astype(q.dtype)


def _mem_kv(mem, w):
    n, m, _ = mem.shape
    hm = _rmsnorm(mem, w['g_mem'])
    shape = (n, m, N_XHEADS, XHEAD_DIM)
    return (hm @ w['wk_x']).reshape(shape), (hm @ w['wv_x']).reshape(shape)


def _cross_attn(h, mem_k, mem_v, wq_x, wo_x):
    n, l, _ = h.shape
    q = (h @ wq_x).reshape(n, l, N_XHEADS, XHEAD_DIM).astype(jnp.float32)
    s = jnp.einsum('blhd,bmhd->bhlm', q, mem_k.astype(jnp.float32)) * (XHEAD_DIM ** -0.5)
    p = jax.nn.softmax(s, axis=-1)
    o = jnp.einsum('bhlm,bmhd->blhd', p, mem_v.astype(jnp.float32)).reshape(n, l, D_MODEL)
    return o.astype(h.dtype) @ wo_x


def _layer(x, h0_re, h0_im, mem_k, mem_v, moba_fn, w):
    n, l, _ = x.shape
    x = x + 0.5 * _swiglu(_rmsnorm(x, w['g_ffn1']), w['w1_ffn1'], w['w3_ffn1'], w['w2_ffn1'])
    proj = _rmsnorm(x, w['g_mix']) @ w['w_in']
    u = proj[..., :D_SSM]
    q, k, v = jnp.split(proj[..., D_SSM:], 3, axis=-1)
    heads = (n, l, N_ATTN_HEADS, ATTN_HEAD_DIM)
    q, k, v = q.reshape(heads), k.reshape(heads), v.reshape(heads)
    y_ssm, h_re, h_im = _s5_mixer(u, h0_re, h0_im, w)
    y_att = moba_fn(q, k, v).reshape(n, l, D_ATTN)
    merged = jnp.concatenate([_rmsnorm(y_ssm, w['g_out_ssm']), _rmsnorm(y_att, w['g_out_attn'])], axis=-1)
    x = x + merged @ w['w_out']
    x = x + _cross_attn(_rmsnorm(x, w['g_xattn']), mem_k, mem_v, w['wq_x'], w['wo_x'])
    x = x + 0.5 * _swiglu(_rmsnorm(x, w['g_ffn2']), w['w1_ffn2'], w['w3_ffn2'], w['w2_ffn2'])
    return x, k, v, h_re, h_im


def setup_inputs(seed: int = 0) -> dict:
    key = jax.random.key(seed)
    keys = jax.random.split(key, 48)
    n_pages = PAST_LEN // PAGE_SIZE
    n_used = DEC_BATCH * n_pages
    n_pool = n_used + max(1, n_used // 4)
    G, P, C = N_SSM_GROUPS, SSM_STATE, SSM_GROUP

    def nrm(i, shape, scale):
        return scale * jax.random.normal(keys[i], shape, jnp.float32)

    def gain(i, shape):
        return 1.0 + 0.01 * jax.random.normal(keys[i], shape, jnp.float32)

    return {
        'x_prompt': nrm(0, (BATCH, SEQ, D_MODEL), 1.0),
        'x_sample': nrm(1, (DEC_BATCH, DEC_SEQ, D_MODEL), 1.0),
        'mem_prompt': nrm(2, (BATCH, N_MEM, D_MODEL), 1.0),
        'cache_k': nrm(3, (DEPTH, n_pool, PAGE_SIZE, N_ATTN_HEADS, ATTN_HEAD_DIM), 1.0),
        'cache_v': nrm(4, (DEPTH, n_pool, PAGE_SIZE, N_ATTN_HEADS, ATTN_HEAD_DIM), 1.0),
        'page_table': jax.random.permutation(keys[5], n_pool)[:n_used].reshape(DEC_BATCH, n_pages).astype(jnp.int32),
        'state_ssm_re': nrm(6, (DEPTH, DEC_BATCH, G, P), 0.1),
        'state_ssm_im': nrm(7, (DEPTH, DEC_BATCH, G, P), 0.1),
        'cache_mem_k': nrm(8, (DEPTH, DEC_BATCH, N_MEM, N_XHEADS, XHEAD_DIM), 1.0),
        'cache_mem_v': nrm(9, (DEPTH, DEC_BATCH, N_MEM, N_XHEADS, XHEAD_DIM), 1.0),
        'g_ffn1': gain(10, (DEPTH, D_MODEL)),
        'w1_ffn1': nrm(11, (DEPTH, D_MODEL, D_FF), D_MODEL ** -0.5),
        'w3_ffn1': nrm(12, (DEPTH, D_MODEL, D_FF), D_MODEL ** -0.5),
        'w2_ffn1': nrm(13, (DEPTH, D_FF, D_MODEL), D_FF ** -0.5),
        'g_mix': gain(14, (DEPTH, D_MODEL)),
        'w_in': nrm(15, (DEPTH, D_MODEL, D_PROJ), D_MODEL ** -0.5),
        'ssm_a_re': -0.5 + nrm(16, (DEPTH, G, P), 0.01),
        'ssm_a_im': math.pi * jnp.arange(P, dtype=jnp.float32) + nrm(17, (DEPTH, G, P), 0.01),
        'ssm_log_dt': jax.random.uniform(keys[18], (DEPTH, G), jnp.float32, math.log(1e-3), math.log(1e-1)),
        'ssm_b_re': nrm(19, (DEPTH, G, P, C), (2 * C) ** -0.5),
        'ssm_b_im': nrm(20, (DEPTH, G, P, C), (2 * C) ** -0.5),
        'ssm_c_re': nrm(21, (DEPTH, G, C, P), P ** -0.5),
        'ssm_c_im': nrm(22, (DEPTH, G, C, P), P ** -0.5),
        'ssm_d': nrm(23, (DEPTH, D_SSM), 1.0),
        'w_glu': nrm(24, (DEPTH, D_SSM, D_SSM), D_SSM ** -0.5),
        'b_glu': nrm(25, (DEPTH, D_SSM), 0.01),
        'g_out_ssm': gain(26, (DEPTH, D_SSM)),
        'g_out_attn': gain(27, (DEPTH, D_ATTN)),
        'w_out': nrm(28, (DEPTH, D_MIX, D_MODEL), D_MIX ** -0.5),
        'g_xattn': gain(29, (DEPTH, D_MODEL)),
        'g_mem': gain(30, (DEPTH, D_MODEL)),
        'wq_x': nrm(31, (DEPTH, D_MODEL, D_MODEL), D_MODEL ** -0.5),
        'wk_x': nrm(32, (DEPTH, D_MODEL, D_MODEL), D_MODEL ** -0.5),
        'wv_x': nrm(33, (DEPTH, D_MODEL, D_MODEL), D_MODEL ** -0.5),
        'wo_x': nrm(34, (DEPTH, D_MODEL, D_MODEL), D_MODEL ** -0.5),
        'g_ffn2': gain(35, (DEPTH, D_MODEL)),
        'w1_ffn2': nrm(36, (DEPTH, D_MODEL, D_FF), D_MODEL ** -0.5),
        'w3_ffn2': nrm(37, (DEPTH, D_MODEL, D_FF), D_MODEL ** -0.5),
        'w2_ffn2': nrm(38, (DEPTH, D_FF, D_MODEL), D_FF ** -0.5),
        'g_final': gain(39, (D_MODEL,)),
    }


def reference(x_prompt, x_sample, mem_prompt, cache_k, cache_v, page_table, state_ssm_re, state_ssm_im,
              cache_mem_k, cache_mem_v, g_ffn1, w1_ffn1, w3_ffn1, w2_ffn1, g_mix, w_in,
              ssm_a_re, ssm_a_im, ssm_log_dt, ssm_b_re, ssm_b_im, ssm_c_re, ssm_c_im, ssm_d,
              w_glu, b_glu, g_out_ssm, g_out_attn, w_out, g_xattn, g_mem, wq_x, wk_x, wv_x, wo_x,
              g_ffn2, w1_ffn2, w3_ffn2, w2_ffn2, g_final):
    stacked = dict(g_ffn1=g_ffn1, w1_ffn1=w1_ffn1, w3_ffn1=w3_ffn1, w2_ffn1=w2_ffn1, g_mix=g_mix,
                   w_in=w_in, ssm_a_re=ssm_a_re, ssm_a_im=ssm_a_im, ssm_log_dt=ssm_log_dt,
                   ssm_b_re=ssm_b_re, ssm_b_im=ssm_b_im, ssm_c_re=ssm_c_re, ssm_c_im=ssm_c_im,
                   ssm_d=ssm_d, w_glu=w_glu, b_glu=b_glu, g_out_ssm=g_out_ssm, g_out_attn=g_out_attn,
                   w_out=w_out, g_xattn=g_xattn, g_mem=g_mem, wq_x=wq_x, wk_x=wk_x, wv_x=wv_x,
                   wo_x=wo_x, g_ffn2=g_ffn2, w1_ffn2=w1_ffn2, w3_ffn2=w3_ffn2, w2_ffn2=w2_ffn2)
    xp, xs = x_prompt, x_sample
    k_p, v_p, re_p, im_p, mk_p, mv_p = [], [], [], [], [], []
    k_s, v_s, re_s, im_s = [], [], [], []
    for layer in range(DEPTH):
        w = {name: arr[layer] for name, arr in stacked.items()}
        mem_k, mem_v = _mem_kv(mem_prompt, w)
        zero_state = jnp.zeros((xp.shape[0], N_SSM_GROUPS, SSM_STATE), jnp.float32)
        xp, kk, vv, hr, hi = _layer(xp, zero_state, zero_state, mem_k, mem_v, _moba_prompt, w)
        k_p.append(kk); v_p.append(vv); re_p.append(hr); im_p.append(hi)
        mk_p.append(mem_k); mv_p.append(mem_v)
        moba_sample = functools.partial(_moba_sample, cache_k=cache_k, cache_v=cache_v,
                                        page_table=page_table, layer=layer)
        xs, kk, vv, hr, hi = _layer(xs, state_ssm_re[layer], state_ssm_im[layer],
                                    cache_mem_k[layer], cache_mem_v[layer], moba_sample, w)
        k_s.append(kk); v_s.append(vv); re_s.append(hr); im_s.append(hi)
    y_prompt = _rmsnorm(xp, g_final)
    y_sample = _rmsnorm(xs, g_final)
    return (y_prompt, y_sample, jnp.stack(k_p), jnp.stack(v_p), jnp.stack(re_p), jnp.stack(im_p),
            jnp.stack(mk_p), jnp.stack(mv_p), jnp.stack(k_s), jnp.stack(v_s), jnp.stack(re_s),
            jnp.stack(im_s))
```

```python
import functools
import math

import jax
import jax.numpy as jnp
from jax import lax
from jax.experimental import pallas as pl
from jax.experimental.pallas import tpu as pltpu

RMS_EPS = 1e-6
NEG_INF = -1e30
SSM_GROUP = 16
ATTN_HEAD_DIM = 64
MOBA_BLOCK = 256
MOBA_TOPK = 3
N_XHEADS = 4
LANES = 128
SUBLANES = 8
VMEM_LIMIT = 56 * 1024 * 1024

F32 = jnp.float32
BF16 = jnp.bfloat16
HIGHEST = lax.Precision.HIGHEST


def _params(*sem):
    return pltpu.CompilerParams(dimension_semantics=sem, vmem_limit_bytes=VMEM_LIMIT)


def _rms(x, g):
    ms = jnp.mean(x * x, axis=-1, keepdims=True)
    return x * lax.rsqrt(ms + RMS_EPS) * g


def _dot(a, b):
    return jnp.dot(a, b, preferred_element_type=F32)


def _dot_t(a, b, precision=None):
    return lax.dot_general(a, b, (((1,), (1,)), ((), ())), preferred_element_type=F32,
                           precision=precision)


def _full(shape):
    n = len(shape)
    return pl.BlockSpec(shape, lambda *_: (0,) * n)


def _ffn_body(x_ref, g_ref, w1_ref, w3_ref, w2_ref, gf_ref, o_ref, *, f_chunk, final_norm):
    x = x_ref[...]
    h = _rms(x, g_ref[...]).astype(BF16)
    acc = jnp.zeros(x.shape, F32)
    for c in range(w1_ref.shape[1] // f_chunk):
        cs = slice(c * f_chunk, (c + 1) * f_chunk)
        a = _dot(h, w1_ref[:, cs])
        b = _dot(h, w3_ref[:, cs])
        act = (a * jax.nn.sigmoid(a) * b).astype(BF16)
        acc = acc + _dot(act, w2_ref[cs, :])
    y = x + 0.5 * acc
    if final_norm:
        y = _rms(y, gf_ref[...])
    o_ref[...] = y


def _ffn(x, g, w1, w3, w2, g_final, *, final_norm, tm):
    n, d = x.shape
    f = w1.shape[1]
    f_chunk = 256 if f % 256 == 0 else f
    return pl.pallas_call(
        functools.partial(_ffn_body, f_chunk=f_chunk, final_norm=final_norm),
        grid=(n // tm,),
        in_specs=[pl.BlockSpec((tm, d), lambda i: (i, 0)), _full((1, d)), _full((d, f)), _full((d, f)),
                  _full((f, d)), _full((1, d))],
        out_specs=pl.BlockSpec((tm, d), lambda i: (i, 0)),
        out_shape=jax.ShapeDtypeStruct((n, d), F32),
        compiler_params=_params("parallel"),
        name="ffn_final" if final_norm else "ffn",
    )(x, g, w1, w3, w2, g_final)


def _proj_body(x_ref, g_ref, w_ref, u_ref, q_ref, k_ref, v_ref, kb_ref, vb_ref, *, d_ssm, d_attn):
    h = _rms(x_ref[...], g_ref[...]).astype(BF16)
    p = _dot(h, w_ref[...])
    u_ref[...] = p[:, :d_ssm]
    q_ref[...] = p[:, d_ssm:d_ssm + d_attn]
    k = p[:, d_ssm + d_attn:d_ssm + 2 * d_attn]
    v = p[:, d_ssm + 2 * d_attn:]
    k_ref[...] = k
    v_ref[...] = v
    kb_ref[...] = k.astype(BF16)
    vb_ref[...] = v.astype(BF16)


def _proj(x, g, w_in, *, d_ssm, tm):
    n, d = x.shape
    d_attn = (w_in.shape[1] - d_ssm) // 3
    row = lambda w: pl.BlockSpec((tm, w), lambda i: (i, 0))
    return pl.pallas_call(
        functools.partial(_proj_body, d_ssm=d_ssm, d_attn=d_attn),
        grid=(n // tm,),
        in_specs=[row(d), _full((1, d)), _full(w_in.shape)],
        out_specs=[row(d_ssm), row(d_attn), row(d_attn), row(d_attn), row(d_attn), row(d_attn)],
        out_shape=[jax.ShapeDtypeStruct((n, d_ssm), F32)] + [jax.ShapeDtypeStruct((n, d_attn), F32)] * 3
                  + [jax.ShapeDtypeStruct((n, d_attn), BF16)] * 2,
        compiler_params=_params("parallel"),
        name="mix_proj",
    )(x, g, w_in)


def _s5_discretise(a_re, a_im, log_dt, b_re, b_im):
    dt = jnp.exp(log_dt)[:, None]
    mag = jnp.exp(a_re * dt)
    abar_re = mag * jnp.cos(a_im * dt)
    abar_im = mag * jnp.sin(a_im * dt)
    den = a_re * a_re + a_im * a_im
    nr = abar_re - 1.0
    f_re = (nr * a_re + abar_im * a_im) / den
    f_im = (abar_im * a_re - nr * a_im) / den
    bbar_re = f_re[..., None] * b_re - f_im[..., None] * b_im
    bbar_im = f_re[..., None] * b_im + f_im[..., None] * b_re
    return abar_re, abar_im, bbar_re, bbar_im


def _s5_matrices(a_re, a_im, log_dt, b_re, b_im, c_re, c_im):
    g, p, c = b_re.shape
    abar_re, abar_im, bbar_re, bbar_im = _s5_discretise(a_re, a_im, log_dt, b_re, b_im)
    eye = jnp.eye(g, dtype=F32)
    bm_re = jnp.einsum('gpc,gh->gchp', bbar_re, eye).reshape(g * c, g * p)
    bm_im = jnp.einsum('gpc,gh->gchp', bbar_im, eye).reshape(g * c, g * p)
    bmat = jnp.concatenate([bm_re, bm_im], axis=1).astype(BF16)
    cm_re = jnp.einsum('gcp,gh->gphc', c_re, eye).reshape(g * p, g * c)
    cm_im = jnp.einsum('gcp,gh->gphc', c_im, eye).reshape(g * p, g * c)
    cmat = jnp.concatenate([cm_re, -cm_im], axis=0).astype(BF16)
    return abar_re.reshape(1, g * p), abar_im.reshape(1, g * p), bmat, cmat


def _s5_output(h_bf, u, cmat_ref, d_ref, wg_ref, bg_ref):
    y = _dot(h_bf, cmat_ref[...]) + d_ref[...] * u
    z = jax.nn.gelu(y, approximate=True)
    return z * jax.nn.sigmoid(_dot(z.astype(BF16), wg_ref[...]) + bg_ref[...])


def _s5_prompt_body(u_ref, ar_ref, ai_ref, bmat_ref, cmat_ref, d_ref, wg_ref, bg_ref,
                    y_ref, hlast_ref, hbuf, carry, *, nbatch):
    gp = ar_ref.shape[1]
    rows = u_ref.shape[0]

    @pl.when(pl.program_id(0) == 0)
    def _():
        carry[...] = jnp.zeros(carry.shape, F32)

    u = u_ref[...]
    hbuf[...] = _dot(u.astype(BF16), bmat_ref[...])

    ar = jnp.broadcast_to(ar_ref[...], (SUBLANES, gp))
    ai = jnp.broadcast_to(ai_ref[...], (SUBLANES, gp))
    lower = lax.broadcasted_iota(jnp.int32, (SUBLANES, gp), 0) < nbatch

    def step(i, st):
        hr, hi = st
        r0 = pl.multiple_of(i * SUBLANES, SUBLANES)
        bur = hbuf[pl.ds(r0, SUBLANES), :gp]
        bui = hbuf[pl.ds(r0, SUBLANES), gp:]
        pr = pltpu.roll(hr, nbatch, 0)
        pi_ = pltpu.roll(hi, nbatch, 0)
        lr = ar * pr - ai * pi_ + bur
        li = ar * pi_ + ai * pr + bui
        qr = pltpu.roll(lr, nbatch, 0)
        qi = pltpu.roll(li, nbatch, 0)
        ur = ar * qr - ai * qi + bur
        ui = ar * qi + ai * qr + bui
        hr = jnp.where(lower, lr, ur)
        hi = jnp.where(lower, li, ui)
        hbuf[pl.ds(r0, SUBLANES), :gp] = hr
        hbuf[pl.ds(r0, SUBLANES), gp:] = hi
        return hr, hi

    hr, hi = lax.fori_loop(0, rows // SUBLANES, step, (carry[:, :gp], carry[:, gp:]))
    carry[:, :gp] = hr
    carry[:, gp:] = hi
    hlast_ref[:, :gp] = hr
    hlast_ref[:, gp:] = hi
    y_ref[...] = _s5_output(hbuf[...].astype(BF16), u, cmat_ref, d_ref, wg_ref, bg_ref)


def _s5_prompt(u_tm, nbatch, ar, ai, bmat, cmat, d, w_glu, b_glu, *, t_chunk):
    rows_total, d_ssm = u_tm.shape
    gp = ar.shape[1]
    rows = t_chunk * nbatch
    assert 2 * nbatch == SUBLANES and rows_total % rows == 0
    return pl.pallas_call(
        functools.partial(_s5_prompt_body, nbatch=nbatch),
        grid=(rows_total // rows,),
        in_specs=[pl.BlockSpec((rows, d_ssm), lambda i: (i, 0)), _full((1, gp)), _full((1, gp)),
                  _full(bmat.shape), _full(cmat.shape), _full((1, d_ssm)), _full(w_glu.shape),
                  _full((1, d_ssm))],
        out_specs=[pl.BlockSpec((rows, d_ssm), lambda i: (i, 0)), _full((SUBLANES, 2 * gp))],
        out_shape=[jax.ShapeDtypeStruct((rows_total, d_ssm), F32),
                   jax.ShapeDtypeStruct((SUBLANES, 2 * gp), F32)],
        scratch_shapes=[pltpu.VMEM((rows, 2 * gp), F32), pltpu.VMEM((SUBLANES, 2 * gp), F32)],
        compiler_params=_params("arbitrary"),
        name="s5_prompt",
    )(u_tm, ar, ai, bmat, cmat, d, w_glu, b_glu)


def _s5_sample_body(u_ref, h0r_ref, h0i_ref, ar_ref, ai_ref, bmat_ref, cmat_ref, d_ref, wg_ref, bg_ref,
                    y_ref, hr_ref, hi_ref):
    gp = ar_ref.shape[1]
    u = u_ref[...]
    bu = _dot(u.astype(BF16), bmat_ref[...])
    ar, ai = ar_ref[...], ai_ref[...]
    h0r, h0i = h0r_ref[...], h0i_ref[...]
    hr = bu[:, :gp] + ar * h0r - ai * h0i
    hi = bu[:, gp:] + ar * h0i + ai * h0r
    hr_ref[...] = hr
    hi_ref[...] = hi
    h_bf = jnp.concatenate([hr, hi], axis=1).astype(BF16)
    y_ref[...] = _s5_output(h_bf, u, cmat_ref, d_ref, wg_ref, bg_ref)


def _s5_sample(u, h0r, h0i, ar, ai, bmat, cmat, d, w_glu, b_glu):
    n, d_ssm = u.shape
    gp = ar.shape[1]
    args = (u, h0r, h0i, ar, ai, bmat, cmat, d, w_glu, b_glu)
    return pl.pallas_call(
        _s5_sample_body,
        grid=(1,),
        in_specs=[_full(a.shape) for a in args],
        out_specs=[_full((n, d_ssm)), _full((n, gp)), _full((n, gp))],
        out_shape=[jax.ShapeDtypeStruct((n, d_ssm), F32), jax.ShapeDtypeStruct((n, gp), F32),
                   jax.ShapeDtypeStruct((n, gp), F32)],
        compiler_params=_params("arbitrary"),
        name="s5_sample",
    )(*args)


def _block_sum_body(k_ref, o_ref, *, blocks):
    for i in range(blocks):
        o_ref[0, i:i + 1, :] = jnp.sum(k_ref[0, i * MOBA_BLOCK:(i + 1) * MOBA_BLOCK, :], axis=0,
                                       keepdims=True)


def _block_means(k):
    b, l, d = k.shape
    nb = l // MOBA_BLOCK
    blocks = SUBLANES if nb % SUBLANES == 0 else nb
    sums = pl.pallas_call(
        functools.partial(_block_sum_body, blocks=blocks),
        grid=(b, nb // blocks),
        in_specs=[pl.BlockSpec((1, blocks * MOBA_BLOCK, d), lambda i, j: (i, j, 0))],
        out_specs=pl.BlockSpec((1, blocks, d), lambda i, j: (i, j, 0)),
        out_shape=jax.ShapeDtypeStruct((b, nb, d), F32),
        compiler_params=_params("parallel", "parallel"),
        name="moba_block_sums",
    )(k)
    return sums


def _top_blocks(gate, bidx, n_valid):
    nb = gate.shape[1]
    bf = bidx.astype(F32)
    g = jnp.where(bidx < n_valid, gate, NEG_INF)
    sel = jnp.zeros(gate.shape, jnp.bool_)
    for _ in range(MOBA_TOPK):
        mx = jnp.max(g, axis=1, keepdims=True)
        first = jnp.min(jnp.where(g == mx, bf, float(nb)), axis=1, keepdims=True)
        pick = bf == first
        sel = jnp.logical_or(sel, pick)
        g = jnp.where(pick, -jnp.inf, g)
    return jnp.logical_and(sel, bidx < n_valid)


def _moba_prompt_body(q_ref, kb_ref, vb_ref, sums_ref, o_ref, *, n_heads):
    j = pl.program_id(1)
    blk = MOBA_BLOCK
    dh = ATTN_HEAD_DIM
    nb = sums_ref.shape[1]
    t0 = pl.multiple_of(j * blk, blk)
    row = lax.broadcasted_iota(jnp.int32, (blk, 1), 0)
    col = lax.broadcasted_iota(jnp.int32, (1, blk), 1)
    colf = col.astype(F32)
    bidx = lax.broadcasted_iota(jnp.int32, (1, nb), 1)
    causal = col <= row
    scale = dh ** -0.5
    outs = []
    for h in range(n_heads):
        hs = slice(h * dh, (h + 1) * dh)
        slope = 2.0 ** (-8.0 * (h + 1) / n_heads)
        qh = q_ref[0, :, hs]
        means = sums_ref[0, :, hs] * (1.0 / blk)
        gate = _dot_t(qh, means, precision=HIGHEST)
        sel = _top_blocks(gate, bidx, j).astype(F32)
        qs = (qh * scale).astype(BF16)

        def scores(n):
            r = pl.multiple_of(n * blk, blk)
            s = _dot_t(qs, kb_ref[0, pl.ds(r, blk), hs])
            return s + slope * (colf + (n - j).astype(F32) * blk), vb_ref[0, pl.ds(r, blk), hs]

        s, v_own = scores(j)
        s = jnp.where(causal, s, NEG_INF)
        m = jnp.max(s, axis=1, keepdims=True)
        p = jnp.exp(s - m)
        l = jnp.sum(p, axis=1, keepdims=True)
        acc = _dot(p.astype(BF16), v_own)

        def past(n, st):
            m, l, acc = st
            s, vn = scores(n)
            picked = jnp.sum(jnp.where(bidx == n, sel, 0.0), axis=1, keepdims=True) > 0.0
            s = jnp.where(picked, s, NEG_INF)
            m_new = jnp.maximum(m, jnp.max(s, axis=1, keepdims=True))
            alpha = jnp.exp(m - m_new)
            p = jnp.exp(s - m_new)
            l = alpha * l + jnp.sum(p, axis=1, keepdims=True)
            acc = alpha * acc + _dot(p.astype(BF16), vn)
            return m_new, l, acc

        m, l, acc = lax.fori_loop(0, j, past, (m, l, acc))
        outs.append(acc / l)
    o_ref[0] = jnp.concatenate(outs, axis=1)


def _moba_prompt(q, kb, vb, sums):
    b, l, d = q.shape
    nb = l // MOBA_BLOCK
    n_heads = d // ATTN_HEAD_DIM
    return pl.pallas_call(
        functools.partial(_moba_prompt_body, n_heads=n_heads),
        grid=(b, nb),
        in_specs=[pl.BlockSpec((1, MOBA_BLOCK, d), lambda i, j: (i, j, 0)),
                  pl.BlockSpec((1, l, d), lambda i, j: (i, 0, 0)),
                  pl.BlockSpec((1, l, d), lambda i, j: (i, 0, 0)),
                  pl.BlockSpec((1, nb, d), lambda i, j: (i, 0, 0))],
        out_specs=pl.BlockSpec((1, MOBA_BLOCK, d), lambda i, j: (i, j, 0)),
        out_shape=jax.ShapeDtypeStruct((b, l, d), F32),
        compiler_params=_params("parallel", "arbitrary"),
        name="moba_prompt",
    )(q, kb, vb, sums)


def _moba_kpass_body(pt_ref, q_ref, *refs, pages, n_heads):
    del pt_ref
    k_refs, (lg_ref, ps_ref) = refs[:pages], refs[pages:]
    s = pl.program_id(1)
    dh = ATTN_HEAD_DIM
    page = k_refs[0].shape[2]

    @pl.when(s == 0)
    def _():
        ps_ref[...] = jnp.zeros(ps_ref.shape, F32)

    qb = q_ref[0].astype(BF16)
    rows = page * n_heads
    own = (lax.broadcasted_iota(jnp.int32, (n_heads, rows), 1) % n_heads
           == lax.broadcasted_iota(jnp.int32, (n_heads, rows), 0))
    plane = lax.broadcasted_iota(jnp.int32, (n_heads, ps_ref.shape[2]), 1)
    psum = ps_ref[0]
    for i in range(pages):
        kp = k_refs[i][0, 0].reshape(rows, dh).astype(BF16)
        cross = jnp.where(own, _dot_t(qb, kp), 0.0)
        lg_ref[0, :, i * rows:(i + 1) * rows] = jnp.sum(cross, axis=0, keepdims=True)
        psum = jnp.where(plane == s * pages + i, jnp.sum(cross, axis=1, keepdims=True), psum)
    ps_ref[0] = psum


def _moba_kpass(q3, cache_k, page_table, *, pages):
    n = q3.shape[0]
    n_pages = page_table.shape[1]
    _, _, page, n_heads, dh = cache_k.shape
    rows = page * n_heads
    k_spec = lambda i: pl.BlockSpec((1, 1, page, n_heads, dh),
                                    lambda b, s, pt: (0, pt[b, s * pages + i], 0, 0, 0))
    grid_spec = pltpu.PrefetchScalarGridSpec(
        num_scalar_prefetch=1,
        grid=(n, n_pages // pages),
        in_specs=[pl.BlockSpec((1, n_heads, dh), lambda b, s, pt: (b, 0, 0))]
                 + [k_spec(i) for i in range(pages)],
        out_specs=[pl.BlockSpec((1, 1, pages * rows), lambda b, s, pt: (b, 0, s)),
                   pl.BlockSpec((1, n_heads, n_pages), lambda b, s, pt: (b, 0, 0))])
    return pl.pallas_call(
        functools.partial(_moba_kpass_body, pages=pages, n_heads=n_heads),
        grid_spec=grid_spec,
        out_shape=[jax.ShapeDtypeStruct((n, 1, n_pages * rows), F32),
                   jax.ShapeDtypeStruct((n, n_heads, n_pages), F32)],
        compiler_params=_params("parallel", "arbitrary"),
        name="moba_sample_kpass",
    )(page_table, q3, *([cache_k] * pages))


def _moba_select_body(ps_ref, idx_ref, *, pages_per_block, n_blocks):
    ps = ps_ref[...]
    lanes = ps.shape[1]
    lane = lax.broadcasted_iota(jnp.int32, (1, lanes), 1)
    g = ps
    for i in range(1, pages_per_block):
        g = g + pltpu.roll(ps, lanes - i, 1)
    g = g * (1.0 / MOBA_BLOCK)
    is_block = jnp.logical_and(lane % pages_per_block == 0, lane < n_blocks * pages_per_block)
    g = jnp.where(is_block, g, -jnp.inf)
    lanef = lane.astype(F32)
    out = jnp.zeros(idx_ref.shape, F32)
    olane = lax.broadcasted_iota(jnp.int32, idx_ref.shape, 1)
    for r in range(MOBA_TOPK):
        mx = jnp.max(g, axis=1, keepdims=True)
        first = jnp.min(jnp.where(g == mx, lanef, float(lanes)), axis=1, keepdims=True)
        out = jnp.where(olane == r, first * (1.0 / pages_per_block), out)
        g = jnp.where(lanef == first, -jnp.inf, g)
    idx_ref[...] = out.astype(jnp.int32)


def _moba_select(page_sums2, *, pages_per_block, n_blocks):
    rows, lanes = page_sums2.shape
    return pl.pallas_call(
        functools.partial(_moba_select_body, pages_per_block=pages_per_block, n_blocks=n_blocks),
        grid=(1,),
        in_specs=[_full((rows, lanes))],
        out_specs=_full((rows, LANES)),
        out_shape=jax.ShapeDtypeStruct((rows, LANES), jnp.int32),
        compiler_params=_params("arbitrary"),
        name="moba_sample_select",
    )(page_sums2)


def _moba_attend_body(blk_ref, pg_ref, q_ref, kn_ref, vn_ref, *refs, pages_per_block, n_heads, past_len):
    del pg_ref
    n_lg = MOBA_TOPK
    n_v = MOBA_TOPK * pages_per_block
    lg_refs, v_refs, o_ref = refs[:n_lg], refs[n_lg:n_lg + n_v], refs[n_lg + n_v]
    b = pl.program_id(0)
    h = pl.program_id(1)
    dh = ATTN_HEAD_DIM
    d = n_heads * dh
    blk = MOBA_BLOCK
    page = blk // pages_per_block
    scale = dh ** -0.5
    slope = jnp.exp2(-8.0 * (h + 1).astype(F32) / n_heads)
    hmask = lax.broadcasted_iota(jnp.int32, (1, d), 1) // dh == h

    @pl.when(h == 0)
    def _():
        o_ref[...] = jnp.zeros(o_ref.shape, F32)

    s_own = jnp.sum(jnp.where(hmask, q_ref[0] * kn_ref[0], 0.0), axis=1, keepdims=True) * scale
    lane = lax.broadcasted_iota(jnp.int32, (1, blk * n_heads), 1)
    mine = lane % n_heads == h
    ss = []
    for i in range(n_lg):
        pos = blk_ref[b, h, i] * blk + lane // n_heads
        s = lg_refs[i][0] * scale - slope * (past_len - pos).astype(F32)
        ss.append(jnp.where(mine, s, NEG_INF))
    m = s_own
    for s in ss:
        m = jnp.maximum(m, jnp.max(s, axis=1, keepdims=True))
    p_own = jnp.exp(s_own - m)
    l = p_own
    acc = jnp.zeros((1, dh), F32)
    rows = page * n_heads
    for i in range(n_lg):
        p = jnp.exp(ss[i] - m)
        l = l + jnp.sum(p, axis=1, keepdims=True)
        for c in range(pages_per_block):
            vp = v_refs[i * pages_per_block + c][0, 0].reshape(rows, dh).astype(BF16)
            acc = acc + _dot(p[:, c * rows:(c + 1) * rows].astype(BF16), vp)
    tiled = jnp.concatenate([acc] * n_heads, axis=1)
    val = (tiled + p_own * vn_ref[0]) / l
    o_ref[0] = jnp.where(hmask, val, o_ref[0])


def _moba_attend(blocks, pages_idx, q3, kn3, vn3, logits, cache_v, *, pages_per_block, past_len):
    n, _, d = q3.shape
    _, _, page, n_heads, dh = cache_v.shape
    n_blocks = logits.shape[0] // n
    row = pl.BlockSpec((1, 1, d), lambda b, h, bl, pg: (b, 0, 0))
    lg_spec = lambda i: pl.BlockSpec(
        (1, 1, MOBA_BLOCK * n_heads), lambda b, h, bl, pg: (b * n_blocks + bl[b, h, i], 0, 0))
    v_spec = lambda i: pl.BlockSpec(
        (1, 1, page, n_heads, dh), lambda b, h, bl, pg: (0, pg[b, h, i], 0, 0, 0))
    n_v = MOBA_TOPK * pages_per_block
    grid_spec = pltpu.PrefetchScalarGridSpec(
        num_scalar_prefetch=2,
        grid=(n, n_heads),
        in_specs=[row, row, row] + [lg_spec(i) for i in range(MOBA_TOPK)] + [v_spec(i) for i in range(n_v)],
        out_specs=row)
    return pl.pallas_call(
        functools.partial(_moba_attend_body, pages_per_block=pages_per_block, n_heads=n_heads,
                          past_len=past_len),
        grid_spec=grid_spec,
        out_shape=jax.ShapeDtypeStruct((n, 1, d), F32),
        compiler_params=_params("parallel", "arbitrary"),
        name="moba_sample_attend",
    )(blocks, pages_idx, q3, kn3, vn3, *([logits] * MOBA_TOPK), *([cache_v] * n_v))


def _moba_sample(q, k_new, v_new, cache_k, cache_v, page_table):
    n, d = q.shape
    _, n_pool, page, n_heads, dh = cache_k.shape
    n_pages = page_table.shape[1]
    past_len = n_pages * page
    pages_per_block = MOBA_BLOCK // page
    n_blocks = n_pages // pages_per_block
    assert n_blocks >= MOBA_TOPK and n_pages % LANES == 0
    q3 = q.reshape(n, 1, d)
    kpass_pages = 8 if n_pages % 8 == 0 else 1
    logits, page_sums = _moba_kpass(q.reshape(n, n_heads, dh), cache_k, page_table, pages=kpass_pages)
    blocks = _moba_select(page_sums.reshape(n * n_heads, n_pages), pages_per_block=pages_per_block,
                          n_blocks=n_blocks)[:, :MOBA_TOPK].reshape(n, n_heads, MOBA_TOPK)
    logical = (blocks[..., None] * pages_per_block + jnp.arange(pages_per_block)).reshape(n, n_heads, -1)
    pages_idx = jnp.take_along_axis(page_table[:, None, :], logical, axis=2)
    y = _moba_attend(blocks, pages_idx, q3, k_new.reshape(n, 1, d), v_new.reshape(n, 1, d),
                     logits.reshape(n * n_blocks, 1, MOBA_BLOCK * n_heads), cache_v,
                     pages_per_block=pages_per_block, past_len=past_len)
    return y.reshape(n, d)


def _memkv_body(m_ref, g_ref, wk_ref, wv_ref, k_ref, v_ref):
    h = _rms(m_ref[...], g_ref[...]).astype(BF16)
    k_ref[...] = _dot(h, wk_ref[...])
    v_ref[...] = _dot(h, wv_ref[...])


def _memkv(mem, g, wk, wv, *, tm):
    n, d = mem.shape
    row = pl.BlockSpec((tm, d), lambda i: (i, 0))
    return pl.pallas_call(
        _memkv_body,
        grid=(n // tm,),
        in_specs=[row, _full((1, d)), _full(wk.shape), _full(wv.shape)],
        out_specs=[row, row],
        out_shape=[jax.ShapeDtypeStruct((n, d), F32)] * 2,
        compiler_params=_params("parallel"),
        name="mem_kv",
    )(mem, g, wk, wv)


def _mix_out(x, ys, ya, gs, ga, w_out_ref, d_ssm):
    ysn = _rms(ys, gs).astype(BF16)
    yan = _rms(ya, ga).astype(BF16)
    return x + _dot(ysn, w_out_ref[:d_ssm, :]) + _dot(yan, w_out_ref[d_ssm:, :])


def _xattn_heads(q, mk_ref, mv_ref):
    d = q.shape[1]
    xd = d // N_XHEADS
    outs = []
    for h in range(N_XHEADS):
        hs = slice(h * xd, (h + 1) * xd)
        s = _dot_t(q[:, hs].astype(BF16), mk_ref[0, :, hs].astype(BF16)) * (xd ** -0.5)
        m = jnp.max(s, axis=1, keepdims=True)
        p = jnp.exp(s - m)
        l = jnp.sum(p, axis=1, keepdims=True)
        outs.append(_dot(p.astype(BF16), mv_ref[0, :, hs].astype(BF16)) / l)
    return jnp.concatenate(outs, axis=1)


def _merge_prompt_body(x_ref, ys_ref, ya_ref, gs_ref, ga_ref, wout_ref, gx_ref, wq_ref, mk_ref, mv_ref,
                       wo_ref, o_ref, *, d_ssm):
    x2 = _mix_out(x_ref[0], ys_ref[0], ya_ref[0], gs_ref[...], ga_ref[...], wout_ref, d_ssm)
    q = _dot(_rms(x2, gx_ref[...]).astype(BF16), wq_ref[...])
    o = _xattn_heads(q, mk_ref, mv_ref)
    o_ref[0] = x2 + _dot(o.astype(BF16), wo_ref[...])


def _merge_prompt(x, ys, ya, gs, ga, w_out, gx, wq, mk, mv, wo, *, tm):
    b, l, d = x.shape
    d_ssm = ys.shape[2]
    d_attn = ya.shape[2]
    n_mem = mk.shape[1]
    row = lambda w: pl.BlockSpec((1, tm, w), lambda i, j: (i, j, 0))
    mem = pl.BlockSpec((1, n_mem, d), lambda i, j: (i, 0, 0))
    return pl.pallas_call(
        functools.partial(_merge_prompt_body, d_ssm=d_ssm),
        grid=(b, l // tm),
        in_specs=[row(d), row(d_ssm), row(d_attn), _full((1, d_ssm)), _full((1, d_attn)),
                  _full(w_out.shape), _full((1, d)), _full(wq.shape), mem, mem, _full(wo.shape)],
        out_specs=row(d),
        out_shape=jax.ShapeDtypeStruct((b, l, d), F32),
        compiler_params=_params("parallel", "parallel"),
        name="merge_prompt",
    )(x, ys, ya, gs, ga, w_out, gx, wq, mk, mv, wo)


def _merge_pre_body(x_ref, ys_ref, ya_ref, gs_ref, ga_ref, wout_ref, gx_ref, wq_ref, x2_ref, q_ref, *, d_ssm):
    x2 = _mix_out(x_ref[...], ys_ref[...], ya_ref[...], gs_ref[...], ga_ref[...], wout_ref, d_ssm)
    x2_ref[...] = x2
    q_ref[...] = _dot(_rms(x2, gx_ref[...]).astype(BF16), wq_ref[...])


def _merge_pre(x, ys, ya, gs, ga, w_out, gx, wq):
    n, d = x.shape
    args = (x, ys, ya, gs, ga, w_out, gx, wq)
    return pl.pallas_call(
        functools.partial(_merge_pre_body, d_ssm=ys.shape[1]),
        grid=(1,),
        in_specs=[_full(a.shape) for a in args],
        out_specs=[_full((n, d)), _full((n, d))],
        out_shape=[jax.ShapeDtypeStruct((n, d), F32)] * 2,
        compiler_params=_params("arbitrary"),
        name="merge_sample_pre",
    )(*args)


def _xattn_sample_body(q_ref, mk_ref, mv_ref, o_ref):
    o_ref[0] = _xattn_heads(q_ref[0], mk_ref, mv_ref)


def _xattn_sample(q3, mk, mv):
    n, _, d = q3.shape
    n_mem = mk.shape[1]
    row = pl.BlockSpec((1, 1, d), lambda i: (i, 0, 0))
    mem = pl.BlockSpec((1, n_mem, d), lambda i: (i, 0, 0))
    return pl.pallas_call(
        _xattn_sample_body,
        grid=(n,),
        in_specs=[row, mem, mem],
        out_specs=row,
        out_shape=jax.ShapeDtypeStruct((n, 1, d), F32),
        compiler_params=_params("parallel"),
        name="xattn_sample",
    )(q3, mk, mv)


def _merge_post_body(x_ref, o_ref, wo_ref, y_ref):
    y_ref[...] = x_ref[...] + _dot(o_ref[...].astype(BF16), wo_ref[...])


def _merge_post(x2, o, wo):
    n, d = x2.shape
    return pl.pallas_call(
        _merge_post_body,
        grid=(1,),
        in_specs=[_full((n, d)), _full((n, d)), _full(wo.shape)],
        out_specs=_full((n, d)),
        out_shape=jax.ShapeDtypeStruct((n, d), F32),
        compiler_params=_params("arbitrary"),
        name="merge_sample_post",
    )(x2, o, wo)


def kernel(x_prompt, x_sample, mem_prompt, cache_k, cache_v, page_table, state_ssm_re, state_ssm_im, cache_mem_k, cache_mem_v, g_ffn1, w1_ffn1, w3_ffn1, w2_ffn1, g_mix, w_in, ssm_a_re, ssm_a_im, ssm_log_dt, ssm_b_re, ssm_b_im, ssm_c_re, ssm_c_im, ssm_d, w_glu, b_glu, g_out_ssm, g_out_attn, w_out, g_xattn, g_mem, wq_x, wk_x, wv_x, wo_x, g_ffn2, w1_ffn2, w3_ffn2, w2_ffn2, g_final):
    depth = g_ffn1.shape[0]
    assert depth == 1
    b, l, d = x_prompt.shape
    ns, ls, _ = x_sample.shape
    assert ls == 1
    n_groups, n_state = ssm_a_re.shape[1:]
    gp = n_groups * n_state
    d_ssm = n_groups * SSM_GROUP
    d_attn = (w_in.shape[2] - d_ssm) // 3
    n_heads = d_attn // ATTN_HEAD_DIM
    n_mem = mem_prompt.shape[1]
    xd = d // N_XHEADS

    vec = lambda a: a[0].reshape(1, -1)
    wb = lambda a: a[0].astype(BF16)
    w1a, w3a, w2a = wb(w1_ffn1), wb(w3_ffn1), wb(w2_ffn1)
    w1b, w3b, w2b = wb(w1_ffn2), wb(w3_ffn2), wb(w2_ffn2)
    w_in_b, w_out_b, w_glu_b = wb(w_in), wb(w_out), wb(w_glu)
    wq_b, wk_b, wv_b, wo_b = wb(wq_x), wb(wk_x), wb(wv_x), wb(wo_x)
    gfin = g_final.reshape(1, -1)
    ar, ai, bmat, cmat = _s5_matrices(ssm_a_re[0], ssm_a_im[0], ssm_log_dt[0], ssm_b_re[0], ssm_b_im[0],
                                      ssm_c_re[0], ssm_c_im[0])
    s5_w = (ar, ai, bmat, cmat, vec(ssm_d), w_glu_b, vec(b_glu))

    tm = 512 if l % 512 == 0 else l
    xp = x_prompt.reshape(b * l, d)
    mem_k, mem_v = _memkv(mem_prompt.reshape(b * n_mem, d), vec(g_mem), wk_b, wv_b,
                          tm=min(256, b * n_mem))
    x1 = _ffn(xp, vec(g_ffn1), w1a, w3a, w2a, gfin, final_norm=False, tm=tm)
    u, q, k, v, kb, vb = _proj(x1, vec(g_mix), w_in_b, d_ssm=d_ssm, tm=tm)
    u_tm = u.reshape(b, l, d_ssm).swapaxes(0, 1).reshape(l * b, d_ssm)
    t_chunk = 128 if l % 128 == 0 else l
    ys_tm, h_last = _s5_prompt(u_tm, b, *s5_w, t_chunk=t_chunk)
    ys = ys_tm.reshape(l, b, d_ssm).swapaxes(0, 1)
    sums = _block_means(k.reshape(b, l, d_attn))
    ya = _moba_prompt(q.reshape(b, l, d_attn), kb.reshape(b, l, d_attn), vb.reshape(b, l, d_attn), sums)
    x3 = _merge_prompt(x1.reshape(b, l, d), ys, ya, vec(g_out_ssm), vec(g_out_attn), w_out_b, vec(g_xattn),
                       wq_b, mem_k.reshape(b, n_mem, d), mem_v.reshape(b, n_mem, d), wo_b, tm=tm)
    y_prompt = _ffn(x3.reshape(b * l, d), vec(g_ffn2), w1b, w3b, w2b, gfin, final_norm=True, tm=tm)
    h_last = h_last[SUBLANES - b:]
    heads = (1, b, l, n_heads, ATTN_HEAD_DIM)
    state = (1, b, n_groups, n_state)
    memkv = (1, b, n_mem, N_XHEADS, xd)

    xs = x_sample.reshape(ns, d)
    xs1 = _ffn(xs, vec(g_ffn1), w1a, w3a, w2a, gfin, final_norm=False, tm=ns)
    us, qs, ks, vs, _, _ = _proj(xs1, vec(g_mix), w_in_b, d_ssm=d_ssm, tm=ns)
    yss, hrs, his = _s5_sample(us, state_ssm_re[0].reshape(ns, gp), state_ssm_im[0].reshape(ns, gp), *s5_w)
    yas = _moba_sample(qs, ks, vs, cache_k, cache_v, page_table)
    xs2, qx = _merge_pre(xs1, yss, yas, vec(g_out_ssm), vec(g_out_attn), w_out_b, vec(g_xattn), wq_b)
    ox = _xattn_sample(qx.reshape(ns, 1, d), cache_mem_k[0].reshape(ns, n_mem, d),
                       cache_mem_v[0].reshape(ns, n_mem, d))
    xs3 = _merge_post(xs2, ox.reshape(ns, d), wo_b)
    y_sample = _ffn(xs3, vec(g_ffn2), w1b, w3b, w2b, gfin, final_norm=True, tm=ns)
    sheads = (1, ns, 1, n_heads, ATTN_HEAD_DIM)
    sstate = (1, ns, n_groups, n_state)

    return (y_prompt.reshape(b, l, d), y_sample.reshape(ns, 1, d),
            k.reshape(heads), v.reshape(heads),
            h_last[:, :gp].reshape(state), h_last[:, gp:].reshape(state),
            mem_k.reshape(memkv), mem_v.reshape(memkv),
            ks.reshape(sheads), vs.reshape(sheads),
            hrs.reshape(sstate), his.reshape(sstate))
```

```python
import functools
import math

import jax
import jax.numpy as jnp
from jax import lax
from jax.experimental import pallas as pl
from jax.experimental.pallas import tpu as pltpu

RMS_EPS = 1e-6
NEG_INF = -1e30
SSM_GROUP = 16
ATTN_HEAD_DIM = 64
MOBA_BLOCK = 256
MOBA_TOPK = 3
N_XHEADS = 4
LANES = 128
SUBLANES = 8
VMEM_LIMIT = 56 * 1024 * 1024

F32 = jnp.float32
BF16 = jnp.bfloat16
HIGHEST = lax.Precision.HIGHEST


def _params(*sem):
    return pltpu.CompilerParams(dimension_semantics=sem, vmem_limit_bytes=VMEM_LIMIT)


def _rms(x, g):
    ms = jnp.mean(x * x, axis=-1, keepdims=True)
    return x * lax.rsqrt(ms + RMS_EPS) * g


def _dot(a, b):
    return jnp.dot(a, b, preferred_element_type=F32)


def _dot_t(a, b, precision=None):
    return lax.dot_general(a, b, (((1,), (1,)), ((), ())), preferred_element_type=F32,
                           precision=precision)


def _full(shape):
    n = len(shape)
    return pl.BlockSpec(shape, lambda *_: (0,) * n)


def _ffn_body(x_ref, g_ref, w1_ref, w3_ref, w2_ref, gf_ref, o_ref, *, f_chunk, final_norm):
    x = x_ref[...]
    h = _rms(x, g_ref[...]).astype(BF16)
    acc = jnp.zeros(x.shape, F32)
    for c in range(w1_ref.shape[1] // f_chunk):
        cs = slice(c * f_chunk, (c + 1) * f_chunk)
        a = _dot(h, w1_ref[:, cs])
        b = _dot(h, w3_ref[:, cs])
        act = (a * jax.nn.sigmoid(a) * b).astype(BF16)
        acc = acc + _dot(act, w2_ref[cs, :])
    y = x + 0.5 * acc
    if final_norm:
        y = _rms(y, gf_ref[...])
    o_ref[...] = y


def _ffn(x, g, w1, w3, w2, g_final, *, final_norm, tm):
    n, d = x.shape
    f = w1.shape[1]
    f_chunk = 256 if f % 256 == 0 else f
    return pl.pallas_call(
        functools.partial(_ffn_body, f_chunk=f_chunk, final_norm=final_norm),
        grid=(n // tm,),
        in_specs=[pl.BlockSpec((tm, d), lambda i: (i, 0)), _full((1, d)), _full((d, f)), _full((d, f)),
                  _full((f, d)), _full((1, d))],
        out_specs=pl.BlockSpec((tm, d), lambda i: (i, 0)),
        out_shape=jax.ShapeDtypeStruct((n, d), F32),
        compiler_params=_params("parallel"),
        name="ffn_final" if final_norm else "ffn",
    )(x, g, w1, w3, w2, g_final)


def _proj_body(x_ref, g_ref, w_ref, u_ref, q_ref, k_ref, v_ref, *, d_ssm, d_attn):
    h = _rms(x_ref[...], g_ref[...]).astype(BF16)
    p = _dot(h, w_ref[...])
    u_ref[...] = p[:, :d_ssm]
    q_ref[...] = p[:, d_ssm:d_ssm + d_attn]
    k_ref[...] = p[:, d_ssm + d_attn:d_ssm + 2 * d_attn]
    v_ref[...] = p[:, d_ssm + 2 * d_attn:]


def _proj(x, g, w_in, *, d_ssm, tm):
    n, d = x.shape
    d_attn = (w_in.shape[1] - d_ssm) // 3
    row = lambda w: pl.BlockSpec((tm, w), lambda i: (i, 0))
    return pl.pallas_call(
        functools.partial(_proj_body, d_ssm=d_ssm, d_attn=d_attn),
        grid=(n // tm,),
        in_specs=[row(d), _full((1, d)), _full(w_in.shape)],
        out_specs=[row(d_ssm), row(d_attn), row(d_attn), row(d_attn)],
        out_shape=[jax.ShapeDtypeStruct((n, d_ssm), F32)] + [jax.ShapeDtypeStruct((n, d_attn), F32)] * 3,
        compiler_params=_params("parallel"),
        name="mix_proj",
    )(x, g, w_in)


AUX_BIAS, AUX_ONE, AUX_BLOCK, AUX_LANES, MAX_BLOCKS = 0, 3, 16, 64, 32


def _alibi_slope(h, n_heads):
    return 2.0 ** (-8.0 * (h + 1) / n_heads)


def _bf16_split3(x):
    hi = x.astype(BF16).astype(F32)
    r = x - hi
    mid = r.astype(BF16).astype(F32)
    lo = (r - mid).astype(BF16).astype(F32)
    return hi, mid, lo


def _proj_prompt_body(x_ref, g_ref, w_ref, u_ref, q_ref, kt_ref, vt_ref, kaug_ref, vtb_ref, sums_ref,
                      *, d_ssm, d_attn, n_heads):
    tm = x_ref.shape[1]
    dh = ATTN_HEAD_DIM
    h = _rms(x_ref[0], g_ref[...]).astype(BF16)
    p = _dot(h, w_ref[...])
    u_ref[0] = p[:, :d_ssm]
    q_ref[0] = p[:, d_ssm:d_ssm + d_attn]
    k = p[:, d_ssm + d_attn:d_ssm + 2 * d_attn]
    v = p[:, d_ssm + 2 * d_attn:]
    kt_ref[0] = k.T.reshape(n_heads, dh, tm)
    vt = v.T.reshape(n_heads, dh, tm)
    vt_ref[0] = vt
    for c in range(tm // MOBA_BLOCK):
        cs = slice(c * MOBA_BLOCK, (c + 1) * MOBA_BLOCK)
        vtb_ref[0, :, c] = vt[:, :, cs].astype(BF16)
        sums_ref[0, c:c + 1, :] = jnp.sum(k[cs], axis=0, keepdims=True)
    pos = pl.program_id(1) * tm + lax.broadcasted_iota(jnp.int32, (tm, AUX_LANES), 0)
    lane = lax.broadcasted_iota(jnp.int32, (tm, AUX_LANES), 1)
    block_hot = (lane - AUX_BLOCK == pos // MOBA_BLOCK).astype(F32)
    posf = pos.astype(F32)
    pieces = []
    for hh in range(n_heads):
        hi, mid, lo = _bf16_split3(_alibi_slope(hh, n_heads) * posf)
        aux = jnp.where(lane == AUX_BIAS, hi, jnp.where(lane == AUX_BIAS + 1, mid, jnp.where(
            lane == AUX_BIAS + 2, lo, jnp.where(lane < AUX_ONE + 3, 1.0, block_hot))))
        pieces += [k[:, hh * dh:(hh + 1) * dh], aux]
    kaug_ref[0] = jnp.concatenate(pieces, axis=1).astype(BF16)


def _proj_prompt(x, g, w_in, *, d_ssm, tm):
    b, l, d = x.shape
    d_attn = (w_in.shape[1] - d_ssm) // 3
    dh = ATTN_HEAD_DIM
    n_heads = d_attn // dh
    nb = l // MOBA_BLOCK
    tb = tm // MOBA_BLOCK
    assert nb <= MAX_BLOCKS and tm % MOBA_BLOCK == 0
    row = lambda w: pl.BlockSpec((1, tm, w), lambda i, j: (i, j, 0))
    tr = pl.BlockSpec((1, n_heads, dh, tm), lambda i, j: (i, 0, 0, j))
    return pl.pallas_call(
        functools.partial(_proj_prompt_body, d_ssm=d_ssm, d_attn=d_attn, n_heads=n_heads),
        grid=(b, l // tm),
        in_specs=[row(d), _full((1, d)), _full(w_in.shape)],
        out_specs=[row(d_ssm), row(d_attn), tr, tr, row(n_heads * (dh + AUX_LANES)),
                   pl.BlockSpec((1, n_heads, tb, dh, MOBA_BLOCK), lambda i, j: (i, 0, j, 0, 0)),
                   pl.BlockSpec((1, tb, d_attn), lambda i, j: (i * (l // tm) + j, 0, 0))],
        out_shape=[jax.ShapeDtypeStruct((b, l, d_ssm), F32), jax.ShapeDtypeStruct((b, l, d_attn), F32),
                   jax.ShapeDtypeStruct((b, n_heads, dh, l), F32), jax.ShapeDtypeStruct((b, n_heads, dh, l), F32),
                   jax.ShapeDtypeStruct((b, l, n_heads * (dh + AUX_LANES)), BF16),
                   jax.ShapeDtypeStruct((b, n_heads, nb, dh, MOBA_BLOCK), BF16),
                   jax.ShapeDtypeStruct((b * l // tm, tb, d_attn), F32)],
        compiler_params=_params("parallel", "parallel"),
        name="mix_proj_prompt",
    )(x, g, w_in)


def _s5_discretise(a_re, a_im, log_dt, b_re, b_im):
    dt = jnp.exp(log_dt)[:, None]
    mag = jnp.exp(a_re * dt)
    abar_re = mag * jnp.cos(a_im * dt)
    abar_im = mag * jnp.sin(a_im * dt)
    den = a_re * a_re + a_im * a_im
    nr = abar_re - 1.0
    f_re = (nr * a_re + abar_im * a_im) / den
    f_im = (abar_im * a_re - nr * a_im) / den
    bbar_re = f_re[..., None] * b_re - f_im[..., None] * b_im
    bbar_im = f_re[..., None] * b_im + f_im[..., None] * b_re
    return abar_re, abar_im, bbar_re, bbar_im


def _s5_matrices(a_re, a_im, log_dt, b_re, b_im, c_re, c_im):
    g, p, c = b_re.shape
    abar_re, abar_im, bbar_re, bbar_im = _s5_discretise(a_re, a_im, log_dt, b_re, b_im)
    eye = jnp.eye(g, dtype=F32)
    bm_re = jnp.einsum('gpc,gh->gchp', bbar_re, eye).reshape(g * c, g * p)
    bm_im = jnp.einsum('gpc,gh->gchp', bbar_im, eye).reshape(g * c, g * p)
    bmat = jnp.concatenate([bm_re, bm_im], axis=1).astype(BF16)
    cm_re = jnp.einsum('gcp,gh->gphc', c_re, eye).reshape(g * p, g * c)
    cm_im = jnp.einsum('gcp,gh->gphc', c_im, eye).reshape(g * p, g * c)
    cmat = jnp.concatenate([cm_re, -cm_im], axis=0).astype(BF16)
    return abar_re.reshape(1, g * p), abar_im.reshape(1, g * p), bmat, cmat


def _s5_output(h_bf, u, cmat_ref, d_ref, wg_ref, bg_ref):
    y = _dot(h_bf, cmat_ref[...]) + d_ref[...] * u
    z = jax.nn.gelu(y, approximate=True)
    return z * jax.nn.sigmoid(_dot(z.astype(BF16), wg_ref[...]) + bg_ref[...])


def _s5_prompt_body(u_ref, ar_ref, ai_ref, bmat_ref, cmat_ref, d_ref, wg_ref, bg_ref,
                    y_ref, hlast_ref, hbuf, carry, *, nbatch):
    gp = ar_ref.shape[1]
    rows = u_ref.shape[0]

    @pl.when(pl.program_id(0) == 0)
    def _():
        carry[...] = jnp.zeros(carry.shape, F32)

    u = u_ref[...]
    hbuf[...] = _dot(u.astype(BF16), bmat_ref[...])

    ar = jnp.broadcast_to(ar_ref[...], (SUBLANES, gp))
    ai = jnp.broadcast_to(ai_ref[...], (SUBLANES, gp))
    lower = lax.broadcasted_iota(jnp.int32, (SUBLANES, gp), 0) < nbatch

    def step(i, st):
        hr, hi = st
        r0 = pl.multiple_of(i * SUBLANES, SUBLANES)
        bur = hbuf[pl.ds(r0, SUBLANES), :gp]
        bui = hbuf[pl.ds(r0, SUBLANES), gp:]
        pr = pltpu.roll(hr, nbatch, 0)
        pi_ = pltpu.roll(hi, nbatch, 0)
        lr = ar * pr - ai * pi_ + bur
        li = ar * pi_ + ai * pr + bui
        qr = pltpu.roll(lr, nbatch, 0)
        qi = pltpu.roll(li, nbatch, 0)
        ur = ar * qr - ai * qi + bur
        ui = ar * qi + ai * qr + bui
        hr = jnp.where(lower, lr, ur)
        hi = jnp.where(lower, li, ui)
        hbuf[pl.ds(r0, SUBLANES), :gp] = hr
        hbuf[pl.ds(r0, SUBLANES), gp:] = hi
        return hr, hi

    hr, hi = lax.fori_loop(0, rows // SUBLANES, step, (carry[:, :gp], carry[:, gp:]))
    carry[:, :gp] = hr
    carry[:, gp:] = hi
    hlast_ref[:, :gp] = hr
    hlast_ref[:, gp:] = hi
    y_ref[...] = _s5_output(hbuf[...].astype(BF16), u, cmat_ref, d_ref, wg_ref, bg_ref)


def _s5_prompt(u_tm, nbatch, ar, ai, bmat, cmat, d, w_glu, b_glu, *, t_chunk):
    rows_total, d_ssm = u_tm.shape
    gp = ar.shape[1]
    rows = t_chunk * nbatch
    assert 2 * nbatch == SUBLANES and rows_total % rows == 0
    return pl.pallas_call(
        functools.partial(_s5_prompt_body, nbatch=nbatch),
        grid=(rows_total // rows,),
        in_specs=[pl.BlockSpec((rows, d_ssm), lambda i: (i, 0)), _full((1, gp)), _full((1, gp)),
                  _full(bmat.shape), _full(cmat.shape), _full((1, d_ssm)), _full(w_glu.shape),
                  _full((1, d_ssm))],
        out_specs=[pl.BlockSpec((rows, d_ssm), lambda i: (i, 0)), _full((SUBLANES, 2 * gp))],
        out_shape=[jax.ShapeDtypeStruct((rows_total, d_ssm), F32),
                   jax.ShapeDtypeStruct((SUBLANES, 2 * gp), F32)],
        scratch_shapes=[pltpu.VMEM((rows, 2 * gp), F32), pltpu.VMEM((SUBLANES, 2 * gp), F32)],
        compiler_params=_params("arbitrary"),
        name="s5_prompt",
    )(u_tm, ar, ai, bmat, cmat, d, w_glu, b_glu)


def _s5_sample_body(u_ref, h0r_ref, h0i_ref, ar_ref, ai_ref, bmat_ref, cmat_ref, d_ref, wg_ref, bg_ref,
                    y_ref, hr_ref, hi_ref):
    gp = ar_ref.shape[1]
    u = u_ref[...]
    bu = _dot(u.astype(BF16), bmat_ref[...])
    ar, ai = ar_ref[...], ai_ref[...]
    h0r, h0i = h0r_ref[...], h0i_ref[...]
    hr = bu[:, :gp] + ar * h0r - ai * h0i
    hi = bu[:, gp:] + ar * h0i + ai * h0r
    hr_ref[...] = hr
    hi_ref[...] = hi
    h_bf = jnp.concatenate([hr, hi], axis=1).astype(BF16)
    y_ref[...] = _s5_output(h_bf, u, cmat_ref, d_ref, wg_ref, bg_ref)


def _s5_sample(u, h0r, h0i, ar, ai, bmat, cmat, d, w_glu, b_glu):
    n, d_ssm = u.shape
    gp = ar.shape[1]
    args = (u, h0r, h0i, ar, ai, bmat, cmat, d, w_glu, b_glu)
    return pl.pallas_call(
        _s5_sample_body,
        grid=(1,),
        in_specs=[_full(a.shape) for a in args],
        out_specs=[_full((n, d_ssm)), _full((n, gp)), _full((n, gp))],
        out_shape=[jax.ShapeDtypeStruct((n, d_ssm), F32), jax.ShapeDtypeStruct((n, gp), F32),
                   jax.ShapeDtypeStruct((n, gp), F32)],
        compiler_params=_params("arbitrary"),
        name="s5_sample",
    )(*args)


def _top_blocks(gate, brow, n_valid):
    bf = brow.astype(F32)
    g = jnp.where(brow < n_valid, gate, NEG_INF)
    sel = jnp.zeros(gate.shape, jnp.bool_)
    for _ in range(MOBA_TOPK):
        mx = jnp.max(g, axis=0, keepdims=True)
        first = jnp.min(jnp.where(g == mx, bf, float(gate.shape[0])), axis=0, keepdims=True)
        pick = bf == first
        sel = jnp.logical_or(sel, pick)
        g = jnp.where(pick, -jnp.inf, g)
    return jnp.logical_and(sel, brow < n_valid)


def _moba_prompt_body(q_ref, kaug_ref, vtb_ref, sums_ref, o_ref, qaug, acc, s_scr, p_scr, *, n_heads):
    j = pl.program_id(1)
    blk = MOBA_BLOCK
    dh = ATTN_HEAD_DIM
    ka = dh + AUX_LANES
    nb = sums_ref.shape[1]
    t0 = pl.multiple_of(j * blk, blk)
    scale = dh ** -0.5
    qt = q_ref[0].T
    brow = lax.broadcasted_iota(jnp.int32, (MAX_BLOCKS, blk), 0)
    crow = lax.broadcasted_iota(jnp.int32, (AUX_BLOCK, blk), 0)
    t0f = jnp.broadcast_to(t0.astype(F32), (AUX_BLOCK, blk))
    for h in range(n_heads):
        qth = qt[h * dh:(h + 1) * dh]
        means = sums_ref[0, :, h * dh:(h + 1) * dh] * (1.0 / blk)
        gate = jnp.dot(means, qth, precision=HIGHEST, preferred_element_type=F32)
        if nb < MAX_BLOCKS:
            gate = jnp.concatenate([gate, jnp.full((MAX_BLOCKS - nb, blk), NEG_INF, F32)], axis=0)
        keep = jnp.logical_or(_top_blocks(gate, brow, j), brow == j)
        mask = jnp.where(keep, 0.0, NEG_INF)
        hi, mid, lo = _bf16_split3(-_alibi_slope(h, n_heads) * t0f)
        const = jnp.where(crow < AUX_ONE, 1.0, jnp.where(crow == AUX_ONE, hi, jnp.where(
            crow == AUX_ONE + 1, mid, jnp.where(crow == AUX_ONE + 2, lo, 0.0))))
        rest = jnp.zeros((AUX_LANES - AUX_BLOCK - MAX_BLOCKS, blk), F32)
        qaug[h] = jnp.concatenate([qth * scale, const, mask, rest], axis=0).astype(BF16)

    def scores(n, h):
        r = pl.multiple_of(n * blk, blk)
        return _dot(kaug_ref[0, pl.ds(r, blk), h * ka:(h + 1) * ka], qaug[h])

    causal = (lax.broadcasted_iota(jnp.int32, (blk, blk), 0) <= lax.broadcasted_iota(jnp.int32, (blk, blk), 1))
    ms, ls = [], []
    for h in range(n_heads):
        s_scr[h] = scores(j, h)
    for h in range(n_heads):
        s = jnp.where(causal, s_scr[h], NEG_INF)
        m = jnp.max(s, axis=0, keepdims=True)
        p = jnp.exp(s - m)
        p_scr[h] = p.astype(BF16)
        ms.append(m)
        ls.append(jnp.sum(p, axis=0, keepdims=True))
    for h in range(n_heads):
        acc[h * dh:(h + 1) * dh, :] = _dot(vtb_ref[0, h, j], p_scr[h])

    def past(n, st):
        m_all, l_all = st
        ms, ls, alphas = [], [], []
        for h in range(n_heads):
            s_scr[h] = scores(n, h)
        for h in range(n_heads):
            s = s_scr[h]
            m_old = m_all[h:h + 1]
            m_new = jnp.maximum(m_old, jnp.max(s, axis=0, keepdims=True))
            alpha = jnp.exp(m_old - m_new)
            p = jnp.exp(s - m_new)
            p_scr[h] = p.astype(BF16)
            ms.append(m_new)
            alphas.append(alpha)
            ls.append(alpha * l_all[h:h + 1] + jnp.sum(p, axis=0, keepdims=True))
        for h in range(n_heads):
            hs = slice(h * dh, (h + 1) * dh)
            acc[hs, :] = alphas[h] * acc[hs, :] + _dot(vtb_ref[0, h, n], p_scr[h])
        return jnp.concatenate(ms, axis=0), jnp.concatenate(ls, axis=0)

    _, l_all = lax.fori_loop(0, j, past, (jnp.concatenate(ms, axis=0), jnp.concatenate(ls, axis=0)))
    out_t = jnp.concatenate([acc[h * dh:(h + 1) * dh, :] / l_all[h:h + 1] for h in range(n_heads)], axis=0)
    o_ref[0] = out_t.T


def _moba_prompt(q, kaug, vtb, sums):
    b, l, d = q.shape
    nb = l // MOBA_BLOCK
    n_heads = d // ATTN_HEAD_DIM
    ka = ATTN_HEAD_DIM + AUX_LANES
    once = pl.Buffered(1)
    return pl.pallas_call(
        functools.partial(_moba_prompt_body, n_heads=n_heads),
        grid=(b, nb),
        in_specs=[pl.BlockSpec((1, MOBA_BLOCK, d), lambda i, j: (i, j, 0)),
                  pl.BlockSpec((1, l, n_heads * ka), lambda i, j: (i, 0, 0), pipeline_mode=once),
                  pl.BlockSpec((1, n_heads, nb, ATTN_HEAD_DIM, MOBA_BLOCK), lambda i, j: (i, 0, 0, 0, 0),
                               pipeline_mode=once),
                  pl.BlockSpec((1, nb, d), lambda i, j: (i, 0, 0))],
        out_specs=pl.BlockSpec((1, MOBA_BLOCK, d), lambda i, j: (i, j, 0)),
        out_shape=jax.ShapeDtypeStruct((b, l, d), F32),
        scratch_shapes=[pltpu.VMEM((n_heads, ka, MOBA_BLOCK), BF16), pltpu.VMEM((d, MOBA_BLOCK), F32),
                        pltpu.VMEM((n_heads, MOBA_BLOCK, MOBA_BLOCK), F32),
                        pltpu.VMEM((n_heads, MOBA_BLOCK, MOBA_BLOCK), BF16)],
        compiler_params=_params("parallel", "arbitrary"),
        name="moba_prompt",
    )(q, kaug, vtb, sums)


def _moba_kpass_body(pt_ref, q_ref, *refs, pages, n_heads):
    del pt_ref
    k_refs, (lg_ref, ps_ref) = refs[:pages], refs[pages:]
    s = pl.program_id(1)
    dh, page = k_refs[0].shape[3:]

    @pl.when(s == 0)
    def _():
        ps_ref[...] = jnp.zeros(ps_ref.shape, F32)

    qcols = q_ref[0]
    qb = [jnp.broadcast_to(qcols[:, h:h + 1], (dh, page)) for h in range(n_heads)]
    plane = lax.broadcasted_iota(jnp.int32, (n_heads, ps_ref.shape[2]), 1)
    psum = ps_ref[0]
    for i in range(pages):
        lg = jnp.concatenate([jnp.sum(k_refs[i][0, 0, h] * qb[h], axis=0, keepdims=True)
                              for h in range(n_heads)], axis=0)
        lg_ref[0, :, i * page:(i + 1) * page] = lg
        psum = jnp.where(plane == s * pages + i, jnp.sum(lg, axis=1, keepdims=True), psum)
    ps_ref[0] = psum


def _moba_kpass(q_t, cache_kt, page_table, *, pages):
    n, dh, n_heads = q_t.shape
    n_pages = page_table.shape[1]
    page = cache_kt.shape[4]
    k_spec = lambda i: pl.BlockSpec((1, 1, n_heads, dh, page),
                                    lambda b, s, pt: (0, pt[b, s * pages + i], 0, 0, 0))
    grid_spec = pltpu.PrefetchScalarGridSpec(
        num_scalar_prefetch=1,
        grid=(n, n_pages // pages),
        in_specs=[pl.BlockSpec((1, dh, n_heads), lambda b, s, pt: (b, 0, 0))]
                 + [k_spec(i) for i in range(pages)],
        out_specs=[pl.BlockSpec((1, n_heads, pages * page), lambda b, s, pt: (b, 0, s)),
                   pl.BlockSpec((1, n_heads, n_pages), lambda b, s, pt: (b, 0, 0))])
    return pl.pallas_call(
        functools.partial(_moba_kpass_body, pages=pages, n_heads=n_heads),
        grid_spec=grid_spec,
        out_shape=[jax.ShapeDtypeStruct((n, n_heads, n_pages * page), F32),
                   jax.ShapeDtypeStruct((n, n_heads, n_pages), F32)],
        compiler_params=_params("parallel", "arbitrary"),
        name="moba_sample_kpass",
    )(page_table, q_t, *([cache_kt] * pages))


def _moba_select_body(ps_ref, idx_ref, *, pages_per_block, n_blocks):
    ps = ps_ref[...]
    lanes = ps.shape[1]
    lane = lax.broadcasted_iota(jnp.int32, (1, lanes), 1)
    g = ps
    for i in range(1, pages_per_block):
        g = g + pltpu.roll(ps, lanes - i, 1)
    g = g * (1.0 / MOBA_BLOCK)
    is_block = jnp.logical_and(lane % pages_per_block == 0, lane < n_blocks * pages_per_block)
    g = jnp.where(is_block, g, -jnp.inf)
    lanef = lane.astype(F32)
    out = jnp.zeros(idx_ref.shape, F32)
    olane = lax.broadcasted_iota(jnp.int32, idx_ref.shape, 1)
    for r in range(MOBA_TOPK):
        mx = jnp.max(g, axis=1, keepdims=True)
        first = jnp.min(jnp.where(g == mx, lanef, float(lanes)), axis=1, keepdims=True)
        out = jnp.where(olane == r, first * (1.0 / pages_per_block), out)
        g = jnp.where(lanef == first, -jnp.inf, g)
    idx_ref[...] = out.astype(jnp.int32)


def _moba_select(page_sums2, *, pages_per_block, n_blocks):
    rows, lanes = page_sums2.shape
    return pl.pallas_call(
        functools.partial(_moba_select_body, pages_per_block=pages_per_block, n_blocks=n_blocks),
        grid=(1,),
        in_specs=[_full((rows, lanes))],
        out_specs=_full((rows, LANES)),
        out_shape=jax.ShapeDtypeStruct((rows, LANES), jnp.int32),
        compiler_params=_params("arbitrary"),
        name="moba_sample_select",
    )(page_sums2)


def _moba_attend_body(blk_ref, pg_ref, q_ref, kn_ref, vn_ref, *refs, pages_per_block, n_heads, past_len):
    del pg_ref
    n_lg = MOBA_TOPK
    n_v = MOBA_TOPK * pages_per_block
    lg_refs, v_refs, o_ref = refs[:n_lg], refs[n_lg:n_lg + n_v], refs[n_lg + n_v]
    b = pl.program_id(0)
    h = pl.program_id(1)
    dh = ATTN_HEAD_DIM
    d = n_heads * dh
    blk = MOBA_BLOCK
    page = blk // pages_per_block
    scale = dh ** -0.5
    slope = jnp.exp2(-8.0 * (h + 1).astype(F32) / n_heads)
    hmask = lax.broadcasted_iota(jnp.int32, (1, d), 1) // dh == h

    @pl.when(h == 0)
    def _():
        o_ref[...] = jnp.zeros(o_ref.shape, F32)

    s_own = jnp.sum(jnp.where(hmask, q_ref[0] * kn_ref[0], 0.0), axis=1, keepdims=True) * scale
    lane = lax.broadcasted_iota(jnp.int32, (1, blk), 1)
    ss = []
    for i in range(n_lg):
        pos = blk_ref[b, h, i] * blk + lane
        ss.append(lg_refs[i][0] * scale - slope * (past_len - pos).astype(F32))
    m = s_own
    for s in ss:
        m = jnp.maximum(m, jnp.max(s, axis=1, keepdims=True))
    p_own = jnp.exp(s_own - m)
    l = p_own
    acc = jnp.zeros((1, dh), F32)
    for i in range(n_lg):
        p = jnp.exp(ss[i] - m)
        l = l + jnp.sum(p, axis=1, keepdims=True)
        for c in range(pages_per_block):
            vt = v_refs[i * pages_per_block + c][0, 0, 0].astype(BF16)
            acc = acc + _dot_t(p[:, c * page:(c + 1) * page].astype(BF16), vt)
    tiled = jnp.concatenate([acc] * n_heads, axis=1)
    val = (tiled + p_own * vn_ref[0]) / l
    o_ref[0] = jnp.where(hmask, val, o_ref[0])


def _moba_attend(blocks, pages_idx, q3, kn3, vn3, logits, cache_vt, *, pages_per_block, past_len):
    n, _, d = q3.shape
    _, _, n_heads, dh, page = cache_vt.shape
    n_blocks = logits.shape[0] // (n * n_heads)
    row = pl.BlockSpec((1, 1, d), lambda b, h, bl, pg: (b, 0, 0))
    lg_spec = lambda i: pl.BlockSpec(
        (1, 1, MOBA_BLOCK), lambda b, h, bl, pg: ((b * n_heads + h) * n_blocks + bl[b, h, i], 0, 0))
    v_spec = lambda i: pl.BlockSpec(
        (1, 1, 1, dh, page), lambda b, h, bl, pg: (0, pg[b, h, i], h, 0, 0))
    n_v = MOBA_TOPK * pages_per_block
    grid_spec = pltpu.PrefetchScalarGridSpec(
        num_scalar_prefetch=2,
        grid=(n, n_heads),
        in_specs=[row, row, row] + [lg_spec(i) for i in range(MOBA_TOPK)] + [v_spec(i) for i in range(n_v)],
        out_specs=row)
    return pl.pallas_call(
        functools.partial(_moba_attend_body, pages_per_block=pages_per_block, n_heads=n_heads,
                          past_len=past_len),
        grid_spec=grid_spec,
        out_shape=jax.ShapeDtypeStruct((n, 1, d), F32),
        compiler_params=_params("parallel", "arbitrary"),
        name="moba_sample_attend",
    )(blocks, pages_idx, q3, kn3, vn3, *([logits] * MOBA_TOPK), *([cache_vt] * n_v))


def _moba_sample(q, k_new, v_new, cache_k, cache_v, page_table):
    n, d = q.shape
    _, n_pool, page, n_heads, dh = cache_k.shape
    n_pages = page_table.shape[1]
    past_len = n_pages * page
    pages_per_block = MOBA_BLOCK // page
    n_blocks = n_pages // pages_per_block
    assert n_blocks >= MOBA_TOPK and n_pages % LANES == 0
    q3 = q.reshape(n, 1, d)
    cache_kt = cache_k.transpose(0, 1, 3, 4, 2)
    cache_vt = cache_v.transpose(0, 1, 3, 4, 2)
    kpass_pages = 8 if n_pages % 8 == 0 else 1
    logits, page_sums = _moba_kpass(q.reshape(n, n_heads, dh).swapaxes(1, 2), cache_kt, page_table,
                                    pages=kpass_pages)
    blocks = _moba_select(page_sums.reshape(n * n_heads, n_pages), pages_per_block=pages_per_block,
                          n_blocks=n_blocks)[:, :MOBA_TOPK].reshape(n, n_heads, MOBA_TOPK)
    logical = (blocks[..., None] * pages_per_block + jnp.arange(pages_per_block)).reshape(n, n_heads, -1)
    pages_idx = jnp.take_along_axis(page_table[:, None, :], logical, axis=2)
    y = _moba_attend(blocks, pages_idx, q3, k_new.reshape(n, 1, d), v_new.reshape(n, 1, d),
                     logits.reshape(n * n_heads * n_blocks, 1, MOBA_BLOCK), cache_vt,
                     pages_per_block=pages_per_block, past_len=past_len)
    return y.reshape(n, d)


def _memkv_body(m_ref, g_ref, wk_ref, wv_ref, k_ref, v_ref):
    h = _rms(m_ref[...], g_ref[...]).astype(BF16)
    k_ref[...] = _dot(h, wk_ref[...])
    v_ref[...] = _dot(h, wv_ref[...])


def _memkv(mem, g, wk, wv, *, tm):
    n, d = mem.shape
    row = pl.BlockSpec((tm, d), lambda i: (i, 0))
    return pl.pallas_call(
        _memkv_body,
        grid=(n // tm,),
        in_specs=[row, _full((1, d)), _full(wk.shape), _full(wv.shape)],
        out_specs=[row, row],
        out_shape=[jax.ShapeDtypeStruct((n, d), F32)] * 2,
        compiler_params=_params("parallel"),
        name="mem_kv",
    )(mem, g, wk, wv)


def _mix_out(x, ys, ya, gs, ga, w_out_ref, d_ssm):
    ysn = _rms(ys, gs).astype(BF16)
    yan = _rms(ya, ga).astype(BF16)
    return x + _dot(ysn, w_out_ref[:d_ssm, :]) + _dot(yan, w_out_ref[d_ssm:, :])


def _xattn_heads(q, mk_ref, mv_ref):
    d = q.shape[1]
    xd = d // N_XHEADS
    outs = []
    for h in range(N_XHEADS):
        hs = slice(h * xd, (h + 1) * xd)
        s = _dot_t(q[:, hs].astype(BF16), mk_ref[0, :, hs].astype(BF16)) * (xd ** -0.5)
        m = jnp.max(s, axis=1, keepdims=True)
        p = jnp.exp(s - m)
        l = jnp.sum(p, axis=1, keepdims=True)
        outs.append(_dot(p.astype(BF16), mv_ref[0, :, hs].astype(BF16)) / l)
    return jnp.concatenate(outs, axis=1)


def _merge_prompt_body(x_ref, ys_ref, ya_ref, gs_ref, ga_ref, wout_ref, gx_ref, wq_ref, mk_ref, mv_ref,
                       wo_ref, o_ref, *, d_ssm):
    x2 = _mix_out(x_ref[0], ys_ref[0], ya_ref[0], gs_ref[...], ga_ref[...], wout_ref, d_ssm)
    q = _dot(_rms(x2, gx_ref[...]).astype(BF16), wq_ref[...])
    o = _xattn_heads(q, mk_ref, mv_ref)
    o_ref[0] = x2 + _dot(o.astype(BF16), wo_ref[...])


def _merge_prompt(x, ys, ya, gs, ga, w_out, gx, wq, mk, mv, wo, *, tm):
    b, l, d = x.shape
    d_ssm = ys.shape[2]
    d_attn = ya.shape[2]
    n_mem = mk.shape[1]
    row = lambda w: pl.BlockSpec((1, tm, w), lambda i, j: (i, j, 0))
    mem = pl.BlockSpec((1, n_mem, d), lambda i, j: (i, 0, 0))
    return pl.pallas_call(
        functools.partial(_merge_prompt_body, d_ssm=d_ssm),
        grid=(b, l // tm),
        in_specs=[row(d), row(d_ssm), row(d_attn), _full((1, d_ssm)), _full((1, d_attn)),
                  _full(w_out.shape), _full((1, d)), _full(wq.shape), mem, mem, _full(wo.shape)],
        out_specs=row(d),
        out_shape=jax.ShapeDtypeStruct((b, l, d), F32),
        compiler_params=_params("parallel", "parallel"),
        name="merge_prompt",
    )(x, ys, ya, gs, ga, w_out, gx, wq, mk, mv, wo)


def _merge_pre_body(x_ref, ys_ref, ya_ref, gs_ref, ga_ref, wout_ref, gx_ref, wq_ref, x2_ref, q_ref, *, d_ssm):
    x2 = _mix_out(x_ref[...], ys_ref[...], ya_ref[...], gs_ref[...], ga_ref[...], wout_ref, d_ssm)
    x2_ref[...] = x2
    q_ref[...] = _dot(_rms(x2, gx_ref[...]).astype(BF16), wq_ref[...])


def _merge_pre(x, ys, ya, gs, ga, w_out, gx, wq):
    n, d = x.shape
    args = (x, ys, ya, gs, ga, w_out, gx, wq)
    return pl.pallas_call(
        functools.partial(_merge_pre_body, d_ssm=ys.shape[1]),
        grid=(1,),
        in_specs=[_full(a.shape) for a in args],
        out_specs=[_full((n, d)), _full((n, d))],
        out_shape=[jax.ShapeDtypeStruct((n, d), F32)] * 2,
        compiler_params=_params("arbitrary"),
        name="merge_sample_pre",
    )(*args)


def _xattn_sample_body(q_ref, mk_ref, mv_ref, o_ref):
    o_ref[0] = _xattn_heads(q_ref[0], mk_ref, mv_ref)


def _xattn_sample(q3, mk, mv):
    n, _, d = q3.shape
    n_mem = mk.shape[1]
    row = pl.BlockSpec((1, 1, d), lambda i: (i, 0, 0))
    mem = pl.BlockSpec((1, n_mem, d), lambda i: (i, 0, 0))
    return pl.pallas_call(
        _xattn_sample_body,
        grid=(n,),
        in_specs=[row, mem, mem],
        out_specs=row,
        out_shape=jax.ShapeDtypeStruct((n, 1, d), F32),
        compiler_params=_params("parallel"),
        name="xattn_sample",
    )(q3, mk, mv)


def _merge_post_body(x_ref, o_ref, wo_ref, y_ref):
    y_ref[...] = x_ref[...] + _dot(o_ref[...].astype(BF16), wo_ref[...])


def _merge_post(x2, o, wo):
    n, d = x2.shape
    return pl.pallas_call(
        _merge_post_body,
        grid=(1,),
        in_specs=[_full((n, d)), _full((n, d)), _full(wo.shape)],
        out_specs=_full((n, d)),
        out_shape=jax.ShapeDtypeStruct((n, d), F32),
        compiler_params=_params("arbitrary"),
        name="merge_sample_post",
    )(x2, o, wo)


def kernel(x_prompt, x_sample, mem_prompt, cache_k, cache_v, page_table, state_ssm_re, state_ssm_im, cache_mem_k, cache_mem_v, g_ffn1, w1_ffn1, w3_ffn1, w2_ffn1, g_mix, w_in, ssm_a_re, ssm_a_im, ssm_log_dt, ssm_b_re, ssm_b_im, ssm_c_re, ssm_c_im, ssm_d, w_glu, b_glu, g_out_ssm, g_out_attn, w_out, g_xattn, g_mem, wq_x, wk_x, wv_x, wo_x, g_ffn2, w1_ffn2, w3_ffn2, w2_ffn2, g_final):
    depth = g_ffn1.shape[0]
    assert depth == 1
    b, l, d = x_prompt.shape
    ns, ls, _ = x_sample.shape
    assert ls == 1
    n_groups, n_state = ssm_a_re.shape[1:]
    gp = n_groups * n_state
    d_ssm = n_groups * SSM_GROUP
    d_attn = (w_in.shape[2] - d_ssm) // 3
    n_heads = d_attn // ATTN_HEAD_DIM
    n_mem = mem_prompt.shape[1]
    xd = d // N_XHEADS

    vec = lambda a: a[0].reshape(1, -1)
    wb = lambda a: a[0].astype(BF16)
    w1a, w3a, w2a = wb(w1_ffn1), wb(w3_ffn1), wb(w2_ffn1)
    w1b, w3b, w2b = wb(w1_ffn2), wb(w3_ffn2), wb(w2_ffn2)
    w_in_b, w_out_b, w_glu_b = wb(w_in), wb(w_out), wb(w_glu)
    wq_b, wk_b, wv_b, wo_b = wb(wq_x), wb(wk_x), wb(wv_x), wb(wo_x)
    gfin = g_final.reshape(1, -1)
    ar, ai, bmat, cmat = _s5_matrices(ssm_a_re[0], ssm_a_im[0], ssm_log_dt[0], ssm_b_re[0], ssm_b_im[0],
                                      ssm_c_re[0], ssm_c_im[0])
    s5_w = (ar, ai, bmat, cmat, vec(ssm_d), w_glu_b, vec(b_glu))

    tm = 512 if l % 512 == 0 else l
    xp = x_prompt.reshape(b * l, d)
    mem_k, mem_v = _memkv(mem_prompt.reshape(b * n_mem, d), vec(g_mem), wk_b, wv_b,
                          tm=min(256, b * n_mem))
    x1 = _ffn(xp, vec(g_ffn1), w1a, w3a, w2a, gfin, final_norm=False, tm=tm)
    x1 = x1.reshape(b, l, d)
    u, q, k_t, v_t, kaug, vtb, sums = _proj_prompt(x1, vec(g_mix), w_in_b, d_ssm=d_ssm, tm=tm)
    u_tm = u.swapaxes(0, 1).reshape(l * b, d_ssm)
    t_chunk = 128 if l % 128 == 0 else l
    ys_tm, h_last = _s5_prompt(u_tm, b, *s5_w, t_chunk=t_chunk)
    ys = ys_tm.reshape(l, b, d_ssm).swapaxes(0, 1)
    ya = _moba_prompt(q, kaug, vtb, sums.reshape(b, l // MOBA_BLOCK, d_attn))
    x3 = _merge_prompt(x1, ys, ya, vec(g_out_ssm), vec(g_out_attn), w_out_b, vec(g_xattn),
                       wq_b, mem_k.reshape(b, n_mem, d), mem_v.reshape(b, n_mem, d), wo_b, tm=tm)
    y_prompt = _ffn(x3.reshape(b * l, d), vec(g_ffn2), w1b, w3b, w2b, gfin, final_norm=True, tm=tm)
    h_last = h_last[SUBLANES - b:]
    k = k_t.transpose(0, 3, 1, 2)
    v = v_t.transpose(0, 3, 1, 2)
    heads = (1, b, l, n_heads, ATTN_HEAD_DIM)
    state = (1, b, n_groups, n_state)
    memkv = (1, b, n_mem, N_XHEADS, xd)

    xs = x_sample.reshape(ns, d)
    xs1 = _ffn(xs, vec(g_ffn1), w1a, w3a, w2a, gfin, final_norm=False, tm=ns)
    us, qs, ks, vs = _proj(xs1, vec(g_mix), w_in_b, d_ssm=d_ssm, tm=ns)
    yss, hrs, his = _s5_sample(us, state_ssm_re[0].reshape(ns, gp), state_ssm_im[0].reshape(ns, gp), *s5_w)
    yas = _moba_sample(qs, ks, vs, cache_k, cache_v, page_table)
    xs2, qx = _merge_pre(xs1, yss, yas, vec(g_out_ssm), vec(g_out_attn), w_out_b, vec(g_xattn), wq_b)
    ox = _xattn_sample(qx.reshape(ns, 1, d), cache_mem_k[0].reshape(ns, n_mem, d),
                       cache_mem_v[0].reshape(ns, n_mem, d))
    xs3 = _merge_post(xs2, ox.reshape(ns, d), wo_b)
    y_sample = _ffn(xs3, vec(g_ffn2), w1b, w3b, w2b, gfin, final_norm=True, tm=ns)
    sheads = (1, ns, 1, n_heads, ATTN_HEAD_DIM)
    sstate = (1, ns, n_groups, n_state)

    return (y_prompt.reshape(b, l, d), y_sample.reshape(ns, 1, d),
            k.reshape(heads), v.reshape(heads),
            h_last[:, :gp].reshape(state), h_last[:, gp:].reshape(state),
            mem_k.reshape(memkv), mem_v.reshape(memkv),
            ks.reshape(sheads), vs.reshape(sheads),
            hrs.reshape(sstate), his.reshape(sstate))
```

```python
import functools
import math

import jax
import jax.numpy as jnp
from jax import lax
from jax.experimental import pallas as pl
from jax.experimental.pallas import tpu as pltpu

RMS_EPS = 1e-6
NEG_INF = -1e30
SSM_GROUP = 16
ATTN_HEAD_DIM = 64
MOBA_BLOCK = 256
MOBA_TOPK = 3
N_XHEADS = 4
LANES = 128
SUBLANES = 8
VMEM_LIMIT = 56 * 1024 * 1024

F32 = jnp.float32
BF16 = jnp.bfloat16
HIGHEST = lax.Precision.HIGHEST


def _params(*sem):
    return pltpu.CompilerParams(dimension_semantics=sem, vmem_limit_bytes=VMEM_LIMIT)


def _rms(x, g):
    ms = jnp.mean(x * x, axis=-1, keepdims=True)
    return x * lax.rsqrt(ms + RMS_EPS) * g


def _dot(a, b):
    return jnp.dot(a, b, preferred_element_type=F32)


def _dot_t(a, b, precision=None):
    return lax.dot_general(a, b, (((1,), (1,)), ((), ())), preferred_element_type=F32,
                           precision=precision)


def _full(shape):
    n = len(shape)
    return pl.BlockSpec(shape, lambda *_: (0,) * n)


def _ffn_body(x_ref, g_ref, w1_ref, w3_ref, w2_ref, gf_ref, o_ref, *, f_chunk, final_norm):
    x = x_ref[...]
    h = _rms(x, g_ref[...]).astype(BF16)
    acc = jnp.zeros(x.shape, F32)
    for c in range(w1_ref.shape[1] // f_chunk):
        cs = slice(c * f_chunk, (c + 1) * f_chunk)
        a = _dot(h, w1_ref[:, cs])
        b = _dot(h, w3_ref[:, cs])
        act = (a * jax.nn.sigmoid(a) * b).astype(BF16)
        acc = acc + _dot(act, w2_ref[cs, :])
    y = x + 0.5 * acc
    if final_norm:
        y = _rms(y, gf_ref[...])
    o_ref[...] = y


def _ffn(x, g, w1, w3, w2, g_final, *, final_norm, tm):
    n, d = x.shape
    f = w1.shape[1]
    f_chunk = 256 if f % 256 == 0 else f
    return pl.pallas_call(
        functools.partial(_ffn_body, f_chunk=f_chunk, final_norm=final_norm),
        grid=(n // tm,),
        in_specs=[pl.BlockSpec((tm, d), lambda i: (i, 0)), _full((1, d)), _full((d, f)), _full((d, f)),
                  _full((f, d)), _full((1, d))],
        out_specs=pl.BlockSpec((tm, d), lambda i: (i, 0)),
        out_shape=jax.ShapeDtypeStruct((n, d), F32),
        compiler_params=_params("parallel"),
        name="ffn_final" if final_norm else "ffn",
    )(x, g, w1, w3, w2, g_final)


def _proj_body(x_ref, g_ref, w_ref, u_ref, q_ref, k_ref, v_ref, *, d_ssm, d_attn):
    h = _rms(x_ref[...], g_ref[...]).astype(BF16)
    p = _dot(h, w_ref[...])
    u_ref[...] = p[:, :d_ssm]
    q_ref[...] = p[:, d_ssm:d_ssm + d_attn]
    k_ref[...] = p[:, d_ssm + d_attn:d_ssm + 2 * d_attn]
    v_ref[...] = p[:, d_ssm + 2 * d_attn:]


def _proj(x, g, w_in, *, d_ssm, tm):
    n, d = x.shape
    d_attn = (w_in.shape[1] - d_ssm) // 3
    row = lambda w: pl.BlockSpec((tm, w), lambda i: (i, 0))
    return pl.pallas_call(
        functools.partial(_proj_body, d_ssm=d_ssm, d_attn=d_attn),
        grid=(n // tm,),
        in_specs=[row(d), _full((1, d)), _full(w_in.shape)],
        out_specs=[row(d_ssm), row(d_attn), row(d_attn), row(d_attn)],
        out_shape=[jax.ShapeDtypeStruct((n, d_ssm), F32)] + [jax.ShapeDtypeStruct((n, d_attn), F32)] * 3,
        compiler_params=_params("parallel"),
        name="mix_proj",
    )(x, g, w_in)


AUX_BIAS, AUX_ONE, AUX_BLOCK, AUX_LANES, MAX_BLOCKS = 0, 3, 16, 64, 32
LOG2E = math.log2(math.e)
BOUND_SLACK = 1.02
MAX_SPREAD = 60.0


def _alibi_slope(h, n_heads):
    return 2.0 ** (-8.0 * (h + 1) / n_heads)


def _bf16_split3(x):
    hi = x.astype(BF16).astype(F32)
    r = x - hi
    mid = r.astype(BF16).astype(F32)
    lo = (r - mid).astype(BF16).astype(F32)
    return hi, mid, lo


def _proj_prompt_body(x_ref, g_ref, w_ref, u_ref, q_ref, kt_ref, vt_ref, kaug_ref, vtb_ref, sums_ref, kn_ref,
                      *, d_ssm, d_attn, n_heads):
    tm = x_ref.shape[1]
    dh = ATTN_HEAD_DIM
    h = _rms(x_ref[0], g_ref[...]).astype(BF16)
    p = _dot(h, w_ref[...])
    u_ref[0] = p[:, :d_ssm]
    q_ref[0] = p[:, d_ssm:d_ssm + d_attn]
    k = p[:, d_ssm + d_attn:d_ssm + 2 * d_attn]
    v = p[:, d_ssm + 2 * d_attn:]
    kt_ref[0] = k.T.reshape(n_heads, dh, tm)
    vt = v.T.reshape(n_heads, dh, tm)
    vt_ref[0] = vt
    for c in range(tm // MOBA_BLOCK):
        cs = slice(c * MOBA_BLOCK, (c + 1) * MOBA_BLOCK)
        vtb_ref[0, :, c] = vt[:, :, cs].astype(BF16)
        sums_ref[0, c:c + 1, :] = jnp.sum(k[cs], axis=0, keepdims=True)
    pos = pl.program_id(1) * tm + lax.broadcasted_iota(jnp.int32, (tm, AUX_LANES), 0)
    lane = lax.broadcasted_iota(jnp.int32, (tm, AUX_LANES), 1)
    block_hot = (lane - AUX_BLOCK == pos // MOBA_BLOCK).astype(F32)
    posf = pos.astype(F32)
    head_of = (lax.broadcasted_iota(jnp.int32, (d_attn, LANES), 0) // dh
               == lax.broadcasted_iota(jnp.int32, (d_attn, LANES), 1))
    kn_ref[0] = jnp.max(_dot((k * k).astype(BF16), head_of.astype(BF16)), axis=0, keepdims=True)
    pieces = []
    for hh in range(n_heads):
        hi, mid, lo = _bf16_split3((_alibi_slope(hh, n_heads) * LOG2E) * posf)
        aux = jnp.where(lane == AUX_BIAS, hi, jnp.where(lane == AUX_BIAS + 1, mid, jnp.where(
            lane == AUX_BIAS + 2, lo, jnp.where(lane < AUX_ONE + 3, 1.0, block_hot))))
        pieces += [k[:, hh * dh:(hh + 1) * dh], aux]
    kaug_ref[0] = jnp.concatenate(pieces, axis=1).astype(BF16)


def _proj_prompt(x, g, w_in, *, d_ssm, tm):
    b, l, d = x.shape
    d_attn = (w_in.shape[1] - d_ssm) // 3
    dh = ATTN_HEAD_DIM
    n_heads = d_attn // dh
    nb = l // MOBA_BLOCK
    tb = tm // MOBA_BLOCK
    assert nb <= MAX_BLOCKS and tm % MOBA_BLOCK == 0
    row = lambda w: pl.BlockSpec((1, tm, w), lambda i, j: (i, j, 0))
    tr = pl.BlockSpec((1, n_heads, dh, tm), lambda i, j: (i, 0, 0, j))
    return pl.pallas_call(
        functools.partial(_proj_prompt_body, d_ssm=d_ssm, d_attn=d_attn, n_heads=n_heads),
        grid=(b, l // tm),
        in_specs=[row(d), _full((1, d)), _full(w_in.shape)],
        out_specs=[row(d_ssm), row(d_attn), tr, tr, row(n_heads * (dh + AUX_LANES)),
                   pl.BlockSpec((1, n_heads, tb, dh, MOBA_BLOCK), lambda i, j: (i, 0, j, 0, 0)),
                   pl.BlockSpec((1, tb, d_attn), lambda i, j: (i * (l // tm) + j, 0, 0)),
                   pl.BlockSpec((1, 1, LANES), lambda i, j: (i * (l // tm) + j, 0, 0))],
        out_shape=[jax.ShapeDtypeStruct((b, l, d_ssm), F32), jax.ShapeDtypeStruct((b, l, d_attn), F32),
                   jax.ShapeDtypeStruct((b, n_heads, dh, l), F32), jax.ShapeDtypeStruct((b, n_heads, dh, l), F32),
                   jax.ShapeDtypeStruct((b, l, n_heads * (dh + AUX_LANES)), BF16),
                   jax.ShapeDtypeStruct((b, n_heads, nb, dh, MOBA_BLOCK), BF16),
                   jax.ShapeDtypeStruct((b * l // tm, tb, d_attn), F32),
                   jax.ShapeDtypeStruct((b * l // tm, 1, LANES), F32)],
        compiler_params=_params("parallel", "parallel"),
        name="mix_proj_prompt",
    )(x, g, w_in)


def _s5_discretise(a_re, a_im, log_dt, b_re, b_im):
    dt = jnp.exp(log_dt)[:, None]
    mag = jnp.exp(a_re * dt)
    abar_re = mag * jnp.cos(a_im * dt)
    abar_im = mag * jnp.sin(a_im * dt)
    den = a_re * a_re + a_im * a_im
    nr = abar_re - 1.0
    f_re = (nr * a_re + abar_im * a_im) / den
    f_im = (abar_im * a_re - nr * a_im) / den
    bbar_re = f_re[..., None] * b_re - f_im[..., None] * b_im
    bbar_im = f_re[..., None] * b_im + f_im[..., None] * b_re
    return abar_re, abar_im, bbar_re, bbar_im


def _s5_matrices(a_re, a_im, log_dt, b_re, b_im, c_re, c_im):
    g, p, c = b_re.shape
    abar_re, abar_im, bbar_re, bbar_im = _s5_discretise(a_re, a_im, log_dt, b_re, b_im)
    eye = jnp.eye(g, dtype=F32)
    bm_re = jnp.einsum('gpc,gh->gchp', bbar_re, eye).reshape(g * c, g * p)
    bm_im = jnp.einsum('gpc,gh->gchp', bbar_im, eye).reshape(g * c, g * p)
    bmat = jnp.concatenate([bm_re, bm_im], axis=1).astype(BF16)
    cm_re = jnp.einsum('gcp,gh->gphc', c_re, eye).reshape(g * p, g * c)
    cm_im = jnp.einsum('gcp,gh->gphc', c_im, eye).reshape(g * p, g * c)
    cmat = jnp.concatenate([cm_re, -cm_im], axis=0).astype(BF16)
    return abar_re.reshape(1, g * p), abar_im.reshape(1, g * p), bmat, cmat


def _s5_output(h_bf, u, cmat_ref, d_ref, wg_ref, bg_ref):
    y = _dot(h_bf, cmat_ref[...]) + d_ref[...] * u
    z = jax.nn.gelu(y, approximate=True)
    return z * jax.nn.sigmoid(_dot(z.astype(BF16), wg_ref[...]) + bg_ref[...])


def _s5_prompt_body(u_ref, ar_ref, ai_ref, bmat_ref, cmat_ref, d_ref, wg_ref, bg_ref,
                    y_ref, hlast_ref, hbuf, carry, *, nbatch):
    gp = ar_ref.shape[1]
    rows = u_ref.shape[0]

    @pl.when(pl.program_id(0) == 0)
    def _():
        carry[...] = jnp.zeros(carry.shape, F32)

    u = u_ref[...]
    hbuf[...] = _dot(u.astype(BF16), bmat_ref[...])

    ar = jnp.broadcast_to(ar_ref[...], (SUBLANES, gp))
    ai = jnp.broadcast_to(ai_ref[...], (SUBLANES, gp))
    lower = lax.broadcasted_iota(jnp.int32, (SUBLANES, gp), 0) < nbatch

    def step(i, st):
        hr, hi = st
        r0 = pl.multiple_of(i * SUBLANES, SUBLANES)
        bur = hbuf[pl.ds(r0, SUBLANES), :gp]
        bui = hbuf[pl.ds(r0, SUBLANES), gp:]
        pr = pltpu.roll(hr, nbatch, 0)
        pi_ = pltpu.roll(hi, nbatch, 0)
        lr = ar * pr - ai * pi_ + bur
        li = ar * pi_ + ai * pr + bui
        qr = pltpu.roll(lr, nbatch, 0)
        qi = pltpu.roll(li, nbatch, 0)
        ur = ar * qr - ai * qi + bur
        ui = ar * qi + ai * qr + bui
        hr = jnp.where(lower, lr, ur)
        hi = jnp.where(lower, li, ui)
        hbuf[pl.ds(r0, SUBLANES), :gp] = hr
        hbuf[pl.ds(r0, SUBLANES), gp:] = hi
        return hr, hi

    hr, hi = lax.fori_loop(0, rows // SUBLANES, step, (carry[:, :gp], carry[:, gp:]))
    carry[:, :gp] = hr
    carry[:, gp:] = hi
    hlast_ref[:, :gp] = hr
    hlast_ref[:, gp:] = hi
    y_ref[...] = _s5_output(hbuf[...].astype(BF16), u, cmat_ref, d_ref, wg_ref, bg_ref)


def _s5_prompt(u_tm, nbatch, ar, ai, bmat, cmat, d, w_glu, b_glu, *, t_chunk):
    rows_total, d_ssm = u_tm.shape
    gp = ar.shape[1]
    rows = t_chunk * nbatch
    assert 2 * nbatch == SUBLANES and rows_total % rows == 0
    return pl.pallas_call(
        functools.partial(_s5_prompt_body, nbatch=nbatch),
        grid=(rows_total // rows,),
        in_specs=[pl.BlockSpec((rows, d_ssm), lambda i: (i, 0)), _full((1, gp)), _full((1, gp)),
                  _full(bmat.shape), _full(cmat.shape), _full((1, d_ssm)), _full(w_glu.shape),
                  _full((1, d_ssm))],
        out_specs=[pl.BlockSpec((rows, d_ssm), lambda i: (i, 0)), _full((SUBLANES, 2 * gp))],
        out_shape=[jax.ShapeDtypeStruct((rows_total, d_ssm), F32),
                   jax.ShapeDtypeStruct((SUBLANES, 2 * gp), F32)],
        scratch_shapes=[pltpu.VMEM((rows, 2 * gp), F32), pltpu.VMEM((SUBLANES, 2 * gp), F32)],
        compiler_params=_params("arbitrary"),
        name="s5_prompt",
    )(u_tm, ar, ai, bmat, cmat, d, w_glu, b_glu)


def _s5_sample_body(u_ref, h0r_ref, h0i_ref, ar_ref, ai_ref, bmat_ref, cmat_ref, d_ref, wg_ref, bg_ref,
                    y_ref, hr_ref, hi_ref):
    gp = ar_ref.shape[1]
    u = u_ref[...]
    bu = _dot(u.astype(BF16), bmat_ref[...])
    ar, ai = ar_ref[...], ai_ref[...]
    h0r, h0i = h0r_ref[...], h0i_ref[...]
    hr = bu[:, :gp] + ar * h0r - ai * h0i
    hi = bu[:, gp:] + ar * h0i + ai * h0r
    hr_ref[...] = hr
    hi_ref[...] = hi
    h_bf = jnp.concatenate([hr, hi], axis=1).astype(BF16)
    y_ref[...] = _s5_output(h_bf, u, cmat_ref, d_ref, wg_ref, bg_ref)


def _s5_sample(u, h0r, h0i, ar, ai, bmat, cmat, d, w_glu, b_glu):
    n, d_ssm = u.shape
    gp = ar.shape[1]
    args = (u, h0r, h0i, ar, ai, bmat, cmat, d, w_glu, b_glu)
    return pl.pallas_call(
        _s5_sample_body,
        grid=(1,),
        in_specs=[_full(a.shape) for a in args],
        out_specs=[_full((n, d_ssm)), _full((n, gp)), _full((n, gp))],
        out_shape=[jax.ShapeDtypeStruct((n, d_ssm), F32), jax.ShapeDtypeStruct((n, gp), F32),
                   jax.ShapeDtypeStruct((n, gp), F32)],
        compiler_params=_params("arbitrary"),
        name="s5_sample",
    )(*args)


def _top_blocks(gate, brow, n_valid):
    bf = brow.astype(F32)
    g = jnp.where(brow < n_valid, gate, NEG_INF)
    sel = jnp.zeros(gate.shape, jnp.bool_)
    for _ in range(MOBA_TOPK):
        mx = jnp.max(g, axis=0, keepdims=True)
        first = jnp.min(jnp.where(g == mx, bf, float(gate.shape[0])), axis=0, keepdims=True)
        pick = bf == first
        sel = jnp.logical_or(sel, pick)
        g = jnp.where(pick, -jnp.inf, g)
    return jnp.logical_and(sel, brow < n_valid)


def _moba_prompt_body(q_ref, kaug_ref, vtb_ref, sums_ref, kn_ref, o_ref, qaug, acc, s_scr, p_scr, *, n_heads):
    j = pl.program_id(1)
    blk = MOBA_BLOCK
    dh = ATTN_HEAD_DIM
    ka = dh + AUX_LANES
    nb = sums_ref.shape[1]
    t0 = pl.multiple_of(j * blk, blk)
    scale = dh ** -0.5
    qt = q_ref[0].T
    brow = lax.broadcasted_iota(jnp.int32, (MAX_BLOCKS, blk), 0)
    crow = lax.broadcasted_iota(jnp.int32, (AUX_BLOCK, blk), 0)
    t0f = jnp.broadcast_to(t0.astype(F32), (AUX_BLOCK, blk))
    kmax2 = jnp.max(kn_ref[0], axis=0, keepdims=True)
    hlane = lax.broadcasted_iota(jnp.int32, kmax2.shape, 1)
    tq = lax.broadcasted_iota(jnp.int32, (1, blk), 1).astype(F32)
    shifts, spreads = [], []
    for h in range(n_heads):
        slope = _alibi_slope(h, n_heads)
        qth = qt[h * dh:(h + 1) * dh]
        means = sums_ref[0, :, h * dh:(h + 1) * dh] * (1.0 / blk)
        gate = jnp.dot(means, qth, precision=HIGHEST, preferred_element_type=F32)
        if nb < MAX_BLOCKS:
            gate = jnp.concatenate([gate, jnp.full((MAX_BLOCKS - nb, blk), NEG_INF, F32)], axis=0)
        keep = jnp.logical_or(_top_blocks(gate, brow, j), brow == j)
        mask = jnp.where(keep, 0.0, NEG_INF)
        hi, mid, lo = _bf16_split3((-slope * LOG2E) * t0f)
        const = jnp.where(crow < AUX_ONE, 1.0, jnp.where(crow == AUX_ONE, hi, jnp.where(
            crow == AUX_ONE + 1, mid, jnp.where(crow == AUX_ONE + 2, lo, 0.0))))
        rest = jnp.zeros((AUX_LANES - AUX_BLOCK - MAX_BLOCKS, blk), F32)
        qaug[h] = jnp.concatenate([qth * (scale * LOG2E), const, mask, rest], axis=0).astype(BF16)
        k2 = jnp.max(jnp.where(hlane == h, kmax2, 0.0), axis=1, keepdims=True)
        bound = jnp.sqrt(jnp.sum(qth * qth, axis=0, keepdims=True) * k2) * (scale * BOUND_SLACK)
        shifts.append((bound + slope * tq) * LOG2E)
        spreads.append(bound)
    shift = jnp.concatenate(shifts, axis=0)
    safe = 2.0 * jnp.max(jnp.concatenate(spreads, axis=0)) < MAX_SPREAD

    def scores(n, h):
        r = pl.multiple_of(n * blk, blk)
        return _dot(kaug_ref[0, pl.ds(r, blk), h * ka:(h + 1) * ka], qaug[h])

    causal = (lax.broadcasted_iota(jnp.int32, (blk, blk), 0) <= lax.broadcasted_iota(jnp.int32, (blk, blk), 1))

    def finish(l_all):
        out_t = jnp.concatenate([acc[h * dh:(h + 1) * dh, :] / l_all[h:h + 1] for h in range(n_heads)], axis=0)
        o_ref[0] = out_t.T

    @pl.when(safe)
    def _():
        def tile(n, l_all, own):
            ls = []
            for h in range(n_heads):
                s = scores(n, h)
                if own:
                    s = jnp.where(causal, s, NEG_INF)
                p = jnp.exp2(s - shift[h:h + 1])
                p_scr[h] = p.astype(BF16)
                ls.append(jnp.sum(p, axis=0, keepdims=True))
            for h in range(n_heads):
                hs = slice(h * dh, (h + 1) * dh)
                pv = _dot(vtb_ref[0, h, n], p_scr[h])
                acc[hs, :] = pv if own else acc[hs, :] + pv
            l_new = jnp.concatenate(ls, axis=0)
            return l_new if own else l_all + l_new

        finish(lax.fori_loop(0, j, lambda n, l_all: tile(n, l_all, False), tile(j, None, True)))

    @pl.when(jnp.logical_not(safe))
    def _():
        def tile(n, st, own):
            ms, ls, alphas = [], [], []
            for h in range(n_heads):
                s_scr[h] = scores(n, h)
            for h in range(n_heads):
                s = s_scr[h]
                if own:
                    s = jnp.where(causal, s, NEG_INF)
                    m_new = jnp.max(s, axis=0, keepdims=True)
                else:
                    m_new = jnp.maximum(st[0][h:h + 1], jnp.max(s, axis=0, keepdims=True))
                    alphas.append(jnp.exp2(st[0][h:h + 1] - m_new))
                p = jnp.exp2(s - m_new)
                p_scr[h] = p.astype(BF16)
                ms.append(m_new)
                l_new = jnp.sum(p, axis=0, keepdims=True)
                ls.append(l_new if own else alphas[h] * st[1][h:h + 1] + l_new)
            for h in range(n_heads):
                hs = slice(h * dh, (h + 1) * dh)
                pv = _dot(vtb_ref[0, h, n], p_scr[h])
                acc[hs, :] = pv if own else alphas[h] * acc[hs, :] + pv
            return jnp.concatenate(ms, axis=0), jnp.concatenate(ls, axis=0)

        finish(lax.fori_loop(0, j, lambda n, st: tile(n, st, False), tile(j, None, True))[1])


def _moba_prompt(q, kaug, vtb, sums, knorm):
    b, l, d = q.shape
    nb = l // MOBA_BLOCK
    n_heads = d // ATTN_HEAD_DIM
    ka = ATTN_HEAD_DIM + AUX_LANES
    once = pl.Buffered(1)
    return pl.pallas_call(
        functools.partial(_moba_prompt_body, n_heads=n_heads),
        grid=(b, nb),
        in_specs=[pl.BlockSpec((1, MOBA_BLOCK, d), lambda i, j: (i, j, 0)),
                  pl.BlockSpec((1, l, n_heads * ka), lambda i, j: (i, 0, 0), pipeline_mode=once),
                  pl.BlockSpec((1, n_heads, nb, ATTN_HEAD_DIM, MOBA_BLOCK), lambda i, j: (i, 0, 0, 0, 0),
                               pipeline_mode=once),
                  pl.BlockSpec((1, nb, d), lambda i, j: (i, 0, 0)),
                  pl.BlockSpec((1,) + knorm.shape[1:], lambda i, j: (i, 0, 0))],
        out_specs=pl.BlockSpec((1, MOBA_BLOCK, d), lambda i, j: (i, j, 0)),
        out_shape=jax.ShapeDtypeStruct((b, l, d), F32),
        scratch_shapes=[pltpu.VMEM((n_heads, ka, MOBA_BLOCK), BF16), pltpu.VMEM((d, MOBA_BLOCK), F32),
                        pltpu.VMEM((n_heads, MOBA_BLOCK, MOBA_BLOCK), F32),
                        pltpu.VMEM((n_heads, MOBA_BLOCK, MOBA_BLOCK), BF16)],
        compiler_params=_params("parallel", "arbitrary"),
        name="moba_prompt",
    )(q, kaug, vtb, sums, knorm)


def _moba_kpass_body(pt_ref, q_ref, *refs, pages, n_heads):
    del pt_ref
    k_refs, (lg_ref, ps_ref) = refs[:pages], refs[pages:]
    s = pl.program_id(1)
    dh, page = k_refs[0].shape[3:]

    @pl.when(s == 0)
    def _():
        ps_ref[...] = jnp.zeros(ps_ref.shape, F32)

    qcols = q_ref[0]
    qb = [jnp.broadcast_to(qcols[:, h:h + 1], (dh, page)) for h in range(n_heads)]
    plane = lax.broadcasted_iota(jnp.int32, (n_heads, ps_ref.shape[2]), 1)
    psum = ps_ref[0]
    for i in range(pages):
        lg = jnp.concatenate([jnp.sum(k_refs[i][0, 0, h] * qb[h], axis=0, keepdims=True)
                              for h in range(n_heads)], axis=0)
        lg_ref[0, :, i * page:(i + 1) * page] = lg
        psum = jnp.where(plane == s * pages + i, jnp.sum(lg, axis=1, keepdims=True), psum)
    ps_ref[0] = psum


def _moba_kpass(q_t, cache_kt, page_table, *, pages):
    n, dh, n_heads = q_t.shape
    n_pages = page_table.shape[1]
    page = cache_kt.shape[4]
    k_spec = lambda i: pl.BlockSpec((1, 1, n_heads, dh, page),
                                    lambda b, s, pt: (0, pt[b, s * pages + i], 0, 0, 0))
    grid_spec = pltpu.PrefetchScalarGridSpec(
        num_scalar_prefetch=1,
        grid=(n, n_pages // pages),
        in_specs=[pl.BlockSpec((1, dh, n_heads), lambda b, s, pt: (b, 0, 0))]
                 + [k_spec(i) for i in range(pages)],
        out_specs=[pl.BlockSpec((1, n_heads, pages * page), lambda b, s, pt: (b, 0, s)),
                   pl.BlockSpec((1, n_heads, n_pages), lambda b, s, pt: (b, 0, 0))])
    return pl.pallas_call(
        functools.partial(_moba_kpass_body, pages=pages, n_heads=n_heads),
        grid_spec=grid_spec,
        out_shape=[jax.ShapeDtypeStruct((n, n_heads, n_pages * page), F32),
                   jax.ShapeDtypeStruct((n, n_heads, n_pages), F32)],
        compiler_params=_params("parallel", "arbitrary"),
        name="moba_sample_kpass",
    )(page_table, q_t, *([cache_kt] * pages))


def _moba_select_body(ps_ref, idx_ref, *, pages_per_block, n_blocks):
    ps = ps_ref[...]
    lanes = ps.shape[1]
    lane = lax.broadcasted_iota(jnp.int32, (1, lanes), 1)
    g = ps
    for i in range(1, pages_per_block):
        g = g + pltpu.roll(ps, lanes - i, 1)
    g = g * (1.0 / MOBA_BLOCK)
    is_block = jnp.logical_and(lane % pages_per_block == 0, lane < n_blocks * pages_per_block)
    g = jnp.where(is_block, g, -jnp.inf)
    lanef = lane.astype(F32)
    out = jnp.zeros(idx_ref.shape, F32)
    olane = lax.broadcasted_iota(jnp.int32, idx_ref.shape, 1)
    for r in range(MOBA_TOPK):
        mx = jnp.max(g, axis=1, keepdims=True)
        first = jnp.min(jnp.where(g == mx, lanef, float(lanes)), axis=1, keepdims=True)
        out = jnp.where(olane == r, first * (1.0 / pages_per_block), out)
        g = jnp.where(lanef == first, -jnp.inf, g)
    idx_ref[...] = out.astype(jnp.int32)


def _moba_select(page_sums2, *, pages_per_block, n_blocks):
    rows, lanes = page_sums2.shape
    return pl.pallas_call(
        functools.partial(_moba_select_body, pages_per_block=pages_per_block, n_blocks=n_blocks),
        grid=(1,),
        in_specs=[_full((rows, lanes))],
        out_specs=_full((rows, LANES)),
        out_shape=jax.ShapeDtypeStruct((rows, LANES), jnp.int32),
        compiler_params=_params("arbitrary"),
        name="moba_sample_select",
    )(page_sums2)


def _moba_attend_body(blk_ref, pg_ref, q_ref, kn_ref, vn_ref, *refs, pages_per_block, n_heads, past_len):
    del pg_ref
    n_lg = MOBA_TOPK
    n_v = MOBA_TOPK * pages_per_block
    lg_refs, v_refs, o_ref = refs[:n_heads * n_lg], refs[n_heads * n_lg:n_heads * (n_lg + n_v)], refs[-1]
    b = pl.program_id(0)
    dh = ATTN_HEAD_DIM
    blk = MOBA_BLOCK
    page = blk // pages_per_block
    scale = dh ** -0.5
    lane = lax.broadcasted_iota(jnp.int32, (1, blk), 1)
    qk = q_ref[0] * kn_ref[0]
    outs = []
    for h in range(n_heads):
        hs = slice(h * dh, (h + 1) * dh)
        slope = _alibi_slope(h, n_heads)
        s_own = jnp.sum(qk[:, hs], axis=1, keepdims=True) * scale
        ss = []
        for i in range(n_lg):
            pos = blk_ref[b, h, i] * blk + lane
            ss.append(lg_refs[h * n_lg + i][0] * scale - slope * (past_len - pos).astype(F32))
        m = s_own
        for s in ss:
            m = jnp.maximum(m, jnp.max(s, axis=1, keepdims=True))
        p_own = jnp.exp(s_own - m)
        l = p_own
        acc = p_own * vn_ref[0][:, hs]
        for i in range(n_lg):
            p = jnp.exp(ss[i] - m)
            l = l + jnp.sum(p, axis=1, keepdims=True)
            for c in range(pages_per_block):
                vt = v_refs[h * n_v + i * pages_per_block + c][0, 0, 0].astype(BF16)
                acc = acc + _dot_t(p[:, c * page:(c + 1) * page].astype(BF16), vt)
        outs.append(acc / l)
    o_ref[0] = jnp.concatenate(outs, axis=1)


def _moba_attend(blocks, pages_idx, q3, kn3, vn3, logits, cache_vt, *, pages_per_block, past_len):
    n, _, d = q3.shape
    _, _, n_heads, dh, page = cache_vt.shape
    n_blocks = logits.shape[0] // (n * n_heads)
    row = pl.BlockSpec((1, 1, d), lambda b, bl, pg: (b, 0, 0))
    lg_spec = lambda h, i: pl.BlockSpec(
        (1, 1, MOBA_BLOCK), lambda b, bl, pg: ((b * n_heads + h) * n_blocks + bl[b, h, i], 0, 0))
    v_spec = lambda h, i: pl.BlockSpec(
        (1, 1, 1, dh, page), lambda b, bl, pg: (0, pg[b, h, i], h, 0, 0))
    n_v = MOBA_TOPK * pages_per_block
    lg_specs = [lg_spec(h, i) for h in range(n_heads) for i in range(MOBA_TOPK)]
    v_specs = [v_spec(h, i) for h in range(n_heads) for i in range(n_v)]
    grid_spec = pltpu.PrefetchScalarGridSpec(
        num_scalar_prefetch=2,
        grid=(n,),
        in_specs=[row, row, row] + lg_specs + v_specs,
        out_specs=row)
    return pl.pallas_call(
        functools.partial(_moba_attend_body, pages_per_block=pages_per_block, n_heads=n_heads,
                          past_len=past_len),
        grid_spec=grid_spec,
        out_shape=jax.ShapeDtypeStruct((n, 1, d), F32),
        compiler_params=_params("parallel"),
        name="moba_sample_attend",
    )(blocks, pages_idx, q3, kn3, vn3, *([logits] * len(lg_specs)), *([cache_vt] * len(v_specs)))


def _moba_sample(q, k_new, v_new, cache_k, cache_v, page_table):
    n, d = q.shape
    _, n_pool, page, n_heads, dh = cache_k.shape
    n_pages = page_table.shape[1]
    past_len = n_pages * page
    pages_per_block = MOBA_BLOCK // page
    n_blocks = n_pages // pages_per_block
    assert n_blocks >= MOBA_TOPK and n_pages % LANES == 0
    q3 = q.reshape(n, 1, d)
    cache_kt = cache_k.transpose(0, 1, 3, 4, 2)
    cache_vt = cache_v.transpose(0, 1, 3, 4, 2)
    kpass_pages = 16 if n_pages % 16 == 0 else 1
    logits, page_sums = _moba_kpass(q.reshape(n, n_heads, dh).swapaxes(1, 2), cache_kt, page_table,
                                    pages=kpass_pages)
    blocks = _moba_select(page_sums.reshape(n * n_heads, n_pages), pages_per_block=pages_per_block,
                          n_blocks=n_blocks)[:, :MOBA_TOPK].reshape(n, n_heads, MOBA_TOPK)
    logical = (blocks[..., None] * pages_per_block + jnp.arange(pages_per_block)).reshape(n, n_heads, -1)
    pages_idx = jnp.take_along_axis(page_table[:, None, :], logical, axis=2)
    y = _moba_attend(blocks, pages_idx, q3, k_new.reshape(n, 1, d), v_new.reshape(n, 1, d),
                     logits.reshape(n * n_heads * n_blocks, 1, MOBA_BLOCK), cache_vt,
                     pages_per_block=pages_per_block, past_len=past_len)
    return y.reshape(n, d)


def _memkv_body(m_ref, g_ref, wk_ref, wv_ref, k_ref, v_ref):
    h = _rms(m_ref[...], g_ref[...]).astype(BF16)
    k_ref[...] = _dot(h, wk_ref[...])
    v_ref[...] = _dot(h, wv_ref[...])


def _memkv(mem, g, wk, wv, *, tm):
    n, d = mem.shape
    row = pl.BlockSpec((tm, d), lambda i: (i, 0))
    return pl.pallas_call(
        _memkv_body,
        grid=(n // tm,),
        in_specs=[row, _full((1, d)), _full(wk.shape), _full(wv.shape)],
        out_specs=[row, row],
        out_shape=[jax.ShapeDtypeStruct((n, d), F32)] * 2,
        compiler_params=_params("parallel"),
        name="mem_kv",
    )(mem, g, wk, wv)


def _mix_out(x, ys, ya, gs, ga, w_out_ref, d_ssm):
    ysn = _rms(ys, gs).astype(BF16)
    yan = _rms(ya, ga).astype(BF16)
    return x + _dot(ysn, w_out_ref[:d_ssm, :]) + _dot(yan, w_out_ref[d_ssm:, :])


def _xattn_heads(q, mk_ref, mv_ref):
    d = q.shape[1]
    xd = d // N_XHEADS
    outs = []
    for h in range(N_XHEADS):
        hs = slice(h * xd, (h + 1) * xd)
        s = _dot_t(q[:, hs].astype(BF16), mk_ref[0, :, hs].astype(BF16)) * (xd ** -0.5)
        m = jnp.max(s, axis=1, keepdims=True)
        p = jnp.exp(s - m)
        l = jnp.sum(p, axis=1, keepdims=True)
        outs.append(_dot(p.astype(BF16), mv_ref[0, :, hs].astype(BF16)) / l)
    return jnp.concatenate(outs, axis=1)


def _merge_prompt_body(x_ref, ys_ref, ya_ref, gs_ref, ga_ref, wout_ref, gx_ref, wq_ref, mk_ref, mv_ref,
                       wo_ref, o_ref, *, d_ssm):
    x2 = _mix_out(x_ref[0], ys_ref[0], ya_ref[0], gs_ref[...], ga_ref[...], wout_ref, d_ssm)
    q = _dot(_rms(x2, gx_ref[...]).astype(BF16), wq_ref[...])
    o = _xattn_heads(q, mk_ref, mv_ref)
    o_ref[0] = x2 + _dot(o.astype(BF16), wo_ref[...])


def _merge_prompt(x, ys, ya, gs, ga, w_out, gx, wq, mk, mv, wo, *, tm):
    b, l, d = x.shape
    d_ssm = ys.shape[2]
    d_attn = ya.shape[2]
    n_mem = mk.shape[1]
    row = lambda w: pl.BlockSpec((1, tm, w), lambda i, j: (i, j, 0))
    mem = pl.BlockSpec((1, n_mem, d), lambda i, j: (i, 0, 0))
    return pl.pallas_call(
        functools.partial(_merge_prompt_body, d_ssm=d_ssm),
        grid=(b, l // tm),
        in_specs=[row(d), row(d_ssm), row(d_attn), _full((1, d_ssm)), _full((1, d_attn)),
                  _full(w_out.shape), _full((1, d)), _full(wq.shape), mem, mem, _full(wo.shape)],
        out_specs=row(d),
        out_shape=jax.ShapeDtypeStruct((b, l, d), F32),
        compiler_params=_params("parallel", "parallel"),
        name="merge_prompt",
    )(x, ys, ya, gs, ga, w_out, gx, wq, mk, mv, wo)


def _merge_pre_body(x_ref, ys_ref, ya_ref, gs_ref, ga_ref, wout_ref, gx_ref, wq_ref, x2_ref, q_ref, *, d_ssm):
    x2 = _mix_out(x_ref[...], ys_ref[...], ya_ref[...], gs_ref[...], ga_ref[...], wout_ref, d_ssm)
    x2_ref[...] = x2
    q_ref[...] = _dot(_rms(x2, gx_ref[...]).astype(BF16), wq_ref[...])


def _merge_pre(x, ys, ya, gs, ga, w_out, gx, wq):
    n, d = x.shape
    args = (x, ys, ya, gs, ga, w_out, gx, wq)
    return pl.pallas_call(
        functools.partial(_merge_pre_body, d_ssm=ys.shape[1]),
        grid=(1,),
        in_specs=[_full(a.shape) for a in args],
        out_specs=[_full((n, d)), _full((n, d))],
        out_shape=[jax.ShapeDtypeStruct((n, d), F32)] * 2,
        compiler_params=_params("arbitrary"),
        name="merge_sample_pre",
    )(*args)


def _xattn_sample_body(q_ref, mk_ref, mv_ref, o_ref):
    o_ref[0] = _xattn_heads(q_ref[0], mk_ref, mv_ref)


def _xattn_sample(q3, mk, mv):
    n, _, d = q3.shape
    n_mem = mk.shape[1]
    row = pl.BlockSpec((1, 1, d), lambda i: (i, 0, 0))
    mem = pl.BlockSpec((1, n_mem, d), lambda i: (i, 0, 0))
    return pl.pallas_call(
        _xattn_sample_body,
        grid=(n,),
        in_specs=[row, mem, mem],
        out_specs=row,
        out_shape=jax.ShapeDtypeStruct((n, 1, d), F32),
        compiler_params=_params("parallel"),
        name="xattn_sample",
    )(q3, mk, mv)


def _merge_post_body(x_ref, o_ref, wo_ref, y_ref):
    y_ref[...] = x_ref[...] + _dot(o_ref[...].astype(BF16), wo_ref[...])


def _merge_post(x2, o, wo):
    n, d = x2.shape
    return pl.pallas_call(
        _merge_post_body,
        grid=(1,),
        in_specs=[_full((n, d)), _full((n, d)), _full(wo.shape)],
        out_specs=_full((n, d)),
        out_shape=jax.ShapeDtypeStruct((n, d), F32),
        compiler_params=_params("arbitrary"),
        name="merge_sample_post",
    )(x2, o, wo)


def kernel(x_prompt, x_sample, mem_prompt, cache_k, cache_v, page_table, state_ssm_re, state_ssm_im, cache_mem_k, cache_mem_v, g_ffn1, w1_ffn1, w3_ffn1, w2_ffn1, g_mix, w_in, ssm_a_re, ssm_a_im, ssm_log_dt, ssm_b_re, ssm_b_im, ssm_c_re, ssm_c_im, ssm_d, w_glu, b_glu, g_out_ssm, g_out_attn, w_out, g_xattn, g_mem, wq_x, wk_x, wv_x, wo_x, g_ffn2, w1_ffn2, w3_ffn2, w2_ffn2, g_final):
    depth = g_ffn1.shape[0]
    assert depth == 1
    b, l, d = x_prompt.shape
    ns, ls, _ = x_sample.shape
    assert ls == 1
    n_groups, n_state = ssm_a_re.shape[1:]
    gp = n_groups * n_state
    d_ssm = n_groups * SSM_GROUP
    d_attn = (w_in.shape[2] - d_ssm) // 3
    n_heads = d_attn // ATTN_HEAD_DIM
    n_mem = mem_prompt.shape[1]
    xd = d // N_XHEADS

    vec = lambda a: a[0].reshape(1, -1)
    wb = lambda a: a[0].astype(BF16)
    w1a, w3a, w2a = wb(w1_ffn1), wb(w3_ffn1), wb(w2_ffn1)
    w1b, w3b, w2b = wb(w1_ffn2), wb(w3_ffn2), wb(w2_ffn2)
    w_in_b, w_out_b, w_glu_b = wb(w_in), wb(w_out), wb(w_glu)
    wq_b, wk_b, wv_b, wo_b = wb(wq_x), wb(wk_x), wb(wv_x), wb(wo_x)
    gfin = g_final.reshape(1, -1)
    ar, ai, bmat, cmat = _s5_matrices(ssm_a_re[0], ssm_a_im[0], ssm_log_dt[0], ssm_b_re[0], ssm_b_im[0],
                                      ssm_c_re[0], ssm_c_im[0])
    s5_w = (ar, ai, bmat, cmat, vec(ssm_d), w_glu_b, vec(b_glu))

    tm = 512 if l % 512 == 0 else l
    xp = x_prompt.reshape(b * l, d)
    mem_k, mem_v = _memkv(mem_prompt.reshape(b * n_mem, d), vec(g_mem), wk_b, wv_b,
                          tm=min(256, b * n_mem))
    x1 = _ffn(xp, vec(g_ffn1), w1a, w3a, w2a, gfin, final_norm=False, tm=tm)
    x1 = x1.reshape(b, l, d)
    u, q, k_t, v_t, kaug, vtb, sums, knorm = _proj_prompt(x1, vec(g_mix), w_in_b, d_ssm=d_ssm, tm=tm)
    u_tm = u.swapaxes(0, 1).reshape(l * b, d_ssm)
    t_chunk = 128 if l % 128 == 0 else l
    ys_tm, h_last = _s5_prompt(u_tm, b, *s5_w, t_chunk=t_chunk)
    ys = ys_tm.reshape(l, b, d_ssm).swapaxes(0, 1)
    ya = _moba_prompt(q, kaug, vtb, sums.reshape(b, l // MOBA_BLOCK, d_attn), knorm.reshape(b, l // tm, LANES))
    x3 = _merge_prompt(x1, ys, ya, vec(g_out_ssm), vec(g_out_attn), w_out_b, vec(g_xattn),
                       wq_b, mem_k.reshape(b, n_mem, d), mem_v.reshape(b, n_mem, d), wo_b, tm=tm)
    y_prompt = _ffn(x3.reshape(b * l, d), vec(g_ffn2), w1b, w3b, w2b, gfin, final_norm=True, tm=tm)
    h_last = h_last[SUBLANES - b:]
    k = k_t.transpose(0, 3, 1, 2)
    v = v_t.transpose(0, 3, 1, 2)
    heads = (1, b, l, n_heads, ATTN_HEAD_DIM)
    state = (1, b, n_groups, n_state)
    memkv = (1, b, n_mem, N_XHEADS, xd)

    xs = x_sample.reshape(ns, d)
    xs1 = _ffn(xs, vec(g_ffn1), w1a, w3a, w2a, gfin, final_norm=False, tm=ns)
    us, qs, ks, vs = _proj(xs1, vec(g_mix), w_in_b, d_ssm=d_ssm, tm=ns)
    yss, hrs, his = _s5_sample(us, state_ssm_re[0].reshape(ns, gp), state_ssm_im[0].reshape(ns, gp), *s5_w)
    yas = _moba_sample(qs, ks, vs, cache_k, cache_v, page_table)
    xs2, qx = _merge_pre(xs1, yss, yas, vec(g_out_ssm), vec(g_out_attn), w_out_b, vec(g_xattn), wq_b)
    ox = _xattn_sample(qx.reshape(ns, 1, d), cache_mem_k[0].reshape(ns, n_mem, d),
                       cache_mem_v[0].reshape(ns, n_mem, d))
    xs3 = _merge_post(xs2, ox.reshape(ns, d), wo_b)
    y_sample = _ffn(xs3, vec(g_ffn2), w1b, w3b, w2b, gfin, final_norm=True, tm=ns)
    sheads = (1, ns, 1, n_heads, ATTN_HEAD_DIM)
    sstate = (1, ns, n_groups, n_state)

    return (y_prompt.reshape(b, l, d), y_sample.reshape(ns, 1, d),
            k.reshape(heads), v.reshape(heads),
            h_last[:, :gp].reshape(state), h_last[:, gp:].reshape(state),
            mem_k.reshape(memkv), mem_v.reshape(memkv),
            ks.reshape(sheads), vs.reshape(sheads),
            hrs.reshape(sstate), his.reshape(sstate))
```

```python
import functools
import math

import jax
import jax.numpy as jnp
from jax import lax
from jax.experimental import pallas as pl
from jax.experimental.pallas import tpu as pltpu

RMS_EPS = 1e-6
NEG_INF = -1e30
SSM_GROUP = 16
ATTN_HEAD_DIM = 64
MOBA_BLOCK = 256
MOBA_TOPK = 3
N_XHEADS = 4
LANES = 128
SUBLANES = 8
VMEM_LIMIT = 56 * 1024 * 1024

F32 = jnp.float32
BF16 = jnp.bfloat16
HIGHEST = lax.Precision.HIGHEST


def _params(*sem):
    return pltpu.CompilerParams(dimension_semantics=sem, vmem_limit_bytes=VMEM_LIMIT)


def _rms(x, g):
    ms = jnp.mean(x * x, axis=-1, keepdims=True)
    return x * lax.rsqrt(ms + RMS_EPS) * g


def _dot(a, b):
    return jnp.dot(a, b, preferred_element_type=F32)


def _dot_t(a, b, precision=None):
    return lax.dot_general(a, b, (((1,), (1,)), ((), ())), preferred_element_type=F32,
                           precision=precision)


def _full(shape):
    n = len(shape)
    return pl.BlockSpec(shape, lambda *_: (0,) * n)


def _ffn_body(x_ref, g_ref, w1_ref, w3_ref, w2_ref, gf_ref, o_ref, *, f_chunk, final_norm):
    x = x_ref[...]
    h = _rms(x, g_ref[...]).astype(BF16)
    acc = jnp.zeros(x.shape, F32)
    for c in range(w1_ref.shape[1] // f_chunk):
        cs = slice(c * f_chunk, (c + 1) * f_chunk)
        a = _dot(h, w1_ref[:, cs])
        b = _dot(h, w3_ref[:, cs])
        act = (a * jax.nn.sigmoid(a) * b).astype(BF16)
        acc = acc + _dot(act, w2_ref[cs, :])
    y = x + 0.5 * acc
    if final_norm:
        y = _rms(y, gf_ref[...])
    o_ref[...] = y


def _ffn(x, g, w1, w3, w2, g_final, *, final_norm, tm):
    n, d = x.shape
    f = w1.shape[1]
    f_chunk = 256 if f % 256 == 0 else f
    return pl.pallas_call(
        functools.partial(_ffn_body, f_chunk=f_chunk, final_norm=final_norm),
        grid=(n // tm,),
        in_specs=[pl.BlockSpec((tm, d), lambda i: (i, 0)), _full((1, d)), _full((d, f)), _full((d, f)),
                  _full((f, d)), _full((1, d))],
        out_specs=pl.BlockSpec((tm, d), lambda i: (i, 0)),
        out_shape=jax.ShapeDtypeStruct((n, d), F32),
        compiler_params=_params("parallel"),
        name="ffn_final" if final_norm else "ffn",
    )(x, g, w1, w3, w2, g_final)


def _proj_body(x_ref, g_ref, w_ref, u_ref, q_ref, k_ref, v_ref, *, d_ssm, d_attn):
    h = _rms(x_ref[...], g_ref[...]).astype(BF16)
    p = _dot(h, w_ref[...])
    u_ref[...] = p[:, :d_ssm]
    q_ref[...] = p[:, d_ssm:d_ssm + d_attn]
    k_ref[...] = p[:, d_ssm + d_attn:d_ssm + 2 * d_attn]
    v_ref[...] = p[:, d_ssm + 2 * d_attn:]


def _proj(x, g, w_in, *, d_ssm, tm):
    n, d = x.shape
    d_attn = (w_in.shape[1] - d_ssm) // 3
    row = lambda w: pl.BlockSpec((tm, w), lambda i: (i, 0))
    return pl.pallas_call(
        functools.partial(_proj_body, d_ssm=d_ssm, d_attn=d_attn),
        grid=(n // tm,),
        in_specs=[row(d), _full((1, d)), _full(w_in.shape)],
        out_specs=[row(d_ssm), row(d_attn), row(d_attn), row(d_attn)],
        out_shape=[jax.ShapeDtypeStruct((n, d_ssm), F32)] + [jax.ShapeDtypeStruct((n, d_attn), F32)] * 3,
        compiler_params=_params("parallel"),
        name="mix_proj",
    )(x, g, w_in)


AUX_BIAS, AUX_ONE, AUX_BLOCK, AUX_LANES, MAX_BLOCKS = 0, 3, 16, 64, 32
LOG2E = math.log2(math.e)
BOUND_SLACK = 1.02
MAX_SPREAD = 60.0


def _alibi_slope(h, n_heads):
    return 2.0 ** (-8.0 * (h + 1) / n_heads)


def _bf16_split3(x):
    hi = x.astype(BF16).astype(F32)
    r = x - hi
    mid = r.astype(BF16).astype(F32)
    lo = (r - mid).astype(BF16).astype(F32)
    return hi, mid, lo


def _proj_prompt_body(x_ref, g_ref, w_ref, u_ref, q_ref, kt_ref, vt_ref, kaug_ref, vtb_ref, sums_ref, kn_ref,
                      *, d_ssm, d_attn, n_heads):
    tm = x_ref.shape[1]
    dh = ATTN_HEAD_DIM
    h = _rms(x_ref[0], g_ref[...]).astype(BF16)
    p = _dot(h, w_ref[...])
    u_ref[...] = p[:, :d_ssm]
    q_ref[0] = p[:, d_ssm:d_ssm + d_attn]
    k = p[:, d_ssm + d_attn:d_ssm + 2 * d_attn]
    v = p[:, d_ssm + 2 * d_attn:]
    kt_ref[0] = k.T.reshape(n_heads, dh, tm)
    vt = v.T.reshape(n_heads, dh, tm)
    vt_ref[0] = vt
    for c in range(tm // MOBA_BLOCK):
        cs = slice(c * MOBA_BLOCK, (c + 1) * MOBA_BLOCK)
        vtb_ref[0, :, c] = vt[:, :, cs].astype(BF16)
        sums_ref[0, c:c + 1, :] = jnp.sum(k[cs], axis=0, keepdims=True)
    pos = pl.program_id(1) * tm + lax.broadcasted_iota(jnp.int32, (tm, AUX_LANES), 0)
    lane = lax.broadcasted_iota(jnp.int32, (tm, AUX_LANES), 1)
    block_hot = (lane - AUX_BLOCK == pos // MOBA_BLOCK).astype(F32)
    posf = pos.astype(F32)
    head_of = (lax.broadcasted_iota(jnp.int32, (d_attn, LANES), 0) // dh
               == lax.broadcasted_iota(jnp.int32, (d_attn, LANES), 1))
    kn_ref[0] = jnp.max(_dot((k * k).astype(BF16), head_of.astype(BF16)), axis=0, keepdims=True)
    pieces = []
    for hh in range(n_heads):
        hi, mid, lo = _bf16_split3((_alibi_slope(hh, n_heads) * LOG2E) * posf)
        aux = jnp.where(lane == AUX_BIAS, hi, jnp.where(lane == AUX_BIAS + 1, mid, jnp.where(
            lane == AUX_BIAS + 2, lo, jnp.where(lane < AUX_ONE + 3, 1.0, block_hot))))
        pieces += [k[:, hh * dh:(hh + 1) * dh], aux]
    kaug_ref[0] = jnp.concatenate(pieces, axis=1).astype(BF16)


def _proj_prompt(x, g, w_in, *, d_ssm, tm):
    b, l, d = x.shape
    d_attn = (w_in.shape[1] - d_ssm) // 3
    dh = ATTN_HEAD_DIM
    n_heads = d_attn // dh
    nb = l // MOBA_BLOCK
    tb = tm // MOBA_BLOCK
    assert nb <= MAX_BLOCKS and tm % MOBA_BLOCK == 0
    row = lambda w: pl.BlockSpec((1, tm, w), lambda i, j: (i, j, 0))
    tr = pl.BlockSpec((1, n_heads, dh, tm), lambda i, j: (i, 0, 0, j))
    return pl.pallas_call(
        functools.partial(_proj_prompt_body, d_ssm=d_ssm, d_attn=d_attn, n_heads=n_heads),
        grid=(b, l // tm),
        in_specs=[row(d), _full((1, d)), _full(w_in.shape)],
        out_specs=[pl.BlockSpec((tm, d_ssm), lambda i, j: (j, i)),
                   row(d_attn), tr, tr, row(n_heads * (dh + AUX_LANES)),
                   pl.BlockSpec((1, n_heads, tb, dh, MOBA_BLOCK), lambda i, j: (i, 0, j, 0, 0)),
                   pl.BlockSpec((1, tb, d_attn), lambda i, j: (i * (l // tm) + j, 0, 0)),
                   pl.BlockSpec((1, 1, LANES), lambda i, j: (i * (l // tm) + j, 0, 0))],
        out_shape=[jax.ShapeDtypeStruct((l, b * d_ssm), F32), jax.ShapeDtypeStruct((b, l, d_attn), F32),
                   jax.ShapeDtypeStruct((b, n_heads, dh, l), F32), jax.ShapeDtypeStruct((b, n_heads, dh, l), F32),
                   jax.ShapeDtypeStruct((b, l, n_heads * (dh + AUX_LANES)), BF16),
                   jax.ShapeDtypeStruct((b, n_heads, nb, dh, MOBA_BLOCK), BF16),
                   jax.ShapeDtypeStruct((b * l // tm, tb, d_attn), F32),
                   jax.ShapeDtypeStruct((b * l // tm, 1, LANES), F32)],
        compiler_params=_params("parallel", "parallel"),
        name="mix_proj_prompt",
    )(x, g, w_in)


def _s5_discretise(a_re, a_im, log_dt, b_re, b_im):
    dt = jnp.exp(log_dt)[:, None]
    mag = jnp.exp(a_re * dt)
    abar_re = mag * jnp.cos(a_im * dt)
    abar_im = mag * jnp.sin(a_im * dt)
    den = a_re * a_re + a_im * a_im
    nr = abar_re - 1.0
    f_re = (nr * a_re + abar_im * a_im) / den
    f_im = (abar_im * a_re - nr * a_im) / den
    bbar_re = f_re[..., None] * b_re - f_im[..., None] * b_im
    bbar_im = f_re[..., None] * b_im + f_im[..., None] * b_re
    return abar_re, abar_im, bbar_re, bbar_im


def _s5_matrices(a_re, a_im, log_dt, b_re, b_im, c_re, c_im):
    g, p, c = b_re.shape
    abar_re, abar_im, bbar_re, bbar_im = _s5_discretise(a_re, a_im, log_dt, b_re, b_im)
    eye = jnp.eye(g, dtype=F32)
    bm_re = jnp.einsum('gpc,gh->gchp', bbar_re, eye).reshape(g * c, g * p)
    bm_im = jnp.einsum('gpc,gh->gchp', bbar_im, eye).reshape(g * c, g * p)
    bmat = jnp.concatenate([bm_re, bm_im], axis=1).astype(BF16)
    cm_re = jnp.einsum('gcp,gh->gphc', c_re, eye).reshape(g * p, g * c)
    cm_im = jnp.einsum('gcp,gh->gphc', c_im, eye).reshape(g * p, g * c)
    cmat = jnp.concatenate([cm_re, -cm_im], axis=0).astype(BF16)
    return abar_re.reshape(1, g * p), abar_im.reshape(1, g * p), bmat, cmat


S5_IN_TILES = 8
S5_OUT_TILES = 2


def _s5_compact(bmat, cmat):
    d_ssm, gp2 = bmat.shape
    gp = gp2 // 2
    ci, si = d_ssm // S5_IN_TILES, gp // S5_IN_TILES
    bc = jnp.stack([jnp.concatenate([bmat[c * ci:(c + 1) * ci, c * si:(c + 1) * si],
                                     bmat[c * ci:(c + 1) * ci, gp + c * si:gp + (c + 1) * si]], axis=1)
                    for c in range(S5_IN_TILES)])
    co, so = d_ssm // S5_OUT_TILES, gp // S5_OUT_TILES
    cc = jnp.stack([jnp.stack([cmat[part * gp + o * so:part * gp + (o + 1) * so, o * co:(o + 1) * co]
                               for part in range(2)]) for o in range(S5_OUT_TILES)])
    return bc, cc


def _s5_output(ch, u, d_ref, wg_ref, bg_ref):
    y = ch + d_ref[...] * u
    z = jax.nn.gelu(y, approximate=True)
    return z * jax.nn.sigmoid(_dot(z.astype(BF16), wg_ref[...]) + bg_ref[...])


def _s5_prompt_body(u_ref, ar_ref, ai_ref, bc_ref, cc_ref, d_ref, wg_ref, bg_ref,
                    y_ref, hlast_ref, hbuf, carry, tbuf, *, nbatch):
    gp = ar_ref.shape[1]
    steps = u_ref.shape[0]
    rows = steps * nbatch
    d_ssm = u_ref.shape[1] // nbatch
    tiles = d_ssm // LANES

    @pl.when(pl.program_id(0) == 0)
    def _():
        carry[...] = jnp.zeros(carry.shape, F32)

    for b in range(nbatch):
        for k in range(tiles):
            tbuf[k, pl.ds(b, steps, stride=nbatch), :] = u_ref[:, b * d_ssm + k * LANES:b * d_ssm + (k + 1) * LANES]
    u = jnp.concatenate([tbuf[k] for k in range(tiles)], axis=1)
    ub = u.astype(BF16)
    ci, si = u.shape[1] // S5_IN_TILES, gp // S5_IN_TILES
    for c in range(S5_IN_TILES):
        bu = _dot(ub[:, c * ci:(c + 1) * ci], bc_ref[c])
        hbuf[:, c * si:(c + 1) * si] = bu[:, :si]
        hbuf[:, gp + c * si:gp + (c + 1) * si] = bu[:, si:]

    half = gp // 2
    lower = lax.broadcasted_iota(jnp.int32, (SUBLANES, half), 0) < nbatch
    halves = lambda v: jnp.where(lower, jnp.broadcast_to(v[:, :half], (SUBLANES, half)),
                                 jnp.broadcast_to(v[:, half:], (SUBLANES, half)))
    ar, ai = halves(ar_ref[...]), halves(ai_ref[...])
    swap = lambda v: pltpu.roll(v, nbatch, 0)

    def step(i, st):
        hr, hi = st
        rows8 = pl.ds(pl.multiple_of(i * SUBLANES, SUBLANES), SUBLANES)
        new = []
        for part in range(2):
            a = hbuf[rows8, part * gp:part * gp + half]
            b = hbuf[rows8, part * gp + half:(part + 1) * gp]
            new.append((jnp.where(lower, a, swap(b)), jnp.where(lower, swap(a), b)))
        (x0r, x1r), (x0i, x1i) = new
        n0r = ar * hr - ai * hi + x0r
        n0i = ar * hi + ai * hr + x0i
        n1r = ar * n0r - ai * n0i + x1r
        n1i = ar * n0i + ai * n0r + x1i
        for part, (n0, n1) in enumerate(((n0r, n1r), (n0i, n1i))):
            hbuf[rows8, part * gp:part * gp + half] = jnp.where(lower, n0, swap(n1))
            hbuf[rows8, part * gp + half:(part + 1) * gp] = jnp.where(lower, swap(n0), n1)
        return n1r, n1i

    hr, hi = lax.fori_loop(0, rows // SUBLANES, step, (carry[:, :half], carry[:, half:]))
    carry[:, :half] = hr
    carry[:, half:] = hi
    hlast_ref[:, :half] = hr
    hlast_ref[:, half:] = hi
    so = gp // S5_OUT_TILES
    ch = jnp.concatenate(
        [_dot(hbuf[:, o * so:(o + 1) * so].astype(BF16), cc_ref[o, 0])
         + _dot(hbuf[:, gp + o * so:gp + (o + 1) * so].astype(BF16), cc_ref[o, 1]) for o in range(S5_OUT_TILES)],
        axis=1)
    y = _s5_output(ch, u, d_ref, wg_ref, bg_ref)
    for k in range(tiles):
        tbuf[k] = y[:, k * LANES:(k + 1) * LANES]
    for b in range(nbatch):
        for k in range(tiles):
            y_ref[:, b * d_ssm + k * LANES:b * d_ssm + (k + 1) * LANES] = tbuf[k, pl.ds(b, steps, stride=nbatch), :]


def _s5_prompt(u_tb, nbatch, ar, ai, bmat, cmat, d, w_glu, b_glu, *, t_chunk):
    l, width = u_tb.shape
    d_ssm = width // nbatch
    gp = ar.shape[1]
    bmat, cmat = _s5_compact(bmat, cmat)
    rows = t_chunk * nbatch
    assert 2 * nbatch == SUBLANES and l % t_chunk == 0 and d_ssm % LANES == 0
    return pl.pallas_call(
        functools.partial(_s5_prompt_body, nbatch=nbatch),
        grid=(l // t_chunk,),
        in_specs=[pl.BlockSpec((t_chunk, width), lambda i: (i, 0)), _full((1, gp)), _full((1, gp)),
                  _full(bmat.shape), _full(cmat.shape), _full((1, d_ssm)), _full(w_glu.shape),
                  _full((1, d_ssm))],
        out_specs=[pl.BlockSpec((t_chunk, width), lambda i: (i, 0)), _full((SUBLANES, gp))],
        out_shape=[jax.ShapeDtypeStruct((l, width), F32),
                   jax.ShapeDtypeStruct((SUBLANES, gp), F32)],
        scratch_shapes=[pltpu.VMEM((rows, 2 * gp), F32), pltpu.VMEM((SUBLANES, gp), F32),
                        pltpu.VMEM((d_ssm // LANES, rows, LANES), F32)],
        compiler_params=_params("arbitrary"),
        name="s5_prompt",
    )(u_tb, ar, ai, bmat, cmat, d, w_glu, b_glu)


def _s5_sample_body(u_ref, h0r_ref, h0i_ref, ar_ref, ai_ref, bmat_ref, cmat_ref, d_ref, wg_ref, bg_ref,
                    y_ref, hr_ref, hi_ref):
    gp = ar_ref.shape[1]
    u = u_ref[...]
    bu = _dot(u.astype(BF16), bmat_ref[...])
    ar, ai = ar_ref[...], ai_ref[...]
    h0r, h0i = h0r_ref[...], h0i_ref[...]
    hr = bu[:, :gp] + ar * h0r - ai * h0i
    hi = bu[:, gp:] + ar * h0i + ai * h0r
    hr_ref[...] = hr
    hi_ref[...] = hi
    h_bf = jnp.concatenate([hr, hi], axis=1).astype(BF16)
    y_ref[...] = _s5_output(_dot(h_bf, cmat_ref[...]), u, d_ref, wg_ref, bg_ref)


def _s5_sample(u, h0r, h0i, ar, ai, bmat, cmat, d, w_glu, b_glu):
    n, d_ssm = u.shape
    gp = ar.shape[1]
    args = (u, h0r, h0i, ar, ai, bmat, cmat, d, w_glu, b_glu)
    return pl.pallas_call(
        _s5_sample_body,
        grid=(1,),
        in_specs=[_full(a.shape) for a in args],
        out_specs=[_full((n, d_ssm)), _full((n, gp)), _full((n, gp))],
        out_shape=[jax.ShapeDtypeStruct((n, d_ssm), F32), jax.ShapeDtypeStruct((n, gp), F32),
                   jax.ShapeDtypeStruct((n, gp), F32)],
        compiler_params=_params("arbitrary"),
        name="s5_sample",
    )(*args)


def _top_blocks(gate, brow, n_valid):
    bf = brow.astype(F32)
    g = jnp.where(brow < n_valid, gate, NEG_INF)
    sel = jnp.zeros(gate.shape, jnp.bool_)
    for _ in range(MOBA_TOPK):
        mx = jnp.max(g, axis=0, keepdims=True)
        first = jnp.min(jnp.where(g == mx, bf, float(gate.shape[0])), axis=0, keepdims=True)
        pick = bf == first
        sel = jnp.logical_or(sel, pick)
        g = jnp.where(pick, -jnp.inf, g)
    return jnp.logical_and(sel, brow < n_valid)


def _moba_prompt_body(q_ref, kaug_ref, vtb_ref, sums_ref, kn_ref, o_ref, qaug, acc, s_scr, p_scr, *, n_heads):
    j = pl.program_id(1)
    blk = MOBA_BLOCK
    dh = ATTN_HEAD_DIM
    ka = dh + AUX_LANES
    nb = sums_ref.shape[1]
    t0 = pl.multiple_of(j * blk, blk)
    scale = dh ** -0.5
    qt = q_ref[0].T
    brow = lax.broadcasted_iota(jnp.int32, (MAX_BLOCKS, blk), 0)
    crow = lax.broadcasted_iota(jnp.int32, (AUX_BLOCK, blk), 0)
    t0f = jnp.broadcast_to(t0.astype(F32), (AUX_BLOCK, blk))
    kmax2 = jnp.max(kn_ref[0], axis=0, keepdims=True)
    hlane = lax.broadcasted_iota(jnp.int32, kmax2.shape, 1)
    tq = lax.broadcasted_iota(jnp.int32, (1, blk), 1).astype(F32)
    shifts, spreads = [], []
    for h in range(n_heads):
        slope = _alibi_slope(h, n_heads)
        qth = qt[h * dh:(h + 1) * dh]
        means = sums_ref[0, :, h * dh:(h + 1) * dh] * (1.0 / blk)
        gate = jnp.dot(means, qth, precision=HIGHEST, preferred_element_type=F32)
        if nb < MAX_BLOCKS:
            gate = jnp.concatenate([gate, jnp.full((MAX_BLOCKS - nb, blk), NEG_INF, F32)], axis=0)
        keep = jnp.logical_or(_top_blocks(gate, brow, j), brow == j)
        mask = jnp.where(keep, 0.0, NEG_INF)
        hi, mid, lo = _bf16_split3((-slope * LOG2E) * t0f)
        const = jnp.where(crow < AUX_ONE, 1.0, jnp.where(crow == AUX_ONE, hi, jnp.where(
            crow == AUX_ONE + 1, mid, jnp.where(crow == AUX_ONE + 2, lo, 0.0))))
        rest = jnp.zeros((AUX_LANES - AUX_BLOCK - MAX_BLOCKS, blk), F32)
        qaug[h] = jnp.concatenate([qth * (scale * LOG2E), const, mask, rest], axis=0).astype(BF16)
        k2 = jnp.max(jnp.where(hlane == h, kmax2, 0.0), axis=1, keepdims=True)
        bound = jnp.sqrt(jnp.sum(qth * qth, axis=0, keepdims=True) * k2) * (scale * BOUND_SLACK)
        shifts.append((bound + slope * tq) * LOG2E)
        spreads.append(bound)
    shift = jnp.concatenate(shifts, axis=0)
    safe = 2.0 * jnp.max(jnp.concatenate(spreads, axis=0)) < MAX_SPREAD

    def scores(n, h):
        r = pl.multiple_of(n * blk, blk)
        return _dot(kaug_ref[0, pl.ds(r, blk), h * ka:(h + 1) * ka], qaug[h])

    causal = (lax.broadcasted_iota(jnp.int32, (blk, blk), 0) <= lax.broadcasted_iota(jnp.int32, (blk, blk), 1))

    def finish(l_all):
        out_t = jnp.concatenate([acc[h * dh:(h + 1) * dh, :] / l_all[h:h + 1] for h in range(n_heads)], axis=0)
        o_ref[0] = out_t.T

    @pl.when(safe)
    def _():
        def tile(n, l_all, own):
            ls = []
            for h in range(n_heads):
                s = scores(n, h)
                if own:
                    s = jnp.where(causal, s, NEG_INF)
                p = jnp.exp2(s - shift[h:h + 1])
                p_scr[h] = p.astype(BF16)
                ls.append(jnp.sum(p, axis=0, keepdims=True))
            for h in range(n_heads):
                hs = slice(h * dh, (h + 1) * dh)
                pv = _dot(vtb_ref[0, h, n], p_scr[h])
                acc[hs, :] = pv if own else acc[hs, :] + pv
            l_new = jnp.concatenate(ls, axis=0)
            return l_new if own else l_all + l_new

        finish(lax.fori_loop(0, j, lambda n, l_all: tile(n, l_all, False), tile(j, None, True)))

    @pl.when(jnp.logical_not(safe))
    def _():
        def tile(n, st, own):
            ms, ls, alphas = [], [], []
            for h in range(n_heads):
                s_scr[h] = scores(n, h)
            for h in range(n_heads):
                s = s_scr[h]
                if own:
                    s = jnp.where(causal, s, NEG_INF)
                    m_new = jnp.max(s, axis=0, keepdims=True)
                else:
                    m_new = jnp.maximum(st[0][h:h + 1], jnp.max(s, axis=0, keepdims=True))
                    alphas.append(jnp.exp2(st[0][h:h + 1] - m_new))
                p = jnp.exp2(s - m_new)
                p_scr[h] = p.astype(BF16)
                ms.append(m_new)
                l_new = jnp.sum(p, axis=0, keepdims=True)
                ls.append(l_new if own else alphas[h] * st[1][h:h + 1] + l_new)
            for h in range(n_heads):
                hs = slice(h * dh, (h + 1) * dh)
                pv = _dot(vtb_ref[0, h, n], p_scr[h])
                acc[hs, :] = pv if own else alphas[h] * acc[hs, :] + pv
            return jnp.concatenate(ms, axis=0), jnp.concatenate(ls, axis=0)

        finish(lax.fori_loop(0, j, lambda n, st: tile(n, st, False), tile(j, None, True))[1])


def _moba_prompt(q, kaug, vtb, sums, knorm):
    b, l, d = q.shape
    nb = l // MOBA_BLOCK
    n_heads = d // ATTN_HEAD_DIM
    ka = ATTN_HEAD_DIM + AUX_LANES
    once = pl.Buffered(1)
    return pl.pallas_call(
        functools.partial(_moba_prompt_body, n_heads=n_heads),
        grid=(b, nb),
        in_specs=[pl.BlockSpec((1, MOBA_BLOCK, d), lambda i, j: (i, j, 0)),
                  pl.BlockSpec((1, l, n_heads * ka), lambda i, j: (i, 0, 0), pipeline_mode=once),
                  pl.BlockSpec((1, n_heads, nb, ATTN_HEAD_DIM, MOBA_BLOCK), lambda i, j: (i, 0, 0, 0, 0),
                               pipeline_mode=once),
                  pl.BlockSpec((1, nb, d), lambda i, j: (i, 0, 0)),
                  pl.BlockSpec((1,) + knorm.shape[1:], lambda i, j: (i, 0, 0))],
        out_specs=pl.BlockSpec((1, MOBA_BLOCK, d), lambda i, j: (i, j, 0)),
        out_shape=jax.ShapeDtypeStruct((b, l, d), F32),
        scratch_shapes=[pltpu.VMEM((n_heads, ka, MOBA_BLOCK), BF16), pltpu.VMEM((d, MOBA_BLOCK), F32),
                        pltpu.VMEM((n_heads, MOBA_BLOCK, MOBA_BLOCK), F32),
                        pltpu.VMEM((n_heads, MOBA_BLOCK, MOBA_BLOCK), BF16)],
        compiler_params=_params("parallel", "arbitrary"),
        name="moba_prompt",
    )(q, kaug, vtb, sums, knorm)


def _moba_kpass_body(pt_ref, q_ref, *refs, pages, n_heads):
    del pt_ref
    k_refs, (lg_ref, ps_ref) = refs[:pages], refs[pages:]
    s = pl.program_id(1)
    dh, page = k_refs[0].shape[3:]

    @pl.when(s == 0)
    def _():
        ps_ref[...] = jnp.zeros(ps_ref.shape, F32)

    qcols = q_ref[0]
    qb = [jnp.broadcast_to(qcols[:, h:h + 1], (dh, page)) for h in range(n_heads)]
    plane = lax.broadcasted_iota(jnp.int32, (n_heads, ps_ref.shape[2]), 1)
    psum = ps_ref[0]
    for i in range(pages):
        lg = jnp.concatenate([jnp.sum(k_refs[i][0, 0, h] * qb[h], axis=0, keepdims=True)
                              for h in range(n_heads)], axis=0)
        lg_ref[0, :, i * page:(i + 1) * page] = lg
        psum = jnp.where(plane == s * pages + i, jnp.sum(lg, axis=1, keepdims=True), psum)
    ps_ref[0] = psum


def _moba_kpass(q_t, cache_kt, page_table, *, pages):
    n, dh, n_heads = q_t.shape
    n_pages = page_table.shape[1]
    page = cache_kt.shape[4]
    k_spec = lambda i: pl.BlockSpec((1, 1, n_heads, dh, page),
                                    lambda b, s, pt: (0, pt[b, s * pages + i], 0, 0, 0))
    grid_spec = pltpu.PrefetchScalarGridSpec(
        num_scalar_prefetch=1,
        grid=(n, n_pages // pages),
        in_specs=[pl.BlockSpec((1, dh, n_heads), lambda b, s, pt: (b, 0, 0))]
                 + [k_spec(i) for i in range(pages)],
        out_specs=[pl.BlockSpec((1, n_heads, pages * page), lambda b, s, pt: (b, 0, s)),
                   pl.BlockSpec((1, n_heads, n_pages), lambda b, s, pt: (b, 0, 0))])
    return pl.pallas_call(
        functools.partial(_moba_kpass_body, pages=pages, n_heads=n_heads),
        grid_spec=grid_spec,
        out_shape=[jax.ShapeDtypeStruct((n, n_heads, n_pages * page), F32),
                   jax.ShapeDtypeStruct((n, n_heads, n_pages), F32)],
        compiler_params=_params("parallel", "arbitrary"),
        name="moba_sample_kpass",
    )(page_table, q_t, *([cache_kt] * pages))


def _moba_select_body(ps_ref, idx_ref, *, pages_per_block, n_blocks):
    ps = ps_ref[...]
    lanes = ps.shape[1]
    lane = lax.broadcasted_iota(jnp.int32, (1, lanes), 1)
    g = ps
    for i in range(1, pages_per_block):
        g = g + pltpu.roll(ps, lanes - i, 1)
    g = g * (1.0 / MOBA_BLOCK)
    is_block = jnp.logical_and(lane % pages_per_block == 0, lane < n_blocks * pages_per_block)
    g = jnp.where(is_block, g, -jnp.inf)
    lanef = lane.astype(F32)
    out = jnp.zeros(idx_ref.shape, F32)
    olane = lax.broadcasted_iota(jnp.int32, idx_ref.shape, 1)
    for r in range(MOBA_TOPK):
        mx = jnp.max(g, axis=1, keepdims=True)
        first = jnp.min(jnp.where(g == mx, lanef, float(lanes)), axis=1, keepdims=True)
        out = jnp.where(olane == r, first * (1.0 / pages_per_block), out)
        g = jnp.where(lanef == first, -jnp.inf, g)
    idx_ref[...] = out.astype(jnp.int32)


def _moba_select(page_sums2, *, pages_per_block, n_blocks):
    rows, lanes = page_sums2.shape
    return pl.pallas_call(
        functools.partial(_moba_select_body, pages_per_block=pages_per_block, n_blocks=n_blocks),
        grid=(1,),
        in_specs=[_full((rows, lanes))],
        out_specs=_full((rows, LANES)),
        out_shape=jax.ShapeDtypeStruct((rows, LANES), jnp.int32),
        compiler_params=_params("arbitrary"),
        name="moba_sample_select",
    )(page_sums2)


def _moba_attend_body(blk_ref, pg_ref, q_ref, kn_ref, vn_ref, *refs, pages_per_block, n_heads, past_len):
    del pg_ref
    n_lg = MOBA_TOPK
    n_v = MOBA_TOPK * pages_per_block
    lg_refs, v_refs, o_ref = refs[:n_heads * n_lg], refs[n_heads * n_lg:n_heads * (n_lg + n_v)], refs[-1]
    b = pl.program_id(0)
    dh = ATTN_HEAD_DIM
    blk = MOBA_BLOCK
    page = blk // pages_per_block
    scale = dh ** -0.5
    lane = lax.broadcasted_iota(jnp.int32, (1, blk), 1)
    qk = q_ref[0] * kn_ref[0]
    outs = []
    for h in range(n_heads):
        hs = slice(h * dh, (h + 1) * dh)
        slope = _alibi_slope(h, n_heads)
        s_own = jnp.sum(qk[:, hs], axis=1, keepdims=True) * scale
        ss = []
        for i in range(n_lg):
            pos = blk_ref[b, h, i] * blk + lane
            ss.append(lg_refs[h * n_lg + i][0] * scale - slope * (past_len - pos).astype(F32))
        m = s_own
        for s in ss:
            m = jnp.maximum(m, jnp.max(s, axis=1, keepdims=True))
        p_own = jnp.exp(s_own - m)
        l = p_own
        acc = p_own * vn_ref[0][:, hs]
        for i in range(n_lg):
            p = jnp.exp(ss[i] - m)
            l = l + jnp.sum(p, axis=1, keepdims=True)
            for c in range(pages_per_block):
                vt = v_refs[h * n_v + i * pages_per_block + c][0, 0, 0].astype(BF16)
                acc = acc + _dot_t(p[:, c * page:(c + 1) * page].astype(BF16), vt)
        outs.append(acc / l)
    o_ref[0] = jnp.concatenate(outs, axis=1)


def _moba_attend(blocks, pages_idx, q3, kn3, vn3, logits, cache_vt, *, pages_per_block, past_len):
    n, _, d = q3.shape
    _, _, n_heads, dh, page = cache_vt.shape
    n_blocks = logits.shape[0] // (n * n_heads)
    row = pl.BlockSpec((1, 1, d), lambda b, bl, pg: (b, 0, 0))
    lg_spec = lambda h, i: pl.BlockSpec(
        (1, 1, MOBA_BLOCK), lambda b, bl, pg: ((b * n_heads + h) * n_blocks + bl[b, h, i], 0, 0))
    v_spec = lambda h, i: pl.BlockSpec(
        (1, 1, 1, dh, page), lambda b, bl, pg: (0, pg[b, h, i], h, 0, 0))
    n_v = MOBA_TOPK * pages_per_block
    lg_specs = [lg_spec(h, i) for h in range(n_heads) for i in range(MOBA_TOPK)]
    v_specs = [v_spec(h, i) for h in range(n_heads) for i in range(n_v)]
    grid_spec = pltpu.PrefetchScalarGridSpec(
        num_scalar_prefetch=2,
        grid=(n,),
        in_specs=[row, row, row] + lg_specs + v_specs,
        out_specs=row)
    return pl.pallas_call(
        functools.partial(_moba_attend_body, pages_per_block=pages_per_block, n_heads=n_heads,
                          past_len=past_len),
        grid_spec=grid_spec,
        out_shape=jax.ShapeDtypeStruct((n, 1, d), F32),
        compiler_params=_params("parallel"),
        name="moba_sample_attend",
    )(blocks, pages_idx, q3, kn3, vn3, *([logits] * len(lg_specs)), *([cache_vt] * len(v_specs)))


def _moba_sample(q, k_new, v_new, cache_k, cache_v, page_table):
    n, d = q.shape
    _, n_pool, page, n_heads, dh = cache_k.shape
    n_pages = page_table.shape[1]
    past_len = n_pages * page
    pages_per_block = MOBA_BLOCK // page
    n_blocks = n_pages // pages_per_block
    assert n_blocks >= MOBA_TOPK and n_pages % LANES == 0
    q3 = q.reshape(n, 1, d)
    cache_kt = cache_k.transpose(0, 1, 3, 4, 2)
    cache_vt = cache_v.transpose(0, 1, 3, 4, 2)
    kpass_pages = 32 if n_pages % 32 == 0 else 1
    logits, page_sums = _moba_kpass(q.reshape(n, n_heads, dh).swapaxes(1, 2), cache_kt, page_table,
                                    pages=kpass_pages)
    blocks = _moba_select(page_sums.reshape(n * n_heads, n_pages), pages_per_block=pages_per_block,
                          n_blocks=n_blocks)[:, :MOBA_TOPK].reshape(n, n_heads, MOBA_TOPK)
    logical = (blocks[..., None] * pages_per_block + jnp.arange(pages_per_block)).reshape(n, n_heads, -1)
    pages_idx = jnp.take_along_axis(page_table[:, None, :], logical, axis=2)
    y = _moba_attend(blocks, pages_idx, q3, k_new.reshape(n, 1, d), v_new.reshape(n, 1, d),
                     logits.reshape(n * n_heads * n_blocks, 1, MOBA_BLOCK), cache_vt,
                     pages_per_block=pages_per_block, past_len=past_len)
    return y.reshape(n, d)


def _memkv_body(m_ref, g_ref, wk_ref, wv_ref, k_ref, v_ref):
    h = _rms(m_ref[...], g_ref[...]).astype(BF16)
    k_ref[...] = _dot(h, wk_ref[...])
    v_ref[...] = _dot(h, wv_ref[...])


def _memkv(mem, g, wk, wv, *, tm):
    n, d = mem.shape
    row = pl.BlockSpec((tm, d), lambda i: (i, 0))
    return pl.pallas_call(
        _memkv_body,
        grid=(n // tm,),
        in_specs=[row, _full((1, d)), _full(wk.shape), _full(wv.shape)],
        out_specs=[row, row],
        out_shape=[jax.ShapeDtypeStruct((n, d), F32)] * 2,
        compiler_params=_params("parallel"),
        name="mem_kv",
    )(mem, g, wk, wv)


def _mix_out(x, ys, ya, gs, ga, w_out_ref, d_ssm):
    ysn = _rms(ys, gs).astype(BF16)
    yan = _rms(ya, ga).astype(BF16)
    return x + _dot(ysn, w_out_ref[:d_ssm, :]) + _dot(yan, w_out_ref[d_ssm:, :])


def _xattn_heads(q, mk_ref, mv_ref):
    d = q.shape[1]
    xd = d // N_XHEADS
    outs = []
    for h in range(N_XHEADS):
        hs = slice(h * xd, (h + 1) * xd)
        s = _dot_t(q[:, hs].astype(BF16), mk_ref[0, :, hs].astype(BF16)) * (xd ** -0.5)
        m = jnp.max(s, axis=1, keepdims=True)
        p = jnp.exp(s - m)
        l = jnp.sum(p, axis=1, keepdims=True)
        outs.append(_dot(p.astype(BF16), mv_ref[0, :, hs].astype(BF16)) / l)
    return jnp.concatenate(outs, axis=1)


def _merge_prompt_body(x_ref, ys_ref, ya_ref, gs_ref, ga_ref, wout_ref, gx_ref, wq_ref, mk_ref, mv_ref,
                       wo_ref, o_ref, *, d_ssm):
    x2 = _mix_out(x_ref[0], ys_ref[...], ya_ref[0], gs_ref[...], ga_ref[...], wout_ref, d_ssm)
    q = _dot(_rms(x2, gx_ref[...]).astype(BF16), wq_ref[...])
    o = _xattn_heads(q, mk_ref, mv_ref)
    o_ref[0] = x2 + _dot(o.astype(BF16), wo_ref[...])


def _merge_prompt(x, ys, ya, gs, ga, w_out, gx, wq, mk, mv, wo, *, tm):
    b, l, d = x.shape
    d_ssm = ys.shape[1] // b
    d_attn = ya.shape[2]
    n_mem = mk.shape[1]
    row = lambda w: pl.BlockSpec((1, tm, w), lambda i, j: (i, j, 0))
    mem = pl.BlockSpec((1, n_mem, d), lambda i, j: (i, 0, 0))
    return pl.pallas_call(
        functools.partial(_merge_prompt_body, d_ssm=d_ssm),
        grid=(b, l // tm),
        in_specs=[row(d), pl.BlockSpec((tm, d_ssm), lambda i, j: (j, i)), row(d_attn), _full((1, d_ssm)),
                  _full((1, d_attn)), _full(w_out.shape), _full((1, d)), _full(wq.shape), mem, mem,
                  _full(wo.shape)],
        out_specs=row(d),
        out_shape=jax.ShapeDtypeStruct((b, l, d), F32),
        compiler_params=_params("parallel", "parallel"),
        name="merge_prompt",
    )(x, ys, ya, gs, ga, w_out, gx, wq, mk, mv, wo)


def _merge_pre_body(x_ref, ys_ref, ya_ref, gs_ref, ga_ref, wout_ref, gx_ref, wq_ref, x2_ref, q_ref, *, d_ssm):
    x2 = _mix_out(x_ref[...], ys_ref[...], ya_ref[...], gs_ref[...], ga_ref[...], wout_ref, d_ssm)
    x2_ref[...] = x2
    q_ref[...] = _dot(_rms(x2, gx_ref[...]).astype(BF16), wq_ref[...])


def _merge_pre(x, ys, ya, gs, ga, w_out, gx, wq):
    n, d = x.shape
    args = (x, ys, ya, gs, ga, w_out, gx, wq)
    return pl.pallas_call(
        functools.partial(_merge_pre_body, d_ssm=ys.shape[1]),
        grid=(1,),
        in_specs=[_full(a.shape) for a in args],
        out_specs=[_full((n, d)), _full((n, d))],
        out_shape=[jax.ShapeDtypeStruct((n, d), F32)] * 2,
        compiler_params=_params("arbitrary"),
        name="merge_sample_pre",
    )(*args)


def _xattn_sample_body(q_ref, mk_ref, mv_ref, o_ref):
    o_ref[0] = _xattn_heads(q_ref[0], mk_ref, mv_ref)


def _xattn_sample(q3, mk, mv):
    n, _, d = q3.shape
    n_mem = mk.shape[1]
    row = pl.BlockSpec((1, 1, d), lambda i: (i, 0, 0))
    mem = pl.BlockSpec((1, n_mem, d), lambda i: (i, 0, 0))
    return pl.pallas_call(
        _xattn_sample_body,
        grid=(n,),
        in_specs=[row, mem, mem],
        out_specs=row,
        out_shape=jax.ShapeDtypeStruct((n, 1, d), F32),
        compiler_params=_params("parallel"),
        name="xattn_sample",
    )(q3, mk, mv)


def _merge_post_body(x_ref, o_ref, wo_ref, y_ref):
    y_ref[...] = x_ref[...] + _dot(o_ref[...].astype(BF16), wo_ref[...])


def _merge_post(x2, o, wo):
    n, d = x2.shape
    return pl.pallas_call(
        _merge_post_body,
        grid=(1,),
        in_specs=[_full((n, d)), _full((n, d)), _full(wo.shape)],
        out_specs=_full((n, d)),
        out_shape=jax.ShapeDtypeStruct((n, d), F32),
        compiler_params=_params("arbitrary"),
        name="merge_sample_post",
    )(x2, o, wo)


def kernel(x_prompt, x_sample, mem_prompt, cache_k, cache_v, page_table, state_ssm_re, state_ssm_im, cache_mem_k, cache_mem_v, g_ffn1, w1_ffn1, w3_ffn1, w2_ffn1, g_mix, w_in, ssm_a_re, ssm_a_im, ssm_log_dt, ssm_b_re, ssm_b_im, ssm_c_re, ssm_c_im, ssm_d, w_glu, b_glu, g_out_ssm, g_out_attn, w_out, g_xattn, g_mem, wq_x, wk_x, wv_x, wo_x, g_ffn2, w1_ffn2, w3_ffn2, w2_ffn2, g_final):
    depth = g_ffn1.shape[0]
    assert depth == 1
    b, l, d = x_prompt.shape
    ns, ls, _ = x_sample.shape
    assert ls == 1
    n_groups, n_state = ssm_a_re.shape[1:]
    gp = n_groups * n_state
    d_ssm = n_groups * SSM_GROUP
    d_attn = (w_in.shape[2] - d_ssm) // 3
    n_heads = d_attn // ATTN_HEAD_DIM
    n_mem = mem_prompt.shape[1]
    xd = d // N_XHEADS

    vec = lambda a: a[0].reshape(1, -1)
    wb = lambda a: a[0].astype(BF16)
    w1a, w3a, w2a = wb(w1_ffn1), wb(w3_ffn1), wb(w2_ffn1)
    w1b, w3b, w2b = wb(w1_ffn2), wb(w3_ffn2), wb(w2_ffn2)
    w_in_b, w_out_b, w_glu_b = wb(w_in), wb(w_out), wb(w_glu)
    wq_b, wk_b, wv_b, wo_b = wb(wq_x), wb(wk_x), wb(wv_x), wb(wo_x)
    gfin = g_final.reshape(1, -1)
    ar, ai, bmat, cmat = _s5_matrices(ssm_a_re[0], ssm_a_im[0], ssm_log_dt[0], ssm_b_re[0], ssm_b_im[0],
                                      ssm_c_re[0], ssm_c_im[0])
    s5_w = (ar, ai, bmat, cmat, vec(ssm_d), w_glu_b, vec(b_glu))

    tm = 512 if l % 512 == 0 else l
    xp = x_prompt.reshape(b * l, d)
    mem_k, mem_v = _memkv(mem_prompt.reshape(b * n_mem, d), vec(g_mem), wk_b, wv_b,
                          tm=min(256, b * n_mem))
    x1 = _ffn(xp, vec(g_ffn1), w1a, w3a, w2a, gfin, final_norm=False, tm=tm)
    x1 = x1.reshape(b, l, d)
    u, q, k_t, v_t, kaug, vtb, sums, knorm = _proj_prompt(x1, vec(g_mix), w_in_b, d_ssm=d_ssm, tm=tm)
    t_chunk = 128 if l % 128 == 0 else l
    ys, h_last = _s5_prompt(u, b, *s5_w, t_chunk=t_chunk)
    ya = _moba_prompt(q, kaug, vtb, sums.reshape(b, l // MOBA_BLOCK, d_attn), knorm.reshape(b, l // tm, LANES))
    x3 = _merge_prompt(x1, ys, ya, vec(g_out_ssm), vec(g_out_attn), w_out_b, vec(g_xattn),
                       wq_b, mem_k.reshape(b, n_mem, d), mem_v.reshape(b, n_mem, d), wo_b, tm=tm)
    y_prompt = _ffn(x3.reshape(b * l, d), vec(g_ffn2), w1b, w3b, w2b, gfin, final_norm=True, tm=tm)
    h_last = h_last.reshape(2, b, 2, n_groups // 2, n_state).transpose(2, 1, 0, 3, 4).reshape(2, b, gp)
    k = k_t.transpose(0, 3, 1, 2)
    v = v_t.transpose(0, 3, 1, 2)
    heads = (1, b, l, n_heads, ATTN_HEAD_DIM)
    state = (1, b, n_groups, n_state)
    memkv = (1, b, n_mem, N_XHEADS, xd)

    xs = x_sample.reshape(ns, d)
    xs1 = _ffn(xs, vec(g_ffn1), w1a, w3a, w2a, gfin, final_norm=False, tm=ns)
    us, qs, ks, vs = _proj(xs1, vec(g_mix), w_in_b, d_ssm=d_ssm, tm=ns)
    yss, hrs, his = _s5_sample(us, state_ssm_re[0].reshape(ns, gp), state_ssm_im[0].reshape(ns, gp), *s5_w)
    yas = _moba_sample(qs, ks, vs, cache_k, cache_v, page_table)
    xs2, qx = _merge_pre(xs1, yss, yas, vec(g_out_ssm), vec(g_out_attn), w_out_b, vec(g_xattn), wq_b)
    ox = _xattn_sample(qx.reshape(ns, 1, d), cache_mem_k[0].reshape(ns, n_mem, d),
                       cache_mem_v[0].reshape(ns, n_mem, d))
    xs3 = _merge_post(xs2, ox.reshape(ns, d), wo_b)
    y_sample = _ffn(xs3, vec(g_ffn2), w1b, w3b, w2b, gfin, final_norm=True, tm=ns)
    sheads = (1, ns, 1, n_heads, ATTN_HEAD_DIM)
    sstate = (1, ns, n_groups, n_state)

    return (y_prompt.reshape(b, l, d), y_sample.reshape(ns, 1, d),
            k.reshape(heads), v.reshape(heads),
            h_last[0].reshape(state), h_last[1].reshape(state),
            mem_k.reshape(memkv), mem_v.reshape(memkv),
            ks.reshape(sheads), vs.reshape(sheads),
            hrs.reshape(sstate), his.reshape(sstate))
```

```python
import functools
import math

import jax
import jax.numpy as jnp
from jax import lax
from jax.experimental import pallas as pl
from jax.experimental.pallas import tpu as pltpu

RMS_EPS = 1e-6
NEG_INF = -1e30
SSM_GROUP = 16
ATTN_HEAD_DIM = 64
MOBA_BLOCK = 256
MOBA_TOPK = 3
N_XHEADS = 4
LANES = 128
SUBLANES = 8
VMEM_LIMIT = 56 * 1024 * 1024

F32 = jnp.float32
BF16 = jnp.bfloat16
HIGHEST = lax.Precision.HIGHEST


def _params(*sem):
    return pltpu.CompilerParams(dimension_semantics=sem, vmem_limit_bytes=VMEM_LIMIT)


def _rms(x, g):
    ms = jnp.mean(x * x, axis=-1, keepdims=True)
    return x * lax.rsqrt(ms + RMS_EPS) * g


def _dot(a, b):
    return jnp.dot(a, b, preferred_element_type=F32)


def _dot_t(a, b, precision=None):
    return lax.dot_general(a, b, (((1,), (1,)), ((), ())), preferred_element_type=F32,
                           precision=precision)


def _full(shape):
    n = len(shape)
    return pl.BlockSpec(shape, lambda *_: (0,) * n)


def _ffn_body(x_ref, g_ref, w1_ref, w3_ref, w2_ref, gf_ref, o_ref, *, f_chunk, final_norm):
    x = x_ref[...]
    h = _rms(x, g_ref[...]).astype(BF16)
    acc = jnp.zeros(x.shape, F32)
    for c in range(w1_ref.shape[1] // f_chunk):
        cs = slice(c * f_chunk, (c + 1) * f_chunk)
        a = _dot(h, w1_ref[:, cs])
        b = _dot(h, w3_ref[:, cs])
        act = (a * jax.nn.sigmoid(a) * b).astype(BF16)
        acc = acc + _dot(act, w2_ref[cs, :])
    y = x + 0.5 * acc
    if final_norm:
        y = _rms(y, gf_ref[...])
    o_ref[...] = y


def _ffn(x, g, w1, w3, w2, g_final, *, final_norm, tm):
    n, d = x.shape
    f = w1.shape[1]
    f_chunk = 256 if f % 256 == 0 else f
    return pl.pallas_call(
        functools.partial(_ffn_body, f_chunk=f_chunk, final_norm=final_norm),
        grid=(n // tm,),
        in_specs=[pl.BlockSpec((tm, d), lambda i: (i, 0)), _full((1, d)), _full((d, f)), _full((d, f)),
                  _full((f, d)), _full((1, d))],
        out_specs=pl.BlockSpec((tm, d), lambda i: (i, 0)),
        out_shape=jax.ShapeDtypeStruct((n, d), F32),
        compiler_params=_params("parallel"),
        name="ffn_final" if final_norm else "ffn",
    )(x, g, w1, w3, w2, g_final)


def _proj_body(x_ref, g_ref, w_ref, u_ref, q_ref, k_ref, v_ref, *, d_ssm, d_attn):
    h = _rms(x_ref[...], g_ref[...]).astype(BF16)
    p = _dot(h, w_ref[...])
    u_ref[...] = p[:, :d_ssm]
    q_ref[...] = p[:, d_ssm:d_ssm + d_attn]
    k_ref[...] = p[:, d_ssm + d_attn:d_ssm + 2 * d_attn]
    v_ref[...] = p[:, d_ssm + 2 * d_attn:]


def _proj(x, g, w_in, *, d_ssm, tm):
    n, d = x.shape
    d_attn = (w_in.shape[1] - d_ssm) // 3
    row = lambda w: pl.BlockSpec((tm, w), lambda i: (i, 0))
    return pl.pallas_call(
        functools.partial(_proj_body, d_ssm=d_ssm, d_attn=d_attn),
        grid=(n // tm,),
        in_specs=[row(d), _full((1, d)), _full(w_in.shape)],
        out_specs=[row(d_ssm), row(d_attn), row(d_attn), row(d_attn)],
        out_shape=[jax.ShapeDtypeStruct((n, d_ssm), F32)] + [jax.ShapeDtypeStruct((n, d_attn), F32)] * 3,
        compiler_params=_params("parallel"),
        name="mix_proj",
    )(x, g, w_in)


AUX_BIAS, AUX_ONE, AUX_BLOCK, AUX_LANES, MAX_BLOCKS = 0, 3, 16, 64, 32
LOG2E = math.log2(math.e)
BOUND_SLACK = 1.02
MAX_SPREAD = 60.0
PAST_UNROLL = 4


def _alibi_slope(h, n_heads):
    return 2.0 ** (-8.0 * (h + 1) / n_heads)


def _bf16_split3(x):
    hi = x.astype(BF16).astype(F32)
    r = x - hi
    mid = r.astype(BF16).astype(F32)
    lo = (r - mid).astype(BF16).astype(F32)
    return hi, mid, lo


def _proj_prompt_body(x_ref, g_ref, w_ref, u_ref, q_ref, kt_ref, vt_ref, kaug_ref, vtb_ref, sums_ref, kn_ref,
                      *, d_ssm, d_attn, n_heads):
    tm = x_ref.shape[1]
    dh = ATTN_HEAD_DIM
    h = _rms(x_ref[0], g_ref[...]).astype(BF16)
    p = _dot(h, w_ref[...])
    u_ref[...] = p[:, :d_ssm]
    q_ref[0] = p[:, d_ssm:d_ssm + d_attn]
    k = p[:, d_ssm + d_attn:d_ssm + 2 * d_attn]
    v = p[:, d_ssm + 2 * d_attn:]
    kt_ref[0] = k.T.reshape(n_heads, dh, tm)
    vt = v.T.reshape(n_heads, dh, tm)
    vt_ref[0] = vt
    for c in range(tm // MOBA_BLOCK):
        cs = slice(c * MOBA_BLOCK, (c + 1) * MOBA_BLOCK)
        vtb_ref[0, :, c] = vt[:, :, cs].astype(BF16)
        sums_ref[0, c:c + 1, :] = jnp.sum(k[cs], axis=0, keepdims=True)
    pos = pl.program_id(1) * tm + lax.broadcasted_iota(jnp.int32, (tm, AUX_LANES), 0)
    lane = lax.broadcasted_iota(jnp.int32, (tm, AUX_LANES), 1)
    block_hot = (lane - AUX_BLOCK == pos // MOBA_BLOCK).astype(F32)
    posf = pos.astype(F32)
    head_of = (lax.broadcasted_iota(jnp.int32, (d_attn, LANES), 0) // dh
               == lax.broadcasted_iota(jnp.int32, (d_attn, LANES), 1))
    kn_ref[0] = jnp.max(_dot((k * k).astype(BF16), head_of.astype(BF16)), axis=0, keepdims=True)
    pieces = []
    for hh in range(n_heads):
        hi, mid, lo = _bf16_split3((_alibi_slope(hh, n_heads) * LOG2E) * posf)
        aux = jnp.where(lane == AUX_BIAS, hi, jnp.where(lane == AUX_BIAS + 1, mid, jnp.where(
            lane == AUX_BIAS + 2, lo, jnp.where(lane < AUX_ONE + 3, 1.0, block_hot))))
        pieces += [k[:, hh * dh:(hh + 1) * dh], aux]
    kaug_ref[0] = jnp.concatenate(pieces, axis=1).astype(BF16)


def _proj_prompt(x, g, w_in, *, d_ssm, tm):
    b, l, d = x.shape
    d_attn = (w_in.shape[1] - d_ssm) // 3
    dh = ATTN_HEAD_DIM
    n_heads = d_attn // dh
    nb = l // MOBA_BLOCK
    tb = tm // MOBA_BLOCK
    assert nb <= MAX_BLOCKS and tm % MOBA_BLOCK == 0
    row = lambda w: pl.BlockSpec((1, tm, w), lambda i, j: (i, j, 0))
    tr = pl.BlockSpec((1, n_heads, dh, tm), lambda i, j: (i, 0, 0, j))
    return pl.pallas_call(
        functools.partial(_proj_prompt_body, d_ssm=d_ssm, d_attn=d_attn, n_heads=n_heads),
        grid=(b, l // tm),
        in_specs=[row(d), _full((1, d)), _full(w_in.shape)],
        out_specs=[pl.BlockSpec((tm, d_ssm), lambda i, j: (j, i)),
                   row(d_attn), tr, tr, row(n_heads * (dh + AUX_LANES)),
                   pl.BlockSpec((1, n_heads, tb, dh, MOBA_BLOCK), lambda i, j: (i, 0, j, 0, 0)),
                   pl.BlockSpec((1, tb, d_attn), lambda i, j: (i * (l // tm) + j, 0, 0)),
                   pl.BlockSpec((1, 1, LANES), lambda i, j: (i * (l // tm) + j, 0, 0))],
        out_shape=[jax.ShapeDtypeStruct((l, b * d_ssm), F32), jax.ShapeDtypeStruct((b, l, d_attn), F32),
                   jax.ShapeDtypeStruct((b, n_heads, dh, l), F32), jax.ShapeDtypeStruct((b, n_heads, dh, l), F32),
                   jax.ShapeDtypeStruct((b, l, n_heads * (dh + AUX_LANES)), BF16),
                   jax.ShapeDtypeStruct((b, n_heads, nb, dh, MOBA_BLOCK), BF16),
                   jax.ShapeDtypeStruct((b * l // tm, tb, d_attn), F32),
                   jax.ShapeDtypeStruct((b * l // tm, 1, LANES), F32)],
        compiler_params=_params("parallel", "parallel"),
        name="mix_proj_prompt",
    )(x, g, w_in)


def _s5_discretise(a_re, a_im, log_dt, b_re, b_im):
    dt = jnp.exp(log_dt)[:, None]
    mag = jnp.exp(a_re * dt)
    abar_re = mag * jnp.cos(a_im * dt)
    abar_im = mag * jnp.sin(a_im * dt)
    den = a_re * a_re + a_im * a_im
    nr = abar_re - 1.0
    f_re = (nr * a_re + abar_im * a_im) / den
    f_im = (abar_im * a_re - nr * a_im) / den
    bbar_re = f_re[..., None] * b_re - f_im[..., None] * b_im
    bbar_im = f_re[..., None] * b_im + f_im[..., None] * b_re
    return abar_re, abar_im, bbar_re, bbar_im


def _s5_matrices(a_re, a_im, log_dt, b_re, b_im, c_re, c_im):
    g, p, c = b_re.shape
    abar_re, abar_im, bbar_re, bbar_im = _s5_discretise(a_re, a_im, log_dt, b_re, b_im)
    eye = jnp.eye(g, dtype=F32)
    bm_re = jnp.einsum('gpc,gh->gchp', bbar_re, eye).reshape(g * c, g * p)
    bm_im = jnp.einsum('gpc,gh->gchp', bbar_im, eye).reshape(g * c, g * p)
    bmat = jnp.concatenate([bm_re, bm_im], axis=1).astype(BF16)
    cm_re = jnp.einsum('gcp,gh->gphc', c_re, eye).reshape(g * p, g * c)
    cm_im = jnp.einsum('gcp,gh->gphc', c_im, eye).reshape(g * p, g * c)
    cmat = jnp.concatenate([cm_re, -cm_im], axis=0).astype(BF16)
    return abar_re.reshape(1, g * p), abar_im.reshape(1, g * p), bmat, cmat


S5_IN_TILES = 8
S5_OUT_TILES = 2


def _s5_compact(bmat, cmat):
    d_ssm, gp2 = bmat.shape
    gp = gp2 // 2
    ci, si = d_ssm // S5_IN_TILES, gp // S5_IN_TILES
    bc = jnp.stack([jnp.concatenate([bmat[c * ci:(c + 1) * ci, c * si:(c + 1) * si],
                                     bmat[c * ci:(c + 1) * ci, gp + c * si:gp + (c + 1) * si]], axis=1)
                    for c in range(S5_IN_TILES)])
    co, so = d_ssm // S5_OUT_TILES, gp // S5_OUT_TILES
    cc = jnp.stack([jnp.stack([cmat[part * gp + o * so:part * gp + (o + 1) * so, o * co:(o + 1) * co]
                               for part in range(2)]) for o in range(S5_OUT_TILES)])
    return bc, cc


def _s5_output(ch, u, d_ref, wg_ref, bg_ref):
    y = ch + d_ref[...] * u
    z = jax.nn.gelu(y, approximate=True)
    return z * jax.nn.sigmoid(_dot(z.astype(BF16), wg_ref[...]) + bg_ref[...])


def _s5_prompt_body(u_ref, ar_ref, ai_ref, bc_ref, cc_ref, d_ref, wg_ref, bg_ref,
                    y_ref, hlast_ref, hbuf, carry, tbuf, *, nbatch):
    gp = ar_ref.shape[1]
    steps = u_ref.shape[0]
    rows = steps * nbatch
    d_ssm = u_ref.shape[1] // nbatch
    tiles = d_ssm // LANES

    @pl.when(pl.program_id(0) == 0)
    def _():
        carry[...] = jnp.zeros(carry.shape, F32)

    for b in range(nbatch):
        for k in range(tiles):
            tbuf[k, pl.ds(b, steps, stride=nbatch), :] = u_ref[:, b * d_ssm + k * LANES:b * d_ssm + (k + 1) * LANES]
    u = jnp.concatenate([tbuf[k] for k in range(tiles)], axis=1)
    ub = u.astype(BF16)
    ci, si = u.shape[1] // S5_IN_TILES, gp // S5_IN_TILES
    for c in range(S5_IN_TILES):
        bu = _dot(ub[:, c * ci:(c + 1) * ci], bc_ref[c])
        hbuf[:, c * si:(c + 1) * si] = bu[:, :si]
        hbuf[:, gp + c * si:gp + (c + 1) * si] = bu[:, si:]

    half = gp // 2
    lower = lax.broadcasted_iota(jnp.int32, (SUBLANES, half), 0) < nbatch
    halves = lambda v: jnp.where(lower, jnp.broadcast_to(v[:, :half], (SUBLANES, half)),
                                 jnp.broadcast_to(v[:, half:], (SUBLANES, half)))
    ar, ai = halves(ar_ref[...]), halves(ai_ref[...])
    swap = lambda v: pltpu.roll(v, nbatch, 0)

    def step(i, st):
        hr, hi = st
        rows8 = pl.ds(pl.multiple_of(i * SUBLANES, SUBLANES), SUBLANES)
        new = []
        for part in range(2):
            a = hbuf[rows8, part * gp:part * gp + half]
            b = hbuf[rows8, part * gp + half:(part + 1) * gp]
            new.append((jnp.where(lower, a, swap(b)), jnp.where(lower, swap(a), b)))
        (x0r, x1r), (x0i, x1i) = new
        n0r = ar * hr - ai * hi + x0r
        n0i = ar * hi + ai * hr + x0i
        n1r = ar * n0r - ai * n0i + x1r
        n1i = ar * n0i + ai * n0r + x1i
        for part, (n0, n1) in enumerate(((n0r, n1r), (n0i, n1i))):
            hbuf[rows8, part * gp:part * gp + half] = jnp.where(lower, n0, swap(n1))
            hbuf[rows8, part * gp + half:(part + 1) * gp] = jnp.where(lower, swap(n0), n1)
        return n1r, n1i

    hr, hi = lax.fori_loop(0, rows // SUBLANES, step, (carry[:, :half], carry[:, half:]))
    carry[:, :half] = hr
    carry[:, half:] = hi
    hlast_ref[:, :half] = hr
    hlast_ref[:, half:] = hi
    so = gp // S5_OUT_TILES
    ch = jnp.concatenate(
        [_dot(hbuf[:, o * so:(o + 1) * so].astype(BF16), cc_ref[o, 0])
         + _dot(hbuf[:, gp + o * so:gp + (o + 1) * so].astype(BF16), cc_ref[o, 1]) for o in range(S5_OUT_TILES)],
        axis=1)
    y = _s5_output(ch, u, d_ref, wg_ref, bg_ref)
    for k in range(tiles):
        tbuf[k] = y[:, k * LANES:(k + 1) * LANES]
    for b in range(nbatch):
        for k in range(tiles):
            y_ref[:, b * d_ssm + k * LANES:b * d_ssm + (k + 1) * LANES] = tbuf[k, pl.ds(b, steps, stride=nbatch), :]


def _s5_prompt(u_tb, nbatch, ar, ai, bmat, cmat, d, w_glu, b_glu, *, t_chunk):
    l, width = u_tb.shape
    d_ssm = width // nbatch
    gp = ar.shape[1]
    bmat, cmat = _s5_compact(bmat, cmat)
    rows = t_chunk * nbatch
    assert 2 * nbatch == SUBLANES and l % t_chunk == 0 and d_ssm % LANES == 0
    return pl.pallas_call(
        functools.partial(_s5_prompt_body, nbatch=nbatch),
        grid=(l // t_chunk,),
        in_specs=[pl.BlockSpec((t_chunk, width), lambda i: (i, 0)), _full((1, gp)), _full((1, gp)),
                  _full(bmat.shape), _full(cmat.shape), _full((1, d_ssm)), _full(w_glu.shape),
                  _full((1, d_ssm))],
        out_specs=[pl.BlockSpec((t_chunk, width), lambda i: (i, 0)), _full((SUBLANES, gp))],
        out_shape=[jax.ShapeDtypeStruct((l, width), F32),
                   jax.ShapeDtypeStruct((SUBLANES, gp), F32)],
        scratch_shapes=[pltpu.VMEM((rows, 2 * gp), F32), pltpu.VMEM((SUBLANES, gp), F32),
                        pltpu.VMEM((d_ssm // LANES, rows, LANES), F32)],
        compiler_params=_params("arbitrary"),
        name="s5_prompt",
    )(u_tb, ar, ai, bmat, cmat, d, w_glu, b_glu)


def _s5_sample_body(u_ref, h0r_ref, h0i_ref, ar_ref, ai_ref, bmat_ref, cmat_ref, d_ref, wg_ref, bg_ref,
                    y_ref, hr_ref, hi_ref):
    gp = ar_ref.shape[1]
    u = u_ref[...]
    bu = _dot(u.astype(BF16), bmat_ref[...])
    ar, ai = ar_ref[...], ai_ref[...]
    h0r, h0i = h0r_ref[...], h0i_ref[...]
    hr = bu[:, :gp] + ar * h0r - ai * h0i
    hi = bu[:, gp:] + ar * h0i + ai * h0r
    hr_ref[...] = hr
    hi_ref[...] = hi
    h_bf = jnp.concatenate([hr, hi], axis=1).astype(BF16)
    y_ref[...] = _s5_output(_dot(h_bf, cmat_ref[...]), u, d_ref, wg_ref, bg_ref)


def _s5_sample(u, h0r, h0i, ar, ai, bmat, cmat, d, w_glu, b_glu):
    n, d_ssm = u.shape
    gp = ar.shape[1]
    args = (u, h0r, h0i, ar, ai, bmat, cmat, d, w_glu, b_glu)
    return pl.pallas_call(
        _s5_sample_body,
        grid=(1,),
        in_specs=[_full(a.shape) for a in args],
        out_specs=[_full((n, d_ssm)), _full((n, gp)), _full((n, gp))],
        out_shape=[jax.ShapeDtypeStruct((n, d_ssm), F32), jax.ShapeDtypeStruct((n, gp), F32),
                   jax.ShapeDtypeStruct((n, gp), F32)],
        compiler_params=_params("arbitrary"),
        name="s5_sample",
    )(*args)


def _top_blocks(gate, brow, n_valid):
    bf = brow.astype(F32)
    g = jnp.where(brow < n_valid, gate, NEG_INF)
    sel = jnp.zeros(gate.shape, jnp.bool_)
    for _ in range(MOBA_TOPK):
        mx = jnp.max(g, axis=0, keepdims=True)
        first = jnp.min(jnp.where(g == mx, bf, float(gate.shape[0])), axis=0, keepdims=True)
        pick = bf == first
        sel = jnp.logical_or(sel, pick)
        g = jnp.where(pick, -jnp.inf, g)
    return jnp.logical_and(sel, brow < n_valid)


def _moba_prompt_body(q_ref, kaug_ref, vtb_ref, sums_ref, kn_ref, o_ref, qaug, acc, s_scr, p_scr, *, n_heads):
    j = pl.program_id(1)
    blk = MOBA_BLOCK
    dh = ATTN_HEAD_DIM
    ka = dh + AUX_LANES
    nb = sums_ref.shape[1]
    t0 = pl.multiple_of(j * blk, blk)
    scale = dh ** -0.5
    qt = q_ref[0].T
    brow = lax.broadcasted_iota(jnp.int32, (MAX_BLOCKS, blk), 0)
    crow = lax.broadcasted_iota(jnp.int32, (AUX_BLOCK, blk), 0)
    t0f = jnp.broadcast_to(t0.astype(F32), (AUX_BLOCK, blk))
    kmax2 = jnp.max(kn_ref[0], axis=0, keepdims=True)
    hlane = lax.broadcasted_iota(jnp.int32, kmax2.shape, 1)
    tq = lax.broadcasted_iota(jnp.int32, (1, blk), 1).astype(F32)
    shifts, spreads = [], []
    for h in range(n_heads):
        slope = _alibi_slope(h, n_heads)
        qth = qt[h * dh:(h + 1) * dh]
        means = sums_ref[0, :, h * dh:(h + 1) * dh] * (1.0 / blk)
        gate = jnp.dot(means, qth, precision=HIGHEST, preferred_element_type=F32)
        if nb < MAX_BLOCKS:
            gate = jnp.concatenate([gate, jnp.full((MAX_BLOCKS - nb, blk), NEG_INF, F32)], axis=0)
        keep = jnp.logical_or(_top_blocks(gate, brow, j), brow == j)
        mask = jnp.where(keep, 0.0, NEG_INF)
        hi, mid, lo = _bf16_split3((-slope * LOG2E) * t0f)
        const = jnp.where(crow < AUX_ONE, 1.0, jnp.where(crow == AUX_ONE, hi, jnp.where(
            crow == AUX_ONE + 1, mid, jnp.where(crow == AUX_ONE + 2, lo, 0.0))))
        rest = jnp.zeros((AUX_LANES - AUX_BLOCK - MAX_BLOCKS, blk), F32)
        qaug[h] = jnp.concatenate([qth * (scale * LOG2E), const, mask, rest], axis=0).astype(BF16)
        k2 = jnp.max(jnp.where(hlane == h, kmax2, 0.0), axis=1, keepdims=True)
        bound = jnp.sqrt(jnp.sum(qth * qth, axis=0, keepdims=True) * k2) * (scale * BOUND_SLACK)
        shifts.append((bound + slope * tq) * LOG2E)
        spreads.append(bound)
    shift = jnp.concatenate(shifts, axis=0)
    safe = 2.0 * jnp.max(jnp.concatenate(spreads, axis=0)) < MAX_SPREAD

    def scores(n, h):
        r = pl.multiple_of(n * blk, blk)
        return _dot(kaug_ref[0, pl.ds(r, blk), h * ka:(h + 1) * ka], qaug[h])

    causal = (lax.broadcasted_iota(jnp.int32, (blk, blk), 0) <= lax.broadcasted_iota(jnp.int32, (blk, blk), 1))

    def finish(l_all):
        out_t = jnp.concatenate([acc[h * dh:(h + 1) * dh, :] / l_all[h:h + 1] for h in range(n_heads)], axis=0)
        o_ref[0] = out_t.T

    @pl.when(safe)
    def _():
        def consume(n, l_all, get_scores, own=False):
            ls = []
            for h in range(n_heads):
                s = get_scores(h)
                if own:
                    s = jnp.where(causal, s, NEG_INF)
                p = jnp.exp2(s - shift[h:h + 1])
                p_scr[h] = p.astype(BF16)
                ls.append(jnp.sum(p, axis=0, keepdims=True))
            for h in range(n_heads):
                hs = slice(h * dh, (h + 1) * dh)
                pv = _dot(vtb_ref[0, h, n], p_scr[h])
                acc[hs, :] = pv if own else acc[hs, :] + pv
            l_new = jnp.concatenate(ls, axis=0)
            return l_new if own else l_all + l_new

        l_all = consume(j, None, lambda h: scores(j, h), own=True)

        def group(i, l_all):
            for r in range(PAST_UNROLL):
                n = i * PAST_UNROLL + r
                l_all = consume(n, l_all, lambda h: scores(n, h))
            return l_all

        done = (j // PAST_UNROLL) * PAST_UNROLL
        l_all = lax.fori_loop(0, j // PAST_UNROLL, group, l_all)
        finish(lax.fori_loop(done, j, lambda n, l_all: consume(n, l_all, lambda h: scores(n, h)), l_all))

    @pl.when(jnp.logical_not(safe))
    def _():
        def tile(n, st, own):
            ms, ls, alphas = [], [], []
            for h in range(n_heads):
                s_scr[h] = scores(n, h)
            for h in range(n_heads):
                s = s_scr[h]
                if own:
                    s = jnp.where(causal, s, NEG_INF)
                    m_new = jnp.max(s, axis=0, keepdims=True)
                else:
                    m_new = jnp.maximum(st[0][h:h + 1], jnp.max(s, axis=0, keepdims=True))
                    alphas.append(jnp.exp2(st[0][h:h + 1] - m_new))
                p = jnp.exp2(s - m_new)
                p_scr[h] = p.astype(BF16)
                ms.append(m_new)
                l_new = jnp.sum(p, axis=0, keepdims=True)
                ls.append(l_new if own else alphas[h] * st[1][h:h + 1] + l_new)
            for h in range(n_heads):
                hs = slice(h * dh, (h + 1) * dh)
                pv = _dot(vtb_ref[0, h, n], p_scr[h])
                acc[hs, :] = pv if own else alphas[h] * acc[hs, :] + pv
            return jnp.concatenate(ms, axis=0), jnp.concatenate(ls, axis=0)

        finish(lax.fori_loop(0, j, lambda n, st: tile(n, st, False), tile(j, None, True))[1])


def _moba_prompt(q, kaug, vtb, sums, knorm):
    b, l, d = q.shape
    nb = l // MOBA_BLOCK
    n_heads = d // ATTN_HEAD_DIM
    ka = ATTN_HEAD_DIM + AUX_LANES
    once = pl.Buffered(1)
    return pl.pallas_call(
        functools.partial(_moba_prompt_body, n_heads=n_heads),
        grid=(b, nb),
        in_specs=[pl.BlockSpec((1, MOBA_BLOCK, d), lambda i, j: (i, j, 0)),
                  pl.BlockSpec((1, l, n_heads * ka), lambda i, j: (i, 0, 0), pipeline_mode=once),
                  pl.BlockSpec((1, n_heads, nb, ATTN_HEAD_DIM, MOBA_BLOCK), lambda i, j: (i, 0, 0, 0, 0),
                               pipeline_mode=once),
                  pl.BlockSpec((1, nb, d), lambda i, j: (i, 0, 0)),
                  pl.BlockSpec((1,) + knorm.shape[1:], lambda i, j: (i, 0, 0))],
        out_specs=pl.BlockSpec((1, MOBA_BLOCK, d), lambda i, j: (i, j, 0)),
        out_shape=jax.ShapeDtypeStruct((b, l, d), F32),
        scratch_shapes=[pltpu.VMEM((n_heads, ka, MOBA_BLOCK), BF16), pltpu.VMEM((d, MOBA_BLOCK), F32),
                        pltpu.VMEM((n_heads, MOBA_BLOCK, MOBA_BLOCK), F32),
                        pltpu.VMEM((n_heads, MOBA_BLOCK, MOBA_BLOCK), BF16)],
        compiler_params=_params("parallel", "arbitrary"),
        name="moba_prompt",
    )(q, kaug, vtb, sums, knorm)


def _moba_kpass_body(pt_ref, q_ref, *refs, pages, n_heads):
    del pt_ref
    k_refs, (lg_ref, ps_ref) = refs[:pages], refs[pages:]
    s = pl.program_id(1)
    dh, page = k_refs[0].shape[3:]

    @pl.when(s == 0)
    def _():
        ps_ref[...] = jnp.zeros(ps_ref.shape, F32)

    qcols = q_ref[0]
    qb = [jnp.broadcast_to(qcols[:, h:h + 1], (dh, page)) for h in range(n_heads)]
    plane = lax.broadcasted_iota(jnp.int32, (n_heads, ps_ref.shape[2]), 1)
    psum = ps_ref[0]
    for i in range(pages):
        lg = jnp.concatenate([jnp.sum(k_refs[i][0, 0, h] * qb[h], axis=0, keepdims=True)
                              for h in range(n_heads)], axis=0)
        ppb = MOBA_BLOCK // page
        lg_ref[0, i // ppb, :, (i % ppb) * page:(i % ppb + 1) * page] = lg
        psum = jnp.where(plane == s * pages + i, jnp.sum(lg, axis=1, keepdims=True), psum)
    ps_ref[0] = psum


def _moba_kpass(q_t, cache_kt, page_table, *, pages):
    n, dh, n_heads = q_t.shape
    n_pages = page_table.shape[1]
    page = cache_kt.shape[4]
    k_spec = lambda i: pl.BlockSpec((1, 1, n_heads, dh, page),
                                    lambda b, s, pt: (0, pt[b, s * pages + i], 0, 0, 0))
    grid_spec = pltpu.PrefetchScalarGridSpec(
        num_scalar_prefetch=1,
        grid=(n, n_pages // pages),
        in_specs=[pl.BlockSpec((1, dh, n_heads), lambda b, s, pt: (b, 0, 0))]
                 + [k_spec(i) for i in range(pages)],
        out_specs=[pl.BlockSpec((1, pages * page // MOBA_BLOCK, n_heads, MOBA_BLOCK),
                                lambda b, s, pt: (b, s, 0, 0)),
                   pl.BlockSpec((1, n_heads, n_pages), lambda b, s, pt: (b, 0, 0))])
    return pl.pallas_call(
        functools.partial(_moba_kpass_body, pages=pages, n_heads=n_heads),
        grid_spec=grid_spec,
        out_shape=[jax.ShapeDtypeStruct((n, n_pages * page // MOBA_BLOCK, n_heads, MOBA_BLOCK), F32),
                   jax.ShapeDtypeStruct((n, n_heads, n_pages), F32)],
        compiler_params=_params("parallel", "arbitrary"),
        name="moba_sample_kpass",
    )(page_table, q_t, *([cache_kt] * pages))


def _moba_select_body(ps_ref, idx_ref, *, pages_per_block, n_blocks):
    ps = ps_ref[...]
    lanes = ps.shape[1]
    lane = lax.broadcasted_iota(jnp.int32, (1, lanes), 1)
    g = ps
    for i in range(1, pages_per_block):
        g = g + pltpu.roll(ps, lanes - i, 1)
    g = g * (1.0 / MOBA_BLOCK)
    is_block = jnp.logical_and(lane % pages_per_block == 0, lane < n_blocks * pages_per_block)
    g = jnp.where(is_block, g, -jnp.inf)
    lanef = lane.astype(F32)
    out = jnp.zeros(idx_ref.shape, F32)
    olane = lax.broadcasted_iota(jnp.int32, idx_ref.shape, 1)
    for r in range(MOBA_TOPK):
        mx = jnp.max(g, axis=1, keepdims=True)
        first = jnp.min(jnp.where(g == mx, lanef, float(lanes)), axis=1, keepdims=True)
        out = jnp.where(olane == r, first * (1.0 / pages_per_block), out)
        g = jnp.where(lanef == first, -jnp.inf, g)
    idx_ref[...] = out.astype(jnp.int32)


def _moba_select(page_sums2, *, pages_per_block, n_blocks):
    rows, lanes = page_sums2.shape
    return pl.pallas_call(
        functools.partial(_moba_select_body, pages_per_block=pages_per_block, n_blocks=n_blocks),
        grid=(1,),
        in_specs=[_full((rows, lanes))],
        out_specs=_full((rows, LANES)),
        out_shape=jax.ShapeDtypeStruct((rows, LANES), jnp.int32),
        compiler_params=_params("arbitrary"),
        name="moba_sample_select",
    )(page_sums2)


def _moba_attend_body(blk_ref, pg_ref, q_ref, kn_ref, vn_ref, *refs, pages_per_block, n_heads, past_len):
    del pg_ref
    n_lg = MOBA_TOPK
    n_v = MOBA_TOPK * pages_per_block
    lg_refs, v_refs, o_ref = refs[:n_heads * n_lg], refs[n_heads * n_lg:n_heads * (n_lg + n_v)], refs[-1]
    b = pl.program_id(0)
    dh = ATTN_HEAD_DIM
    blk = MOBA_BLOCK
    page = blk // pages_per_block
    scale = dh ** -0.5
    lane = lax.broadcasted_iota(jnp.int32, (1, blk), 1)
    qk = q_ref[0] * kn_ref[0]
    outs = []
    for h in range(n_heads):
        hs = slice(h * dh, (h + 1) * dh)
        slope = _alibi_slope(h, n_heads)
        s_own = jnp.sum(qk[:, hs], axis=1, keepdims=True) * scale
        ss = []
        for i in range(n_lg):
            pos = blk_ref[b, h, i] * blk + lane
            ss.append(lg_refs[h * n_lg + i][0, 0, h:h + 1, :] * scale - slope * (past_len - pos).astype(F32))
        m = s_own
        for s in ss:
            m = jnp.maximum(m, jnp.max(s, axis=1, keepdims=True))
        p_own = jnp.exp(s_own - m)
        l = p_own
        acc = p_own * vn_ref[0][:, hs]
        for i in range(n_lg):
            p = jnp.exp(ss[i] - m)
            l = l + jnp.sum(p, axis=1, keepdims=True)
            for c in range(pages_per_block):
                vt = v_refs[h * n_v + i * pages_per_block + c][0, 0, 0].astype(BF16)
                acc = acc + _dot_t(p[:, c * page:(c + 1) * page].astype(BF16), vt)
        outs.append(acc / l)
    o_ref[0] = jnp.concatenate(outs, axis=1)


def _moba_attend(blocks, pages_idx, q3, kn3, vn3, logits, cache_vt, *, pages_per_block, past_len):
    n, _, d = q3.shape
    _, _, n_heads, dh, page = cache_vt.shape
    row = pl.BlockSpec((1, 1, d), lambda b, bl, pg: (b, 0, 0))
    lg_spec = lambda h, i: pl.BlockSpec(
        (1, 1, n_heads, MOBA_BLOCK), lambda b, bl, pg: (b, bl[b, h, i], 0, 0))
    v_spec = lambda h, i: pl.BlockSpec(
        (1, 1, 1, dh, page), lambda b, bl, pg: (0, pg[b, h, i], h, 0, 0))
    n_v = MOBA_TOPK * pages_per_block
    lg_specs = [lg_spec(h, i) for h in range(n_heads) for i in range(MOBA_TOPK)]
    v_specs = [v_spec(h, i) for h in range(n_heads) for i in range(n_v)]
    grid_spec = pltpu.PrefetchScalarGridSpec(
        num_scalar_prefetch=2,
        grid=(n,),
        in_specs=[row, row, row] + lg_specs + v_specs,
        out_specs=row)
    return pl.pallas_call(
        functools.partial(_moba_attend_body, pages_per_block=pages_per_block, n_heads=n_heads,
                          past_len=past_len),
        grid_spec=grid_spec,
        out_shape=jax.ShapeDtypeStruct((n, 1, d), F32),
        compiler_params=_params("parallel"),
        name="moba_sample_attend",
    )(blocks, pages_idx, q3, kn3, vn3, *([logits] * len(lg_specs)), *([cache_vt] * len(v_specs)))


def _moba_sample(q, k_new, v_new, cache_k, cache_v, page_table):
    n, d = q.shape
    _, n_pool, page, n_heads, dh = cache_k.shape
    n_pages = page_table.shape[1]
    past_len = n_pages * page
    pages_per_block = MOBA_BLOCK // page
    n_blocks = n_pages // pages_per_block
    assert n_blocks >= MOBA_TOPK and n_pages % LANES == 0
    q3 = q.reshape(n, 1, d)
    cache_kt = cache_k.transpose(0, 1, 3, 4, 2)
    cache_vt = cache_v.transpose(0, 1, 3, 4, 2)
    kpass_pages = 32 if n_pages % 32 == 0 else 1
    logits, page_sums = _moba_kpass(q.reshape(n, n_heads, dh).swapaxes(1, 2), cache_kt, page_table,
                                    pages=kpass_pages)
    blocks = _moba_select(page_sums.reshape(n * n_heads, n_pages), pages_per_block=pages_per_block,
                          n_blocks=n_blocks)[:, :MOBA_TOPK].reshape(n, n_heads, MOBA_TOPK)
    logical = (blocks[..., None] * pages_per_block + jnp.arange(pages_per_block)).reshape(n, n_heads, -1)
    pages_idx = jnp.take_along_axis(page_table[:, None, :], logical, axis=2)
    y = _moba_attend(blocks, pages_idx, q3, k_new.reshape(n, 1, d), v_new.reshape(n, 1, d),
                     logits, cache_vt,
                     pages_per_block=pages_per_block, past_len=past_len)
    return y.reshape(n, d)


def _memkv_body(m_ref, g_ref, wk_ref, wv_ref, k_ref, v_ref):
    h = _rms(m_ref[...], g_ref[...]).astype(BF16)
    k_ref[...] = _dot(h, wk_ref[...])
    v_ref[...] = _dot(h, wv_ref[...])


def _memkv(mem, g, wk, wv, *, tm):
    n, d = mem.shape
    row = pl.BlockSpec((tm, d), lambda i: (i, 0))
    return pl.pallas_call(
        _memkv_body,
        grid=(n // tm,),
        in_specs=[row, _full((1, d)), _full(wk.shape), _full(wv.shape)],
        out_specs=[row, row],
        out_shape=[jax.ShapeDtypeStruct((n, d), F32)] * 2,
        compiler_params=_params("parallel"),
        name="mem_kv",
    )(mem, g, wk, wv)


def _mix_out(x, ys, ya, gs, ga, w_out_ref, d_ssm):
    ysn = _rms(ys, gs).astype(BF16)
    yan = _rms(ya, ga).astype(BF16)
    return x + _dot(ysn, w_out_ref[:d_ssm, :]) + _dot(yan, w_out_ref[d_ssm:, :])


def _xattn_heads(q, mk_ref, mv_ref):
    d = q.shape[1]
    xd = d // N_XHEADS
    outs = []
    for h in range(N_XHEADS):
        hs = slice(h * xd, (h + 1) * xd)
        s = _dot_t(q[:, hs].astype(BF16), mk_ref[0, :, hs].astype(BF16)) * (xd ** -0.5)
        m = jnp.max(s, axis=1, keepdims=True)
        p = jnp.exp(s - m)
        l = jnp.sum(p, axis=1, keepdims=True)
        outs.append(_dot(p.astype(BF16), mv_ref[0, :, hs].astype(BF16)) / l)
    return jnp.concatenate(outs, axis=1)


def _merge_prompt_body(x_ref, ys_ref, ya_ref, gs_ref, ga_ref, wout_ref, gx_ref, wq_ref, mk_ref, mv_ref,
                       wo_ref, o_ref, *, d_ssm):
    x2 = _mix_out(x_ref[0], ys_ref[...], ya_ref[0], gs_ref[...], ga_ref[...], wout_ref, d_ssm)
    q = _dot(_rms(x2, gx_ref[...]).astype(BF16), wq_ref[...])
    o = _xattn_heads(q, mk_ref, mv_ref)
    o_ref[0] = x2 + _dot(o.astype(BF16), wo_ref[...])


def _merge_prompt(x, ys, ya, gs, ga, w_out, gx, wq, mk, mv, wo, *, tm):
    b, l, d = x.shape
    d_ssm = ys.shape[1] // b
    d_attn = ya.shape[2]
    n_mem = mk.shape[1]
    row = lambda w: pl.BlockSpec((1, tm, w), lambda i, j: (i, j, 0))
    mem = pl.BlockSpec((1, n_mem, d), lambda i, j: (i, 0, 0))
    return pl.pallas_call(
        functools.partial(_merge_prompt_body, d_ssm=d_ssm),
        grid=(b, l // tm),
        in_specs=[row(d), pl.BlockSpec((tm, d_ssm), lambda i, j: (j, i)), row(d_attn), _full((1, d_ssm)),
                  _full((1, d_attn)), _full(w_out.shape), _full((1, d)), _full(wq.shape), mem, mem,
                  _full(wo.shape)],
        out_specs=row(d),
        out_shape=jax.ShapeDtypeStruct((b, l, d), F32),
        compiler_params=_params("parallel", "parallel"),
        name="merge_prompt",
    )(x, ys, ya, gs, ga, w_out, gx, wq, mk, mv, wo)


def _merge_pre_body(x_ref, ys_ref, ya_ref, gs_ref, ga_ref, wout_ref, gx_ref, wq_ref, x2_ref, q_ref, *, d_ssm):
    x2 = _mix_out(x_ref[...], ys_ref[...], ya_ref[...], gs_ref[...], ga_ref[...], wout_ref, d_ssm)
    x2_ref[...] = x2
    q_ref[...] = _dot(_rms(x2, gx_ref[...]).astype(BF16), wq_ref[...])


def _merge_pre(x, ys, ya, gs, ga, w_out, gx, wq):
    n, d = x.shape
    args = (x, ys, ya, gs, ga, w_out, gx, wq)
    return pl.pallas_call(
        functools.partial(_merge_pre_body, d_ssm=ys.shape[1]),
        grid=(1,),
        in_specs=[_full(a.shape) for a in args],
        out_specs=[_full((n, d)), _full((n, d))],
        out_shape=[jax.ShapeDtypeStruct((n, d), F32)] * 2,
        compiler_params=_params("arbitrary"),
        name="merge_sample_pre",
    )(*args)


def _xattn_sample_body(q_ref, mk_ref, mv_ref, o_ref):
    o_ref[0] = _xattn_heads(q_ref[0], mk_ref, mv_ref)


def _xattn_sample(q3, mk, mv):
    n, _, d = q3.shape
    n_mem = mk.shape[1]
    row = pl.BlockSpec((1, 1, d), lambda i: (i, 0, 0))
    mem = pl.BlockSpec((1, n_mem, d), lambda i: (i, 0, 0))
    return pl.pallas_call(
        _xattn_sample_body,
        grid=(n,),
        in_specs=[row, mem, mem],
        out_specs=row,
        out_shape=jax.ShapeDtypeStruct((n, 1, d), F32),
        compiler_params=_params("parallel"),
        name="xattn_sample",
    )(q3, mk, mv)


def _merge_post_body(x_ref, o_ref, wo_ref, y_ref):
    y_ref[...] = x_ref[...] + _dot(o_ref[...].astype(BF16), wo_ref[...])


def _merge_post(x2, o, wo):
    n, d = x2.shape
    return pl.pallas_call(
        _merge_post_body,
        grid=(1,),
        in_specs=[_full((n, d)), _full((n, d)), _full(wo.shape)],
        out_specs=_full((n, d)),
        out_shape=jax.ShapeDtypeStruct((n, d), F32),
        compiler_params=_params("arbitrary"),
        name="merge_sample_post",
    )(x2, o, wo)


def kernel(x_prompt, x_sample, mem_prompt, cache_k, cache_v, page_table, state_ssm_re, state_ssm_im, cache_mem_k, cache_mem_v, g_ffn1, w1_ffn1, w3_ffn1, w2_ffn1, g_mix, w_in, ssm_a_re, ssm_a_im, ssm_log_dt, ssm_b_re, ssm_b_im, ssm_c_re, ssm_c_im, ssm_d, w_glu, b_glu, g_out_ssm, g_out_attn, w_out, g_xattn, g_mem, wq_x, wk_x, wv_x, wo_x, g_ffn2, w1_ffn2, w3_ffn2, w2_ffn2, g_final):
    depth = g_ffn1.shape[0]
    assert depth == 1
    b, l, d = x_prompt.shape
    ns, ls, _ = x_sample.shape
    assert ls == 1
    n_groups, n_state = ssm_a_re.shape[1:]
    gp = n_groups * n_state
    d_ssm = n_groups * SSM_GROUP
    d_attn = (w_in.shape[2] - d_ssm) // 3
    n_heads = d_attn // ATTN_HEAD_DIM
    n_mem = mem_prompt.shape[1]
    xd = d // N_XHEADS

    vec = lambda a: a[0].reshape(1, -1)
    wb = lambda a: a[0].astype(BF16)
    w1a, w3a, w2a = wb(w1_ffn1), wb(w3_ffn1), wb(w2_ffn1)
    w1b, w3b, w2b = wb(w1_ffn2), wb(w3_ffn2), wb(w2_ffn2)
    w_in_b, w_out_b, w_glu_b = wb(w_in), wb(w_out), wb(w_glu)
    wq_b, wk_b, wv_b, wo_b = wb(wq_x), wb(wk_x), wb(wv_x), wb(wo_x)
    gfin = g_final.reshape(1, -1)
    ar, ai, bmat, cmat = _s5_matrices(ssm_a_re[0], ssm_a_im[0], ssm_log_dt[0], ssm_b_re[0], ssm_b_im[0],
                                      ssm_c_re[0], ssm_c_im[0])
    s5_w = (ar, ai, bmat, cmat, vec(ssm_d), w_glu_b, vec(b_glu))

    tm = 512 if l % 512 == 0 else l
    xp = x_prompt.reshape(b * l, d)
    mem_k, mem_v = _memkv(mem_prompt.reshape(b * n_mem, d), vec(g_mem), wk_b, wv_b,
                          tm=min(256, b * n_mem))
    x1 = _ffn(xp, vec(g_ffn1), w1a, w3a, w2a, gfin, final_norm=False, tm=tm)
    x1 = x1.reshape(b, l, d)
    u, q, k_t, v_t, kaug, vtb, sums, knorm = _proj_prompt(x1, vec(g_mix), w_in_b, d_ssm=d_ssm, tm=tm)
    t_chunk = 128 if l % 128 == 0 else l
    ys, h_last = _s5_prompt(u, b, *s5_w, t_chunk=t_chunk)
    ya = _moba_prompt(q, kaug, vtb, sums.reshape(b, l // MOBA_BLOCK, d_attn), knorm.reshape(b, l // tm, LANES))
    x3 = _merge_prompt(x1, ys, ya, vec(g_out_ssm), vec(g_out_attn), w_out_b, vec(g_xattn),
                       wq_b, mem_k.reshape(b, n_mem, d), mem_v.reshape(b, n_mem, d), wo_b, tm=tm)
    y_prompt = _ffn(x3.reshape(b * l, d), vec(g_ffn2), w1b, w3b, w2b, gfin, final_norm=True, tm=tm)
    h_last = h_last.reshape(2, b, 2, n_groups // 2, n_state).transpose(2, 1, 0, 3, 4).reshape(2, b, gp)
    k = k_t.transpose(0, 3, 1, 2)
    v = v_t.transpose(0, 3, 1, 2)
    heads = (1, b, l, n_heads, ATTN_HEAD_DIM)
    state = (1, b, n_groups, n_state)
    memkv = (1, b, n_mem, N_XHEADS, xd)

    xs = x_sample.reshape(ns, d)
    xs1 = _ffn(xs, vec(g_ffn1), w1a, w3a, w2a, gfin, final_norm=False, tm=ns)
    us, qs, ks, vs = _proj(xs1, vec(g_mix), w_in_b, d_ssm=d_ssm, tm=ns)
    yss, hrs, his = _s5_sample(us, state_ssm_re[0].reshape(ns, gp), state_ssm_im[0].reshape(ns, gp), *s5_w)
    yas = _moba_sample(qs, ks, vs, cache_k, cache_v, page_table)
    xs2, qx = _merge_pre(xs1, yss, yas, vec(g_out_ssm), vec(g_out_attn), w_out_b, vec(g_xattn), wq_b)
    ox = _xattn_sample(qx.reshape(ns, 1, d), cache_mem_k[0].reshape(ns, n_mem, d),
                       cache_mem_v[0].reshape(ns, n_mem, d))
    xs3 = _merge_post(xs2, ox.reshape(ns, d), wo_b)
    y_sample = _ffn(xs3, vec(g_ffn2), w1b, w3b, w2b, gfin, final_norm=True, tm=ns)
    sheads = (1, ns, 1, n_heads, ATTN_HEAD_DIM)
    sstate = (1, ns, n_groups, n_state)

    return (y_prompt.reshape(b, l, d), y_sample.reshape(ns, 1, d),
            k.reshape(heads), v.reshape(heads),
            h_last[0].reshape(state), h_last[1].reshape(state),
            mem_k.reshape(memkv), mem_v.reshape(memkv),
            ks.reshape(sheads), vs.reshape(sheads),
            hrs.reshape(sstate), his.reshape(sstate))
```

```python
import functools
import math

import jax
import jax.numpy as jnp
from jax import lax
from jax.experimental import pallas as pl
from jax.experimental.pallas import tpu as pltpu

RMS_EPS = 1e-6
NEG_INF = -1e30
SSM_GROUP = 16
ATTN_HEAD_DIM = 64
MOBA_BLOCK = 256
MOBA_TOPK = 3
N_XHEADS = 4
LANES = 128
SUBLANES = 8
VMEM_LIMIT = 56 * 1024 * 1024

F32 = jnp.float32
BF16 = jnp.bfloat16
HIGHEST = lax.Precision.HIGHEST


def _params(*sem):
    return pltpu.CompilerParams(dimension_semantics=sem, vmem_limit_bytes=VMEM_LIMIT)


def _rms(x, g):
    ms = jnp.mean(x * x, axis=-1, keepdims=True)
    return x * lax.rsqrt(ms + RMS_EPS) * g


def _dot(a, b):
    return jnp.dot(a, b, preferred_element_type=F32)


def _dot_t(a, b, precision=None):
    return lax.dot_general(a, b, (((1,), (1,)), ((), ())), preferred_element_type=F32,
                           precision=precision)


def _full(shape):
    n = len(shape)
    return pl.BlockSpec(shape, lambda *_: (0,) * n)


def _const(shape):
    n = len(shape)
    return pl.BlockSpec(shape, lambda *_: (0,) * n, pipeline_mode=pl.Buffered(1))


def _weight_dot(w_dtype):
    if w_dtype == BF16:
        return (lambda v: v.astype(BF16)), _dot
    return (lambda v: v), functools.partial(jnp.dot, precision=HIGHEST, preferred_element_type=F32)


def _ffn_body(x_ref, g_ref, w1_ref, w3_ref, w2_ref, gf_ref, o_ref, *, f_chunk, final_norm):
    lhs, dot = _weight_dot(w1_ref.dtype)
    x = x_ref[...]
    h = lhs(_rms(x, g_ref[...]))
    acc = jnp.zeros(x.shape, F32)
    for c in range(w1_ref.shape[1] // f_chunk):
        cs = slice(c * f_chunk, (c + 1) * f_chunk)
        a = dot(h, w1_ref[:, cs])
        b = dot(h, w3_ref[:, cs])
        act = lhs(a * jax.nn.sigmoid(a) * b)
        acc = acc + dot(act, w2_ref[cs, :])
    y = x + 0.5 * acc
    if final_norm:
        y = _rms(y, gf_ref[...])
    o_ref[...] = y


def _kpass_pages(q_ref, k_refs, lg_ref, ps_ref, *, first, lane0):
    n_heads, dh, page = k_refs[0].shape[2:]
    ppb = MOBA_BLOCK // page
    qcols = q_ref[0]
    lanes = lambda i: slice((i % ppb) * page, (i % ppb + 1) * page)
    for h in range(n_heads):
        qb = jnp.broadcast_to(qcols[:, h:h + 1], (dh, page))
        for i, k_ref in enumerate(k_refs):
            lg_ref[0, i // ppb, h:h + 1, lanes(i)] = jnp.sum(k_ref[0, 0, h] * qb, axis=0, keepdims=True)
    plane = lax.broadcasted_iota(jnp.int32, (n_heads, ps_ref.shape[2]), 1)
    psum = jnp.where(first, 0.0, ps_ref[0])
    for i in range(len(k_refs)):
        psum = jnp.where(plane == lane0 + i, jnp.sum(lg_ref[0, i // ppb, :, lanes(i)], axis=1, keepdims=True), psum)
    ps_ref[0] = psum


def _ffn_kpass_body(pt_ref, x_ref, g_ref, w1_ref, w3_ref, w2_ref, gf_ref, q_ref, *refs,
                    f_chunk, final_norm, pages, first_page, steps_per_seq):
    del pt_ref
    k_refs, (o_ref, lg_ref, ps_ref) = refs[:pages], refs[pages:]
    _ffn_body(x_ref, g_ref, w1_ref, w3_ref, w2_ref, gf_ref, o_ref, f_chunk=f_chunk, final_norm=final_norm)
    part = pl.program_id(0) % steps_per_seq
    _kpass_pages(q_ref, k_refs, lg_ref, ps_ref, first=part == 0, lane0=first_page + part * pages)


def _ffn_kpass(x, g, w1, w3, w2, g_final, page_table, q_t, cache_kt, *, final_norm, tm, first_page, n_call_pages):
    n, d = x.shape
    f = w1.shape[1]
    f_chunk = 256 if f % 256 == 0 else f
    ns, dh, n_heads = q_t.shape
    n_pages = page_table.shape[1]
    page = cache_kt.shape[4]
    ppb = MOBA_BLOCK // page
    steps = n // tm
    steps_per_seq = steps // ns
    pages = n_call_pages // steps_per_seq
    assert steps % ns == 0 and n_call_pages % steps_per_seq == 0 and pages % ppb == 0
    once = pl.Buffered(1)
    const = lambda shape: pl.BlockSpec(shape, lambda i, pt: (0,) * len(shape), pipeline_mode=once)
    k_spec = lambda ip: pl.BlockSpec(
        (1, 1, n_heads, dh, page),
        lambda i, pt: (0, pt[i // steps_per_seq, first_page + (i % steps_per_seq) * pages + ip], 0, 0, 0))
    grid_spec = pltpu.PrefetchScalarGridSpec(
        num_scalar_prefetch=1,
        grid=(steps,),
        in_specs=[pl.BlockSpec((tm, d), lambda i, pt: (i, 0)), const((1, d)), const((d, f)), const((d, f)),
                  const((f, d)), const((1, d)),
                  pl.BlockSpec((1, dh, n_heads), lambda i, pt: (i // steps_per_seq, 0, 0))]
                 + [k_spec(ip) for ip in range(pages)],
        out_specs=[pl.BlockSpec((tm, d), lambda i, pt: (i, 0)),
                   pl.BlockSpec((1, pages // ppb, n_heads, MOBA_BLOCK),
                                lambda i, pt: (i // steps_per_seq, i % steps_per_seq, 0, 0)),
                   pl.BlockSpec((1, n_heads, n_pages), lambda i, pt: (i // steps_per_seq, 0, 0))])
    return pl.pallas_call(
        functools.partial(_ffn_kpass_body, f_chunk=f_chunk, final_norm=final_norm, pages=pages,
                          first_page=first_page, steps_per_seq=steps_per_seq),
        grid_spec=grid_spec,
        out_shape=[jax.ShapeDtypeStruct((n, d), F32),
                   jax.ShapeDtypeStruct((ns, n_call_pages // ppb, n_heads, MOBA_BLOCK), F32),
                   jax.ShapeDtypeStruct((ns, n_heads, n_pages), F32)],
        compiler_params=_params("arbitrary"),
        name="ffn_final_kpass" if final_norm else "ffn_kpass",
    )(page_table, x, g, w1, w3, w2, g_final, q_t, *([cache_kt] * pages))


def _ffn(x, g, w1, w3, w2, g_final, *, final_norm, tm):
    n, d = x.shape
    f = w1.shape[1]
    f_chunk = 256 if f % 256 == 0 else f
    return pl.pallas_call(
        functools.partial(_ffn_body, f_chunk=f_chunk, final_norm=final_norm),
        grid=(n // tm,),
        in_specs=[pl.BlockSpec((tm, d), lambda i: (i, 0)), _const((1, d)), _const((d, f)), _const((d, f)),
                  _const((f, d)), _const((1, d))],
        out_specs=pl.BlockSpec((tm, d), lambda i: (i, 0)),
        out_shape=jax.ShapeDtypeStruct((n, d), F32),
        compiler_params=_params("parallel"),
        name="ffn_final" if final_norm else "ffn",
    )(x, g, w1, w3, w2, g_final)


def _proj_body(x_ref, g_ref, w_ref, u_ref, q_ref, k_ref, v_ref, *, d_ssm, d_attn):
    lhs, dot = _weight_dot(w_ref.dtype)
    p = dot(lhs(_rms(x_ref[...], g_ref[...])), w_ref[...])
    u_ref[...] = p[:, :d_ssm]
    q_ref[...] = p[:, d_ssm:d_ssm + d_attn]
    k_ref[...] = p[:, d_ssm + d_attn:d_ssm + 2 * d_attn]
    v_ref[...] = p[:, d_ssm + 2 * d_attn:]


def _proj(x, g, w_in, *, d_ssm, tm):
    n, d = x.shape
    d_attn = (w_in.shape[1] - d_ssm) // 3
    row = lambda w: pl.BlockSpec((tm, w), lambda i: (i, 0))
    return pl.pallas_call(
        functools.partial(_proj_body, d_ssm=d_ssm, d_attn=d_attn),
        grid=(n // tm,),
        in_specs=[row(d), _const((1, d)), _const(w_in.shape)],
        out_specs=[row(d_ssm), row(d_attn), row(d_attn), row(d_attn)],
        out_shape=[jax.ShapeDtypeStruct((n, d_ssm), F32)] + [jax.ShapeDtypeStruct((n, d_attn), F32)] * 3,
        compiler_params=_params("parallel"),
        name="mix_proj",
    )(x, g, w_in)


AUX_BIAS, AUX_ONE, AUX_BLOCK, AUX_LANES, MAX_BLOCKS = 0, 3, 16, 64, 32
LOG2E = math.log2(math.e)
BOUND_SLACK = 1.02
MAX_SPREAD = 60.0
PAST_UNROLL = 4


def _alibi_slope(h, n_heads):
    return 2.0 ** (-8.0 * (h + 1) / n_heads)


def _bf16_split3(x):
    hi = x.astype(BF16).astype(F32)
    r = x - hi
    mid = r.astype(BF16).astype(F32)
    lo = (r - mid).astype(BF16).astype(F32)
    return hi, mid, lo


def _proj_prompt_body(x_ref, g_ref, w_ref, u_ref, q_ref, kt_ref, vt_ref, kaug_ref, vtb_ref, sums_ref, kn_ref,
                      *, d_ssm, d_attn, n_heads):
    tm = x_ref.shape[1]
    dh = ATTN_HEAD_DIM
    h = _rms(x_ref[0], g_ref[...]).astype(BF16)
    p = _dot(h, w_ref[...])
    u_ref[...] = p[:, :d_ssm]
    q_ref[0] = p[:, d_ssm:d_ssm + d_attn]
    k = p[:, d_ssm + d_attn:d_ssm + 2 * d_attn]
    v = p[:, d_ssm + 2 * d_attn:]
    kt_ref[0] = k.T.reshape(n_heads, dh, tm)
    vt = v.T.reshape(n_heads, dh, tm)
    vt_ref[0] = vt
    for c in range(tm // MOBA_BLOCK):
        cs = slice(c * MOBA_BLOCK, (c + 1) * MOBA_BLOCK)
        vtb_ref[0, :, c] = vt[:, :, cs].astype(BF16)
        sums_ref[0, c:c + 1, :] = jnp.sum(k[cs], axis=0, keepdims=True)
    pos = pl.program_id(1) * tm + lax.broadcasted_iota(jnp.int32, (tm, AUX_LANES), 0)
    lane = lax.broadcasted_iota(jnp.int32, (tm, AUX_LANES), 1)
    block_hot = (lane - AUX_BLOCK == pos // MOBA_BLOCK).astype(F32)
    posf = pos.astype(F32)
    head_of = (lax.broadcasted_iota(jnp.int32, (d_attn, LANES), 0) // dh
               == lax.broadcasted_iota(jnp.int32, (d_attn, LANES), 1))
    kn_ref[0] = jnp.max(_dot((k * k).astype(BF16), head_of.astype(BF16)), axis=0, keepdims=True)
    pieces = []
    for hh in range(n_heads):
        hi, mid, lo = _bf16_split3((_alibi_slope(hh, n_heads) * LOG2E) * posf)
        aux = jnp.where(lane == AUX_BIAS, hi, jnp.where(lane == AUX_BIAS + 1, mid, jnp.where(
            lane == AUX_BIAS + 2, lo, jnp.where(lane < AUX_ONE + 3, 1.0, block_hot))))
        pieces += [k[:, hh * dh:(hh + 1) * dh], aux]
    kaug_ref[0] = jnp.concatenate(pieces, axis=1).astype(BF16)


def _proj_prompt(x, g, w_in, *, d_ssm, tm):
    b, l, d = x.shape
    d_attn = (w_in.shape[1] - d_ssm) // 3
    dh = ATTN_HEAD_DIM
    n_heads = d_attn // dh
    nb = l // MOBA_BLOCK
    tb = tm // MOBA_BLOCK
    assert nb <= MAX_BLOCKS and tm % MOBA_BLOCK == 0
    row = lambda w: pl.BlockSpec((1, tm, w), lambda i, j: (i, j, 0))
    tr = pl.BlockSpec((1, n_heads, dh, tm), lambda i, j: (i, 0, 0, j))
    return pl.pallas_call(
        functools.partial(_proj_prompt_body, d_ssm=d_ssm, d_attn=d_attn, n_heads=n_heads),
        grid=(b, l // tm),
        in_specs=[row(d), _full((1, d)), _full(w_in.shape)],
        out_specs=[pl.BlockSpec((tm, d_ssm), lambda i, j: (j, i)),
                   row(d_attn), tr, tr, row(n_heads * (dh + AUX_LANES)),
                   pl.BlockSpec((1, n_heads, tb, dh, MOBA_BLOCK), lambda i, j: (i, 0, j, 0, 0)),
                   pl.BlockSpec((1, tb, d_attn), lambda i, j: (i * (l // tm) + j, 0, 0)),
                   pl.BlockSpec((1, 1, LANES), lambda i, j: (i * (l // tm) + j, 0, 0))],
        out_shape=[jax.ShapeDtypeStruct((l, b * d_ssm), F32), jax.ShapeDtypeStruct((b, l, d_attn), F32),
                   jax.ShapeDtypeStruct((b, n_heads, dh, l), F32), jax.ShapeDtypeStruct((b, n_heads, dh, l), F32),
                   jax.ShapeDtypeStruct((b, l, n_heads * (dh + AUX_LANES)), BF16),
                   jax.ShapeDtypeStruct((b, n_heads, nb, dh, MOBA_BLOCK), BF16),
                   jax.ShapeDtypeStruct((b * l // tm, tb, d_attn), F32),
                   jax.ShapeDtypeStruct((b * l // tm, 1, LANES), F32)],
        compiler_params=_params("parallel", "parallel"),
        name="mix_proj_prompt",
    )(x, g, w_in)


def _s5_discretise(a_re, a_im, log_dt, b_re, b_im):
    dt = jnp.exp(log_dt)[:, None]
    mag = jnp.exp(a_re * dt)
    abar_re = mag * jnp.cos(a_im * dt)
    abar_im = mag * jnp.sin(a_im * dt)
    den = a_re * a_re + a_im * a_im
    nr = abar_re - 1.0
    f_re = (nr * a_re + abar_im * a_im) / den
    f_im = (abar_im * a_re - nr * a_im) / den
    bbar_re = f_re[..., None] * b_re - f_im[..., None] * b_im
    bbar_im = f_re[..., None] * b_im + f_im[..., None] * b_re
    return abar_re, abar_im, bbar_re, bbar_im


def _s5_matrices(a_re, a_im, log_dt, b_re, b_im, c_re, c_im):
    g, p, c = b_re.shape
    abar_re, abar_im, bbar_re, bbar_im = _s5_discretise(a_re, a_im, log_dt, b_re, b_im)
    eye = jnp.eye(g, dtype=F32)
    bm_re = jnp.einsum('gpc,gh->gchp', bbar_re, eye).reshape(g * c, g * p)
    bm_im = jnp.einsum('gpc,gh->gchp', bbar_im, eye).reshape(g * c, g * p)
    bmat = jnp.concatenate([bm_re, bm_im], axis=1).astype(BF16)
    cm_re = jnp.einsum('gcp,gh->gphc', c_re, eye).reshape(g * p, g * c)
    cm_im = jnp.einsum('gcp,gh->gphc', c_im, eye).reshape(g * p, g * c)
    cmat = jnp.concatenate([cm_re, -cm_im], axis=0).astype(BF16)
    return abar_re.reshape(1, g * p), abar_im.reshape(1, g * p), bmat, cmat


S5_IN_TILES = 8
S5_OUT_TILES = 2


def _s5_compact(bmat, cmat):
    d_ssm, gp2 = bmat.shape
    gp = gp2 // 2
    ci, si = d_ssm // S5_IN_TILES, gp // S5_IN_TILES
    bc = jnp.stack([jnp.concatenate([bmat[c * ci:(c + 1) * ci, c * si:(c + 1) * si],
                                     bmat[c * ci:(c + 1) * ci, gp + c * si:gp + (c + 1) * si]], axis=1)
                    for c in range(S5_IN_TILES)])
    co, so = d_ssm // S5_OUT_TILES, gp // S5_OUT_TILES
    cc = jnp.stack([jnp.stack([cmat[part * gp + o * so:part * gp + (o + 1) * so, o * co:(o + 1) * co]
                               for part in range(2)]) for o in range(S5_OUT_TILES)])
    return bc, cc


def _s5_output(ch, u, d_ref, wg_ref, bg_ref):
    y = ch + d_ref[...] * u
    z = jax.nn.gelu(y, approximate=True)
    return z * jax.nn.sigmoid(_dot(z.astype(BF16), wg_ref[...]) + bg_ref[...])


def _s5_prompt_body(u_ref, ar_ref, ai_ref, bc_ref, cc_ref, d_ref, wg_ref, bg_ref,
                    y_ref, hlast_ref, hbuf, carry, tbuf, *, nbatch):
    gp = ar_ref.shape[1]
    steps = u_ref.shape[0]
    rows = steps * nbatch
    d_ssm = u_ref.shape[1] // nbatch
    tiles = d_ssm // LANES

    @pl.when(pl.program_id(0) == 0)
    def _():
        carry[...] = jnp.zeros(carry.shape, F32)

    for b in range(nbatch):
        for k in range(tiles):
            tbuf[k, pl.ds(b, steps, stride=nbatch), :] = u_ref[:, b * d_ssm + k * LANES:b * d_ssm + (k + 1) * LANES]
    u = jnp.concatenate([tbuf[k] for k in range(tiles)], axis=1)
    ub = u.astype(BF16)
    ci, si = u.shape[1] // S5_IN_TILES, gp // S5_IN_TILES
    for c in range(S5_IN_TILES):
        bu = _dot(ub[:, c * ci:(c + 1) * ci], bc_ref[c])
        hbuf[:, c * si:(c + 1) * si] = bu[:, :si]
        hbuf[:, gp + c * si:gp + (c + 1) * si] = bu[:, si:]

    half = gp // 2
    lower = lax.broadcasted_iota(jnp.int32, (SUBLANES, half), 0) < nbatch
    halves = lambda v: jnp.where(lower, jnp.broadcast_to(v[:, :half], (SUBLANES, half)),
                                 jnp.broadcast_to(v[:, half:], (SUBLANES, half)))
    ar, ai = halves(ar_ref[...]), halves(ai_ref[...])
    swap = lambda v: pltpu.roll(v, nbatch, 0)

    def step(i, st):
        hr, hi = st
        rows8 = pl.ds(pl.multiple_of(i * SUBLANES, SUBLANES), SUBLANES)
        new = []
        for part in range(2):
            a = hbuf[rows8, part * gp:part * gp + half]
            b = hbuf[rows8, part * gp + half:(part + 1) * gp]
            new.append((jnp.where(lower, a, swap(b)), jnp.where(lower, swap(a), b)))
        (x0r, x1r), (x0i, x1i) = new
        n0r = ar * hr - ai * hi + x0r
        n0i = ar * hi + ai * hr + x0i
        n1r = ar * n0r - ai * n0i + x1r
        n1i = ar * n0i + ai * n0r + x1i
        for part, (n0, n1) in enumerate(((n0r, n1r), (n0i, n1i))):
            hbuf[rows8, part * gp:part * gp + half] = jnp.where(lower, n0, swap(n1))
            hbuf[rows8, part * gp + half:(part + 1) * gp] = jnp.where(lower, swap(n0), n1)
        return n1r, n1i

    hr, hi = lax.fori_loop(0, rows // SUBLANES, step, (carry[:, :half], carry[:, half:]))
    carry[:, :half] = hr
    carry[:, half:] = hi
    hlast_ref[:, :half] = hr
    hlast_ref[:, half:] = hi
    so = gp // S5_OUT_TILES
    ch = jnp.concatenate(
        [_dot(hbuf[:, o * so:(o + 1) * so].astype(BF16), cc_ref[o, 0])
         + _dot(hbuf[:, gp + o * so:gp + (o + 1) * so].astype(BF16), cc_ref[o, 1]) for o in range(S5_OUT_TILES)],
        axis=1)
    y = _s5_output(ch, u, d_ref, wg_ref, bg_ref)
    for k in range(tiles):
        tbuf[k] = y[:, k * LANES:(k + 1) * LANES]
    for b in range(nbatch):
        for k in range(tiles):
            y_ref[:, b * d_ssm + k * LANES:b * d_ssm + (k + 1) * LANES] = tbuf[k, pl.ds(b, steps, stride=nbatch), :]


def _s5_prompt(u_tb, nbatch, ar, ai, bmat, cmat, d, w_glu, b_glu, *, t_chunk):
    l, width = u_tb.shape
    d_ssm = width // nbatch
    gp = ar.shape[1]
    bmat, cmat = _s5_compact(bmat, cmat)
    rows = t_chunk * nbatch
    assert 2 * nbatch == SUBLANES and l % t_chunk == 0 and d_ssm % LANES == 0
    return pl.pallas_call(
        functools.partial(_s5_prompt_body, nbatch=nbatch),
        grid=(l // t_chunk,),
        in_specs=[pl.BlockSpec((t_chunk, width), lambda i: (i, 0)), _full((1, gp)), _full((1, gp)),
                  _full(bmat.shape), _full(cmat.shape), _full((1, d_ssm)), _full(w_glu.shape),
                  _full((1, d_ssm))],
        out_specs=[pl.BlockSpec((t_chunk, width), lambda i: (i, 0)), _full((SUBLANES, gp))],
        out_shape=[jax.ShapeDtypeStruct((l, width), F32),
                   jax.ShapeDtypeStruct((SUBLANES, gp), F32)],
        scratch_shapes=[pltpu.VMEM((rows, 2 * gp), F32), pltpu.VMEM((SUBLANES, gp), F32),
                        pltpu.VMEM((d_ssm // LANES, rows, LANES), F32)],
        compiler_params=_params("arbitrary"),
        name="s5_prompt",
    )(u_tb, ar, ai, bmat, cmat, d, w_glu, b_glu)


def _s5_sample_body(u_ref, h0r_ref, h0i_ref, ar_ref, ai_ref, bmat_ref, cmat_ref, d_ref, wg_ref, bg_ref,
                    y_ref, hr_ref, hi_ref):
    gp = ar_ref.shape[1]
    u = u_ref[...]
    bu = _dot(u.astype(BF16), bmat_ref[...])
    ar, ai = ar_ref[...], ai_ref[...]
    h0r, h0i = h0r_ref[...], h0i_ref[...]
    hr = bu[:, :gp] + ar * h0r - ai * h0i
    hi = bu[:, gp:] + ar * h0i + ai * h0r
    hr_ref[...] = hr
    hi_ref[...] = hi
    h_bf = jnp.concatenate([hr, hi], axis=1).astype(BF16)
    y_ref[...] = _s5_output(_dot(h_bf, cmat_ref[...]), u, d_ref, wg_ref, bg_ref)


def _s5_sample(u, h0r, h0i, ar, ai, bmat, cmat, d, w_glu, b_glu):
    n, d_ssm = u.shape
    gp = ar.shape[1]
    args = (u, h0r, h0i, ar, ai, bmat, cmat, d, w_glu, b_glu)
    return pl.pallas_call(
        _s5_sample_body,
        grid=(1,),
        in_specs=[_full(a.shape) for a in args],
        out_specs=[_full((n, d_ssm)), _full((n, gp)), _full((n, gp))],
        out_shape=[jax.ShapeDtypeStruct((n, d_ssm), F32), jax.ShapeDtypeStruct((n, gp), F32),
                   jax.ShapeDtypeStruct((n, gp), F32)],
        compiler_params=_params("arbitrary"),
        name="s5_sample",
    )(*args)


def _top_blocks(gate, brow, n_valid):
    bf = brow.astype(F32)
    g = jnp.where(brow < n_valid, gate, NEG_INF)
    sel = jnp.zeros(gate.shape, jnp.bool_)
    for _ in range(MOBA_TOPK):
        mx = jnp.max(g, axis=0, keepdims=True)
        first = jnp.min(jnp.where(g == mx, bf, float(gate.shape[0])), axis=0, keepdims=True)
        pick = bf == first
        sel = jnp.logical_or(sel, pick)
        g = jnp.where(pick, -jnp.inf, g)
    return jnp.logical_and(sel, brow < n_valid)


def _moba_prompt_body(q_ref, kaug_ref, vtb_ref, sums_ref, kn_ref, o_ref, qaug, acc, s_scr, p_scr, *, n_heads):
    j = pl.program_id(1)
    blk = MOBA_BLOCK
    dh = ATTN_HEAD_DIM
    ka = dh + AUX_LANES
    nb = sums_ref.shape[1]
    t0 = pl.multiple_of(j * blk, blk)
    scale = dh ** -0.5
    qt = q_ref[0].T
    brow = lax.broadcasted_iota(jnp.int32, (MAX_BLOCKS, blk), 0)
    crow = lax.broadcasted_iota(jnp.int32, (AUX_BLOCK, blk), 0)
    t0f = jnp.broadcast_to(t0.astype(F32), (AUX_BLOCK, blk))
    kmax2 = jnp.max(kn_ref[0], axis=0, keepdims=True)
    hlane = lax.broadcasted_iota(jnp.int32, kmax2.shape, 1)
    tq = lax.broadcasted_iota(jnp.int32, (1, blk), 1).astype(F32)
    shifts, spreads = [], []
    for h in range(n_heads):
        slope = _alibi_slope(h, n_heads)
        qth = qt[h * dh:(h + 1) * dh]
        means = sums_ref[0, :, h * dh:(h + 1) * dh] * (1.0 / blk)
        gate = jnp.dot(means, qth, precision=HIGHEST, preferred_element_type=F32)
        if nb < MAX_BLOCKS:
            gate = jnp.concatenate([gate, jnp.full((MAX_BLOCKS - nb, blk), NEG_INF, F32)], axis=0)
        keep = jnp.logical_or(_top_blocks(gate, brow, j), brow == j)
        mask = jnp.where(keep, 0.0, NEG_INF)
        hi, mid, lo = _bf16_split3((-slope * LOG2E) * t0f)
        const = jnp.where(crow < AUX_ONE, 1.0, jnp.where(crow == AUX_ONE, hi, jnp.where(
            crow == AUX_ONE + 1, mid, jnp.where(crow == AUX_ONE + 2, lo, 0.0))))
        rest = jnp.zeros((AUX_LANES - AUX_BLOCK - MAX_BLOCKS, blk), F32)
        qaug[h] = jnp.concatenate([qth * (scale * LOG2E), const, mask, rest], axis=0).astype(BF16)
        k2 = jnp.max(jnp.where(hlane == h, kmax2, 0.0), axis=1, keepdims=True)
        bound = jnp.sqrt(jnp.sum(qth * qth, axis=0, keepdims=True) * k2) * (scale * BOUND_SLACK)
        shifts.append((bound + slope * tq) * LOG2E)
        spreads.append(bound)
    shift = jnp.concatenate(shifts, axis=0)
    safe = 2.0 * jnp.max(jnp.concatenate(spreads, axis=0)) < MAX_SPREAD

    def scores(n, h):
        r = pl.multiple_of(n * blk, blk)
        return _dot(kaug_ref[0, pl.ds(r, blk), h * ka:(h + 1) * ka], qaug[h])

    causal = (lax.broadcasted_iota(jnp.int32, (blk, blk), 0) <= lax.broadcasted_iota(jnp.int32, (blk, blk), 1))

    def finish(l_all):
        out_t = jnp.concatenate([acc[h * dh:(h + 1) * dh, :] / l_all[h:h + 1] for h in range(n_heads)], axis=0)
        o_ref[0] = out_t.T

    @pl.when(safe)
    def _():
        def consume(n, l_all, get_scores, own=False):
            ls = []
            for h in range(n_heads):
                s = get_scores(h)
                if own:
                    s = jnp.where(causal, s, NEG_INF)
                p = jnp.exp2(s - shift[h:h + 1])
                p_scr[h] = p.astype(BF16)
                ls.append(jnp.sum(p, axis=0, keepdims=True))
            for h in range(n_heads):
                hs = slice(h * dh, (h + 1) * dh)
                pv = _dot(vtb_ref[0, h, n], p_scr[h])
                acc[hs, :] = pv if own else acc[hs, :] + pv
            l_new = jnp.concatenate(ls, axis=0)
            return l_new if own else l_all + l_new

        l_all = consume(j, None, lambda h: scores(j, h), own=True)

        def group(i, l_all):
            for r in range(PAST_UNROLL):
                n = i * PAST_UNROLL + r
                l_all = consume(n, l_all, lambda h: scores(n, h))
            return l_all

        done = (j // PAST_UNROLL) * PAST_UNROLL
        l_all = lax.fori_loop(0, j // PAST_UNROLL, group, l_all)
        finish(lax.fori_loop(done, j, lambda n, l_all: consume(n, l_all, lambda h: scores(n, h)), l_all))

    @pl.when(jnp.logical_not(safe))
    def _():
        def tile(n, st, own):
            ms, ls, alphas = [], [], []
            for h in range(n_heads):
                s_scr[h] = scores(n, h)
            for h in range(n_heads):
                s = s_scr[h]
                if own:
                    s = jnp.where(causal, s, NEG_INF)
                    m_new = jnp.max(s, axis=0, keepdims=True)
                else:
                    m_new = jnp.maximum(st[0][h:h + 1], jnp.max(s, axis=0, keepdims=True))
                    alphas.append(jnp.exp2(st[0][h:h + 1] - m_new))
                p = jnp.exp2(s - m_new)
                p_scr[h] = p.astype(BF16)
                ms.append(m_new)
                l_new = jnp.sum(p, axis=0, keepdims=True)
                ls.append(l_new if own else alphas[h] * st[1][h:h + 1] + l_new)
            for h in range(n_heads):
                hs = slice(h * dh, (h + 1) * dh)
                pv = _dot(vtb_ref[0, h, n], p_scr[h])
                acc[hs, :] = pv if own else alphas[h] * acc[hs, :] + pv
            return jnp.concatenate(ms, axis=0), jnp.concatenate(ls, axis=0)

        finish(lax.fori_loop(0, j, lambda n, st: tile(n, st, False), tile(j, None, True))[1])


def _moba_prompt(q, kaug, vtb, sums, knorm):
    b, l, d = q.shape
    nb = l // MOBA_BLOCK
    n_heads = d // ATTN_HEAD_DIM
    ka = ATTN_HEAD_DIM + AUX_LANES
    once = pl.Buffered(1)
    return pl.pallas_call(
        functools.partial(_moba_prompt_body, n_heads=n_heads),
        grid=(b, nb),
        in_specs=[pl.BlockSpec((1, MOBA_BLOCK, d), lambda i, j: (i, j, 0)),
                  pl.BlockSpec((1, l, n_heads * ka), lambda i, j: (i, 0, 0), pipeline_mode=once),
                  pl.BlockSpec((1, n_heads, nb, ATTN_HEAD_DIM, MOBA_BLOCK), lambda i, j: (i, 0, 0, 0, 0),
                               pipeline_mode=once),
                  pl.BlockSpec((1, nb, d), lambda i, j: (i, 0, 0)),
                  pl.BlockSpec((1,) + knorm.shape[1:], lambda i, j: (i, 0, 0))],
        out_specs=pl.BlockSpec((1, MOBA_BLOCK, d), lambda i, j: (i, j, 0)),
        out_shape=jax.ShapeDtypeStruct((b, l, d), F32),
        scratch_shapes=[pltpu.VMEM((n_heads, ka, MOBA_BLOCK), BF16), pltpu.VMEM((d, MOBA_BLOCK), F32),
                        pltpu.VMEM((n_heads, MOBA_BLOCK, MOBA_BLOCK), F32),
                        pltpu.VMEM((n_heads, MOBA_BLOCK, MOBA_BLOCK), BF16)],
        compiler_params=_params("parallel", "arbitrary"),
        name="moba_prompt",
    )(q, kaug, vtb, sums, knorm)


def _moba_select_body(ps_ref, idx_ref, *, pages_per_block, n_blocks):
    ps = ps_ref[...]
    lanes = ps.shape[1]
    lane = lax.broadcasted_iota(jnp.int32, (1, lanes), 1)
    g = ps
    for i in range(1, pages_per_block):
        g = g + pltpu.roll(ps, lanes - i, 1)
    g = g * (1.0 / MOBA_BLOCK)
    is_block = jnp.logical_and(lane % pages_per_block == 0, lane < n_blocks * pages_per_block)
    g = jnp.where(is_block, g, -jnp.inf)
    lanef = lane.astype(F32)
    out = jnp.zeros(idx_ref.shape, F32)
    olane = lax.broadcasted_iota(jnp.int32, idx_ref.shape, 1)
    for r in range(MOBA_TOPK):
        mx = jnp.max(g, axis=1, keepdims=True)
        first = jnp.min(jnp.where(g == mx, lanef, float(lanes)), axis=1, keepdims=True)
        out = jnp.where(olane == r, first * (1.0 / pages_per_block), out)
        g = jnp.where(lanef == first, -jnp.inf, g)
    idx_ref[...] = out.astype(jnp.int32)


def _moba_select(page_sums2, *, pages_per_block, n_blocks):
    rows, lanes = page_sums2.shape
    return pl.pallas_call(
        functools.partial(_moba_select_body, pages_per_block=pages_per_block, n_blocks=n_blocks),
        grid=(1,),
        in_specs=[_full((rows, lanes))],
        out_specs=_full((rows, LANES)),
        out_shape=jax.ShapeDtypeStruct((rows, LANES), jnp.int32),
        compiler_params=_params("arbitrary"),
        name="moba_sample_select",
    )(page_sums2)


def _moba_attend_body(blk_ref, pg_ref, q_ref, kn_ref, vn_ref, *refs, pages_per_block, n_heads, past_len):
    del pg_ref
    n_lg = MOBA_TOPK
    n_v = MOBA_TOPK * pages_per_block
    lg_refs, v_refs, o_ref = refs[:n_heads * n_lg], refs[n_heads * n_lg:n_heads * (n_lg + n_v)], refs[-1]
    b = pl.program_id(0)
    dh = ATTN_HEAD_DIM
    blk = MOBA_BLOCK
    page = blk // pages_per_block
    scale = dh ** -0.5
    lane = lax.broadcasted_iota(jnp.int32, (1, blk), 1)
    qk = q_ref[0] * kn_ref[0]
    outs = []
    for h in range(n_heads):
        hs = slice(h * dh, (h + 1) * dh)
        slope = _alibi_slope(h, n_heads)
        s_own = jnp.sum(qk[:, hs], axis=1, keepdims=True) * scale
        ss = []
        for i in range(n_lg):
            pos = blk_ref[b, h, i] * blk + lane
            ss.append(lg_refs[h * n_lg + i][0, 0, h:h + 1, :] * scale - slope * (past_len - pos).astype(F32))
        m = s_own
        for s in ss:
            m = jnp.maximum(m, jnp.max(s, axis=1, keepdims=True))
        p_own = jnp.exp(s_own - m)
        l = p_own
        acc = p_own * vn_ref[0][:, hs]
        for i in range(n_lg):
            p = jnp.exp(ss[i] - m)
            l = l + jnp.sum(p, axis=1, keepdims=True)
            for c in range(pages_per_block):
                vt = v_refs[h * n_v + i * pages_per_block + c][0, 0, 0].astype(BF16)
                acc = acc + _dot_t(p[:, c * page:(c + 1) * page].astype(BF16), vt)
        outs.append(acc / l)
    o_ref[0] = jnp.concatenate(outs, axis=1)


def _moba_attend(blocks, pages_idx, q3, kn3, vn3, logits, cache_vt, *, pages_per_block, past_len):
    n, _, d = q3.shape
    _, _, n_heads, dh, page = cache_vt.shape
    row = pl.BlockSpec((1, 1, d), lambda b, bl, pg: (b, 0, 0))
    lg_spec = lambda h, i: pl.BlockSpec(
        (1, 1, n_heads, MOBA_BLOCK), lambda b, bl, pg: (b, bl[b, h, i], 0, 0))
    v_spec = lambda h, i: pl.BlockSpec(
        (1, 1, 1, dh, page), lambda b, bl, pg: (0, pg[b, h, i], h, 0, 0))
    n_v = MOBA_TOPK * pages_per_block
    lg_specs = [lg_spec(h, i) for h in range(n_heads) for i in range(MOBA_TOPK)]
    v_specs = [v_spec(h, i) for h in range(n_heads) for i in range(n_v)]
    grid_spec = pltpu.PrefetchScalarGridSpec(
        num_scalar_prefetch=2,
        grid=(n,),
        in_specs=[row, row, row] + lg_specs + v_specs,
        out_specs=row)
    return pl.pallas_call(
        functools.partial(_moba_attend_body, pages_per_block=pages_per_block, n_heads=n_heads,
                          past_len=past_len),
        grid_spec=grid_spec,
        out_shape=jax.ShapeDtypeStruct((n, 1, d), F32),
        compiler_params=_params("parallel"),
        name="moba_sample_attend",
    )(blocks, pages_idx, q3, kn3, vn3, *([logits] * len(lg_specs)), *([cache_vt] * len(v_specs)))


def _moba_sample(q, k_new, v_new, logits, page_sums, cache_vt, page_table):
    n, d = q.shape
    _, _, n_heads, dh, page = cache_vt.shape
    n_pages = page_table.shape[1]
    past_len = n_pages * page
    pages_per_block = MOBA_BLOCK // page
    n_blocks = n_pages // pages_per_block
    assert n_blocks >= MOBA_TOPK and n_pages % LANES == 0
    q3 = q.reshape(n, 1, d)
    blocks = _moba_select(page_sums.reshape(n * n_heads, n_pages), pages_per_block=pages_per_block,
                          n_blocks=n_blocks)[:, :MOBA_TOPK].reshape(n, n_heads, MOBA_TOPK)
    logical = (blocks[..., None] * pages_per_block + jnp.arange(pages_per_block)).reshape(n, n_heads, -1)
    pages_idx = jnp.take_along_axis(page_table[:, None, :], logical, axis=2)
    y = _moba_attend(blocks, pages_idx, q3, k_new.reshape(n, 1, d), v_new.reshape(n, 1, d),
                     logits, cache_vt,
                     pages_per_block=pages_per_block, past_len=past_len)
    return y.reshape(n, d)


def _memkv_body(m_ref, g_ref, wk_ref, wv_ref, k_ref, v_ref):
    h = _rms(m_ref[...], g_ref[...]).astype(BF16)
    k_ref[...] = _dot(h, wk_ref[...])
    v_ref[...] = _dot(h, wv_ref[...])


def _memkv(mem, g, wk, wv, *, tm):
    n, d = mem.shape
    row = pl.BlockSpec((tm, d), lambda i: (i, 0))
    return pl.pallas_call(
        _memkv_body,
        grid=(n // tm,),
        in_specs=[row, _full((1, d)), _full(wk.shape), _full(wv.shape)],
        out_specs=[row, row],
        out_shape=[jax.ShapeDtypeStruct((n, d), F32)] * 2,
        compiler_params=_params("parallel"),
        name="mem_kv",
    )(mem, g, wk, wv)


def _mix_out(x, ys, ya, gs, ga, w_out_ref, d_ssm):
    ysn = _rms(ys, gs).astype(BF16)
    yan = _rms(ya, ga).astype(BF16)
    return x + _dot(ysn, w_out_ref[:d_ssm, :]) + _dot(yan, w_out_ref[d_ssm:, :])


def _xattn_heads(q, mk_ref, mv_ref):
    d = q.shape[1]
    xd = d // N_XHEADS
    outs = []
    for h in range(N_XHEADS):
        hs = slice(h * xd, (h + 1) * xd)
        s = _dot_t(q[:, hs].astype(BF16), mk_ref[0, :, hs].astype(BF16)) * (xd ** -0.5)
        m = jnp.max(s, axis=1, keepdims=True)
        p = jnp.exp(s - m)
        l = jnp.sum(p, axis=1, keepdims=True)
        outs.append(_dot(p.astype(BF16), mv_ref[0, :, hs].astype(BF16)) / l)
    return jnp.concatenate(outs, axis=1)


def _merge_prompt_body(x_ref, ys_ref, ya_ref, gs_ref, ga_ref, wout_ref, gx_ref, wq_ref, mk_ref, mv_ref,
                       wo_ref, o_ref, *, d_ssm):
    x2 = _mix_out(x_ref[0], ys_ref[...], ya_ref[0], gs_ref[...], ga_ref[...], wout_ref, d_ssm)
    q = _dot(_rms(x2, gx_ref[...]).astype(BF16), wq_ref[...])
    o = _xattn_heads(q, mk_ref, mv_ref)
    o_ref[0] = x2 + _dot(o.astype(BF16), wo_ref[...])


def _merge_prompt(x, ys, ya, gs, ga, w_out, gx, wq, mk, mv, wo, *, tm):
    b, l, d = x.shape
    d_ssm = ys.shape[1] // b
    d_attn = ya.shape[2]
    n_mem = mk.shape[1]
    row = lambda w: pl.BlockSpec((1, tm, w), lambda i, j: (i, j, 0))
    mem = pl.BlockSpec((1, n_mem, d), lambda i, j: (i, 0, 0))
    return pl.pallas_call(
        functools.partial(_merge_prompt_body, d_ssm=d_ssm),
        grid=(b, l // tm),
        in_specs=[row(d), pl.BlockSpec((tm, d_ssm), lambda i, j: (j, i)), row(d_attn), _full((1, d_ssm)),
                  _full((1, d_attn)), _full(w_out.shape), _full((1, d)), _full(wq.shape), mem, mem,
                  _full(wo.shape)],
        out_specs=row(d),
        out_shape=jax.ShapeDtypeStruct((b, l, d), F32),
        compiler_params=_params("parallel", "parallel"),
        name="merge_prompt",
    )(x, ys, ya, gs, ga, w_out, gx, wq, mk, mv, wo)


def _merge_pre_body(x_ref, ys_ref, ya_ref, gs_ref, ga_ref, wout_ref, gx_ref, wq_ref, x2_ref, q_ref, *, d_ssm):
    x2 = _mix_out(x_ref[...], ys_ref[...], ya_ref[...], gs_ref[...], ga_ref[...], wout_ref, d_ssm)
    x2_ref[...] = x2
    q_ref[...] = _dot(_rms(x2, gx_ref[...]).astype(BF16), wq_ref[...])


def _merge_pre(x, ys, ya, gs, ga, w_out, gx, wq):
    n, d = x.shape
    args = (x, ys, ya, gs, ga, w_out, gx, wq)
    return pl.pallas_call(
        functools.partial(_merge_pre_body, d_ssm=ys.shape[1]),
        grid=(1,),
        in_specs=[_full(a.shape) for a in args],
        out_specs=[_full((n, d)), _full((n, d))],
        out_shape=[jax.ShapeDtypeStruct((n, d), F32)] * 2,
        compiler_params=_params("arbitrary"),
        name="merge_sample_pre",
    )(*args)


def _xattn_sample_body(q_ref, mk_ref, mv_ref, o_ref):
    o_ref[0] = _xattn_heads(q_ref[0], mk_ref, mv_ref)


def _xattn_sample(q3, mk, mv):
    n, _, d = q3.shape
    n_mem = mk.shape[1]
    row = pl.BlockSpec((1, 1, d), lambda i: (i, 0, 0))
    mem = pl.BlockSpec((1, n_mem, d), lambda i: (i, 0, 0))
    return pl.pallas_call(
        _xattn_sample_body,
        grid=(n,),
        in_specs=[row, mem, mem],
        out_specs=row,
        out_shape=jax.ShapeDtypeStruct((n, 1, d), F32),
        compiler_params=_params("parallel"),
        name="xattn_sample",
    )(q3, mk, mv)


def _merge_post_body(x_ref, o_ref, wo_ref, y_ref):
    y_ref[...] = x_ref[...] + _dot(o_ref[...].astype(BF16), wo_ref[...])


def _merge_post(x2, o, wo):
    n, d = x2.shape
    return pl.pallas_call(
        _merge_post_body,
        grid=(1,),
        in_specs=[_full((n, d)), _full((n, d)), _full(wo.shape)],
        out_specs=_full((n, d)),
        out_shape=jax.ShapeDtypeStruct((n, d), F32),
        compiler_params=_params("arbitrary"),
        name="merge_sample_post",
    )(x2, o, wo)


def kernel(x_prompt, x_sample, mem_prompt, cache_k, cache_v, page_table, state_ssm_re, state_ssm_im, cache_mem_k, cache_mem_v, g_ffn1, w1_ffn1, w3_ffn1, w2_ffn1, g_mix, w_in, ssm_a_re, ssm_a_im, ssm_log_dt, ssm_b_re, ssm_b_im, ssm_c_re, ssm_c_im, ssm_d, w_glu, b_glu, g_out_ssm, g_out_attn, w_out, g_xattn, g_mem, wq_x, wk_x, wv_x, wo_x, g_ffn2, w1_ffn2, w3_ffn2, w2_ffn2, g_final):
    depth = g_ffn1.shape[0]
    assert depth == 1
    b, l, d = x_prompt.shape
    ns, ls, _ = x_sample.shape
    assert ls == 1
    n_groups, n_state = ssm_a_re.shape[1:]
    gp = n_groups * n_state
    d_ssm = n_groups * SSM_GROUP
    d_attn = (w_in.shape[2] - d_ssm) // 3
    n_heads = d_attn // ATTN_HEAD_DIM
    n_mem = mem_prompt.shape[1]
    xd = d // N_XHEADS

    vec = lambda a: a[0].reshape(1, -1)
    wb = lambda a: a[0].astype(BF16)
    w1a, w3a, w2a = wb(w1_ffn1), wb(w3_ffn1), wb(w2_ffn1)
    w1b, w3b, w2b = wb(w1_ffn2), wb(w3_ffn2), wb(w2_ffn2)
    w_in_b, w_out_b, w_glu_b = wb(w_in), wb(w_out), wb(w_glu)
    wq_b, wk_b, wv_b, wo_b = wb(wq_x), wb(wk_x), wb(wv_x), wb(wo_x)
    gfin = g_final.reshape(1, -1)
    ar, ai, bmat, cmat = _s5_matrices(ssm_a_re[0], ssm_a_im[0], ssm_log_dt[0], ssm_b_re[0], ssm_b_im[0],
                                      ssm_c_re[0], ssm_c_im[0])
    s5_w = (ar, ai, bmat, cmat, vec(ssm_d), w_glu_b, vec(b_glu))

    xs = x_sample.reshape(ns, d)
    xs1 = _ffn(xs, vec(g_ffn1), w1_ffn1[0], w3_ffn1[0], w2_ffn1[0], gfin, final_norm=False, tm=ns)
    us, qs, ks, vs = _proj(xs1, vec(g_mix), w_in[0], d_ssm=d_ssm, tm=ns)
    q_t = qs.reshape(ns, n_heads, ATTN_HEAD_DIM).swapaxes(1, 2)
    cache_kt = cache_k.transpose(0, 1, 3, 4, 2)
    cache_vt = cache_v.transpose(0, 1, 3, 4, 2)
    half_pages = page_table.shape[1] // 2

    tm = 512 if l % 512 == 0 else l
    xp = x_prompt.reshape(b * l, d)
    mem_k, mem_v = _memkv(mem_prompt.reshape(b * n_mem, d), vec(g_mem), wk_b, wv_b,
                          tm=min(256, b * n_mem))
    x1, lg_a, ps_a = _ffn_kpass(xp, vec(g_ffn1), w1a, w3a, w2a, gfin, page_table, q_t, cache_kt,
                                final_norm=False, tm=tm, first_page=0, n_call_pages=half_pages)
    x1 = x1.reshape(b, l, d)
    u, q, k_t, v_t, kaug, vtb, sums, knorm = _proj_prompt(x1, vec(g_mix), w_in_b, d_ssm=d_ssm, tm=tm)
    t_chunk = 128 if l % 128 == 0 else l
    ys, h_last = _s5_prompt(u, b, *s5_w, t_chunk=t_chunk)
    ya = _moba_prompt(q, kaug, vtb, sums.reshape(b, l // MOBA_BLOCK, d_attn), knorm.reshape(b, l // tm, LANES))
    x3 = _merge_prompt(x1, ys, ya, vec(g_out_ssm), vec(g_out_attn), w_out_b, vec(g_xattn),
                       wq_b, mem_k.reshape(b, n_mem, d), mem_v.reshape(b, n_mem, d), wo_b, tm=tm)
    y_prompt, lg_b, ps_b = _ffn_kpass(x3.reshape(b * l, d), vec(g_ffn2), w1b, w3b, w2b, gfin, page_table, q_t,
                                      cache_kt, final_norm=True, tm=tm, first_page=half_pages,
                                      n_call_pages=half_pages)
    h_last = h_last.reshape(2, b, 2, n_groups // 2, n_state).transpose(2, 1, 0, 3, 4).reshape(2, b, gp)
    k = k_t.transpose(0, 3, 1, 2)
    v = v_t.transpose(0, 3, 1, 2)
    heads = (1, b, l, n_heads, ATTN_HEAD_DIM)
    state = (1, b, n_groups, n_state)
    memkv = (1, b, n_mem, N_XHEADS, xd)

    yss, hrs, his = _s5_sample(us, state_ssm_re[0].reshape(ns, gp), state_ssm_im[0].reshape(ns, gp), *s5_w)
    yas = _moba_sample(qs, ks, vs, jnp.concatenate([lg_a, lg_b], axis=1), ps_a + ps_b, cache_vt, page_table)
    xs2, qx = _merge_pre(xs1, yss, yas, vec(g_out_ssm), vec(g_out_attn), w_out_b, vec(g_xattn), wq_b)
    ox = _xattn_sample(qx.reshape(ns, 1, d), cache_mem_k[0].reshape(ns, n_mem, d),
                       cache_mem_v[0].reshape(ns, n_mem, d))
    xs3 = _merge_post(xs2, ox.reshape(ns, d), wo_b)
    y_sample = _ffn(xs3, vec(g_ffn2), w1b, w3b, w2b, gfin, final_norm=True, tm=ns)
    sheads = (1, ns, 1, n_heads, ATTN_HEAD_DIM)
    sstate = (1, ns, n_groups, n_state)

    return (y_prompt.reshape(b, l, d), y_sample.reshape(ns, 1, d),
            k.reshape(heads), v.reshape(heads),
            h_last[0].reshape(state), h_last[1].reshape(state),
            mem_k.reshape(memkv), mem_v.reshape(memkv),
            ks.reshape(sheads), vs.reshape(sheads),
            hrs.reshape(sstate), his.reshape(sstate))
```

```python
import functools
import math

import jax
import jax.numpy as jnp
from jax import lax
from jax.experimental import pallas as pl
from jax.experimental.pallas import tpu as pltpu

RMS_EPS = 1e-6
NEG_INF = -1e30
SSM_GROUP = 16
ATTN_HEAD_DIM = 64
MOBA_BLOCK = 256
MOBA_TOPK = 3
N_XHEADS = 4
LANES = 128
SUBLANES = 8
VMEM_LIMIT = 56 * 1024 * 1024

F32 = jnp.float32
BF16 = jnp.bfloat16
HIGHEST = lax.Precision.HIGHEST


def _params(*sem):
    return pltpu.CompilerParams(dimension_semantics=sem, vmem_limit_bytes=VMEM_LIMIT)


def _rms(x, g):
    ms = jnp.mean(x * x, axis=-1, keepdims=True)
    return x * lax.rsqrt(ms + RMS_EPS) * g


def _dot(a, b):
    return jnp.dot(a, b, preferred_element_type=F32)


def _dot_t(a, b, precision=None):
    return lax.dot_general(a, b, (((1,), (1,)), ((), ())), preferred_element_type=F32,
                           precision=precision)


def _full(shape):
    n = len(shape)
    return pl.BlockSpec(shape, lambda *_: (0,) * n)


def _const(shape):
    n = len(shape)
    return pl.BlockSpec(shape, lambda *_: (0,) * n, pipeline_mode=pl.Buffered(1))


def _weight_dot(w_dtype):
    if w_dtype == BF16:
        return (lambda v: v.astype(BF16)), _dot
    return (lambda v: v), functools.partial(jnp.dot, precision=HIGHEST, preferred_element_type=F32)


def _ffn_body(x_ref, g_ref, w1_ref, w3_ref, w2_ref, gf_ref, o_ref, *, f_chunk, final_norm):
    lhs, dot = _weight_dot(w1_ref.dtype)
    x = x_ref[...]
    h = lhs(_rms(x, g_ref[...]))
    acc = jnp.zeros(x.shape, F32)
    for c in range(w1_ref.shape[1] // f_chunk):
        cs = slice(c * f_chunk, (c + 1) * f_chunk)
        a = dot(h, w1_ref[:, cs])
        b = dot(h, w3_ref[:, cs])
        act = lhs(a * jax.nn.sigmoid(a) * b)
        acc = acc + dot(act, w2_ref[cs, :])
    y = x + 0.5 * acc
    if final_norm:
        y = _rms(y, gf_ref[...])
    o_ref[...] = y


def _kpass_pages(q_ref, k_refs, lg_ref, ps_ref, *, first, lane0):
    n_heads, dh, page = k_refs[0].shape[2:]
    ppb = MOBA_BLOCK // page
    qcols = q_ref[0]
    lanes = lambda i: slice((i % ppb) * page, (i % ppb + 1) * page)
    for h in range(n_heads):
        qb = jnp.broadcast_to(qcols[:, h:h + 1], (dh, page))
        for i, k_ref in enumerate(k_refs):
            lg_ref[0, i // ppb, h:h + 1, lanes(i)] = jnp.sum(k_ref[0, 0, h] * qb, axis=0, keepdims=True)
    plane = lax.broadcasted_iota(jnp.int32, (n_heads, ps_ref.shape[2]), 1)
    psum = jnp.where(first, 0.0, ps_ref[0])
    for i in range(len(k_refs)):
        psum = jnp.where(plane == lane0 + i, jnp.sum(lg_ref[0, i // ppb, :, lanes(i)], axis=1, keepdims=True), psum)
    ps_ref[0] = psum


def _ffn_kpass_body(pt_ref, x_ref, g_ref, w1_ref, w3_ref, w2_ref, gf_ref, q_ref, *refs,
                    f_chunk, final_norm, pages, first_page, steps_per_seq):
    del pt_ref
    k_refs, (o_ref, lg_ref, ps_ref) = refs[:pages], refs[pages:]
    _ffn_body(x_ref, g_ref, w1_ref, w3_ref, w2_ref, gf_ref, o_ref, f_chunk=f_chunk, final_norm=final_norm)
    part = pl.program_id(0) % steps_per_seq
    _kpass_pages(q_ref, k_refs, lg_ref, ps_ref, first=part == 0, lane0=first_page + part * pages)


def _ffn_kpass(x, g, w1, w3, w2, g_final, page_table, q_t, cache_kt, *, final_norm, tm, first_page, n_call_pages):
    n, d = x.shape
    f = w1.shape[1]
    f_chunk = 256 if f % 256 == 0 else f
    ns, dh, n_heads = q_t.shape
    n_pages = page_table.shape[1]
    page = cache_kt.shape[4]
    ppb = MOBA_BLOCK // page
    steps = n // tm
    steps_per_seq = steps // ns
    pages = n_call_pages // steps_per_seq
    assert steps % ns == 0 and n_call_pages % steps_per_seq == 0 and pages % ppb == 0
    once = pl.Buffered(1)
    const = lambda shape: pl.BlockSpec(shape, lambda i, pt: (0,) * len(shape), pipeline_mode=once)
    k_spec = lambda ip: pl.BlockSpec(
        (1, 1, n_heads, dh, page),
        lambda i, pt: (0, pt[i // steps_per_seq, first_page + (i % steps_per_seq) * pages + ip], 0, 0, 0))
    grid_spec = pltpu.PrefetchScalarGridSpec(
        num_scalar_prefetch=1,
        grid=(steps,),
        in_specs=[pl.BlockSpec((tm, d), lambda i, pt: (i, 0)), const((1, d)), const((d, f)), const((d, f)),
                  const((f, d)), const((1, d)),
                  pl.BlockSpec((1, dh, n_heads), lambda i, pt: (i // steps_per_seq, 0, 0))]
                 + [k_spec(ip) for ip in range(pages)],
        out_specs=[pl.BlockSpec((tm, d), lambda i, pt: (i, 0)),
                   pl.BlockSpec((1, pages // ppb, n_heads, MOBA_BLOCK),
                                lambda i, pt: (i // steps_per_seq, i % steps_per_seq, 0, 0)),
                   pl.BlockSpec((1, n_heads, n_pages), lambda i, pt: (i // steps_per_seq, 0, 0))])
    return pl.pallas_call(
        functools.partial(_ffn_kpass_body, f_chunk=f_chunk, final_norm=final_norm, pages=pages,
                          first_page=first_page, steps_per_seq=steps_per_seq),
        grid_spec=grid_spec,
        out_shape=[jax.ShapeDtypeStruct((n, d), F32),
                   jax.ShapeDtypeStruct((ns, n_call_pages // ppb, n_heads, MOBA_BLOCK), F32),
                   jax.ShapeDtypeStruct((ns, n_heads, n_pages), F32)],
        compiler_params=_params("arbitrary"),
        name="ffn_final_kpass" if final_norm else "ffn_kpass",
    )(page_table, x, g, w1, w3, w2, g_final, q_t, *([cache_kt] * pages))


def _ffn(x, g, w1, w3, w2, g_final, *, final_norm, tm):
    n, d = x.shape
    f = w1.shape[1]
    f_chunk = 256 if f % 256 == 0 else f
    return pl.pallas_call(
        functools.partial(_ffn_body, f_chunk=f_chunk, final_norm=final_norm),
        grid=(n // tm,),
        in_specs=[pl.BlockSpec((tm, d), lambda i: (i, 0)), _const((1, d)), _const((d, f)), _const((d, f)),
                  _const((f, d)), _const((1, d))],
        out_specs=pl.BlockSpec((tm, d), lambda i: (i, 0)),
        out_shape=jax.ShapeDtypeStruct((n, d), F32),
        compiler_params=_params("parallel"),
        name="ffn_final" if final_norm else "ffn",
    )(x, g, w1, w3, w2, g_final)


def _proj_body(x_ref, g_ref, w_ref, u_ref, q_ref, k_ref, v_ref, *, d_ssm, d_attn):
    lhs, dot = _weight_dot(w_ref.dtype)
    p = dot(lhs(_rms(x_ref[...], g_ref[...])), w_ref[...])
    u_ref[...] = p[:, :d_ssm]
    q_ref[...] = p[:, d_ssm:d_ssm + d_attn]
    k_ref[...] = p[:, d_ssm + d_attn:d_ssm + 2 * d_attn]
    v_ref[...] = p[:, d_ssm + 2 * d_attn:]


def _proj(x, g, w_in, *, d_ssm, tm):
    n, d = x.shape
    d_attn = (w_in.shape[1] - d_ssm) // 3
    row = lambda w: pl.BlockSpec((tm, w), lambda i: (i, 0))
    return pl.pallas_call(
        functools.partial(_proj_body, d_ssm=d_ssm, d_attn=d_attn),
        grid=(n // tm,),
        in_specs=[row(d), _const((1, d)), _const(w_in.shape)],
        out_specs=[row(d_ssm), row(d_attn), row(d_attn), row(d_attn)],
        out_shape=[jax.ShapeDtypeStruct((n, d_ssm), F32)] + [jax.ShapeDtypeStruct((n, d_attn), F32)] * 3,
        compiler_params=_params("parallel"),
        name="mix_proj",
    )(x, g, w_in)


AUX_BIAS, AUX_ONE, AUX_BLOCK, AUX_LANES, MAX_BLOCKS = 0, 3, 16, 64, 32
LOG2E = math.log2(math.e)
BOUND_SLACK = 1.02
MAX_SPREAD = 60.0
PAST_UNROLL = 4


def _alibi_slope(h, n_heads):
    return 2.0 ** (-8.0 * (h + 1) / n_heads)


def _bf16_split3(x):
    hi = x.astype(BF16).astype(F32)
    r = x - hi
    mid = r.astype(BF16).astype(F32)
    lo = (r - mid).astype(BF16).astype(F32)
    return hi, mid, lo


def _proj_prompt_body(x_ref, g_ref, w_ref, u_ref, q_ref, kt_ref, vt_ref, kaug_ref, vtb_ref, sums_ref, kn_ref,
                      *, d_ssm, d_attn, n_heads):
    tm = x_ref.shape[1]
    dh = ATTN_HEAD_DIM
    h = _rms(x_ref[0], g_ref[...]).astype(BF16)
    p = _dot(h, w_ref[...])
    u_ref[...] = p[:, :d_ssm]
    q_ref[0] = p[:, d_ssm:d_ssm + d_attn]
    k = p[:, d_ssm + d_attn:d_ssm + 2 * d_attn]
    v = p[:, d_ssm + 2 * d_attn:]
    kt_ref[0] = k.T.reshape(n_heads, dh, tm)
    vt = v.T.reshape(n_heads, dh, tm)
    vt_ref[0] = vt
    for c in range(tm // MOBA_BLOCK):
        cs = slice(c * MOBA_BLOCK, (c + 1) * MOBA_BLOCK)
        vtb_ref[0, :, c] = vt[:, :, cs].astype(BF16)
        sums_ref[0, c:c + 1, :] = jnp.sum(k[cs], axis=0, keepdims=True)
    pos = pl.program_id(1) * tm + lax.broadcasted_iota(jnp.int32, (tm, AUX_LANES), 0)
    lane = lax.broadcasted_iota(jnp.int32, (tm, AUX_LANES), 1)
    block_hot = (lane - AUX_BLOCK == pos // MOBA_BLOCK).astype(F32)
    posf = pos.astype(F32)
    head_of = (lax.broadcasted_iota(jnp.int32, (d_attn, LANES), 0) // dh
               == lax.broadcasted_iota(jnp.int32, (d_attn, LANES), 1))
    kn_ref[0] = jnp.max(_dot((k * k).astype(BF16), head_of.astype(BF16)), axis=0, keepdims=True)
    pieces = []
    for hh in range(n_heads):
        hi, mid, lo = _bf16_split3((_alibi_slope(hh, n_heads) * LOG2E) * posf)
        aux = jnp.where(lane == AUX_BIAS, hi, jnp.where(lane == AUX_BIAS + 1, mid, jnp.where(
            lane == AUX_BIAS + 2, lo, jnp.where(lane < AUX_ONE + 3, 1.0, block_hot))))
        pieces += [k[:, hh * dh:(hh + 1) * dh], aux]
    kaug_ref[0] = jnp.concatenate(pieces, axis=1).astype(BF16)


def _proj_prompt(x, g, w_in, *, d_ssm, tm):
    b, l, d = x.shape
    d_attn = (w_in.shape[1] - d_ssm) // 3
    dh = ATTN_HEAD_DIM
    n_heads = d_attn // dh
    nb = l // MOBA_BLOCK
    tb = tm // MOBA_BLOCK
    assert nb <= MAX_BLOCKS and tm % MOBA_BLOCK == 0
    row = lambda w: pl.BlockSpec((1, tm, w), lambda i, j: (i, j, 0))
    tr = pl.BlockSpec((1, n_heads, dh, tm), lambda i, j: (i, 0, 0, j))
    return pl.pallas_call(
        functools.partial(_proj_prompt_body, d_ssm=d_ssm, d_attn=d_attn, n_heads=n_heads),
        grid=(b, l // tm),
        in_specs=[row(d), _full((1, d)), _full(w_in.shape)],
        out_specs=[pl.BlockSpec((tm, d_ssm), lambda i, j: (j, i)),
                   row(d_attn), tr, tr, row(n_heads * (dh + AUX_LANES)),
                   pl.BlockSpec((1, n_heads, tb, dh, MOBA_BLOCK), lambda i, j: (i, 0, j, 0, 0)),
                   pl.BlockSpec((1, tb, d_attn), lambda i, j: (i * (l // tm) + j, 0, 0)),
                   pl.BlockSpec((1, 1, LANES), lambda i, j: (i * (l // tm) + j, 0, 0))],
        out_shape=[jax.ShapeDtypeStruct((l, b * d_ssm), F32), jax.ShapeDtypeStruct((b, l, d_attn), F32),
                   jax.ShapeDtypeStruct((b, n_heads, dh, l), F32), jax.ShapeDtypeStruct((b, n_heads, dh, l), F32),
                   jax.ShapeDtypeStruct((b, l, n_heads * (dh + AUX_LANES)), BF16),
                   jax.ShapeDtypeStruct((b, n_heads, nb, dh, MOBA_BLOCK), BF16),
                   jax.ShapeDtypeStruct((b * l // tm, tb, d_attn), F32),
                   jax.ShapeDtypeStruct((b * l // tm, 1, LANES), F32)],
        compiler_params=_params("parallel", "parallel"),
        name="mix_proj_prompt",
    )(x, g, w_in)


def _s5_discretise(a_re, a_im, log_dt, b_re, b_im):
    dt = jnp.exp(log_dt)[:, None]
    mag = jnp.exp(a_re * dt)
    abar_re = mag * jnp.cos(a_im * dt)
    abar_im = mag * jnp.sin(a_im * dt)
    den = a_re * a_re + a_im * a_im
    nr = abar_re - 1.0
    f_re = (nr * a_re + abar_im * a_im) / den
    f_im = (abar_im * a_re - nr * a_im) / den
    bbar_re = f_re[..., None] * b_re - f_im[..., None] * b_im
    bbar_im = f_re[..., None] * b_im + f_im[..., None] * b_re
    return abar_re, abar_im, bbar_re, bbar_im


def _s5_matrices(a_re, a_im, log_dt, b_re, b_im, c_re, c_im):
    g, p, c = b_re.shape
    abar_re, abar_im, bbar_re, bbar_im = _s5_discretise(a_re, a_im, log_dt, b_re, b_im)
    eye = jnp.eye(g, dtype=F32)
    bm_re = jnp.einsum('gpc,gh->gchp', bbar_re, eye).reshape(g * c, g * p)
    bm_im = jnp.einsum('gpc,gh->gchp', bbar_im, eye).reshape(g * c, g * p)
    bmat = jnp.concatenate([bm_re, bm_im], axis=1).astype(BF16)
    cm_re = jnp.einsum('gcp,gh->gphc', c_re, eye).reshape(g * p, g * c)
    cm_im = jnp.einsum('gcp,gh->gphc', c_im, eye).reshape(g * p, g * c)
    cmat = jnp.concatenate([cm_re, -cm_im], axis=0).astype(BF16)
    return abar_re.reshape(1, g * p), abar_im.reshape(1, g * p), bmat, cmat


S5_IN_TILES = 8
S5_OUT_TILES = 2


def _s5_compact(bmat, cmat):
    d_ssm, gp2 = bmat.shape
    gp = gp2 // 2
    ci, si = d_ssm // S5_IN_TILES, gp // S5_IN_TILES
    bc = jnp.stack([jnp.concatenate([bmat[c * ci:(c + 1) * ci, c * si:(c + 1) * si],
                                     bmat[c * ci:(c + 1) * ci, gp + c * si:gp + (c + 1) * si]], axis=1)
                    for c in range(S5_IN_TILES)])
    co, so = d_ssm // S5_OUT_TILES, gp // S5_OUT_TILES
    cc = jnp.stack([jnp.stack([cmat[part * gp + o * so:part * gp + (o + 1) * so, o * co:(o + 1) * co]
                               for part in range(2)]) for o in range(S5_OUT_TILES)])
    return bc, cc


def _s5_output(ch, u, d_ref, wg_ref, bg_ref):
    y = ch + d_ref[...] * u
    z = jax.nn.gelu(y, approximate=True)
    return z * jax.nn.sigmoid(_dot(z.astype(BF16), wg_ref[...]) + bg_ref[...])


def _s5_prompt_body(u_ref, ar_ref, ai_ref, bc_ref, cc_ref, d_ref, wg_ref, bg_ref,
                    y_ref, hlast_ref, hbuf, carry, tbuf, *, nbatch):
    gp = ar_ref.shape[1]
    steps = u_ref.shape[0]
    rows = steps * nbatch
    d_ssm = u_ref.shape[1] // nbatch
    tiles = d_ssm // LANES

    @pl.when(pl.program_id(0) == 0)
    def _():
        carry[...] = jnp.zeros(carry.shape, F32)

    for b in range(nbatch):
        for k in range(tiles):
            tbuf[k, pl.ds(b, steps, stride=nbatch), :] = u_ref[:, b * d_ssm + k * LANES:b * d_ssm + (k + 1) * LANES]
    u = jnp.concatenate([tbuf[k] for k in range(tiles)], axis=1)
    ub = u.astype(BF16)
    ci, si = u.shape[1] // S5_IN_TILES, gp // S5_IN_TILES
    for c in range(S5_IN_TILES):
        bu = _dot(ub[:, c * ci:(c + 1) * ci], bc_ref[c])
        hbuf[:, c * si:(c + 1) * si] = bu[:, :si]
        hbuf[:, gp + c * si:gp + (c + 1) * si] = bu[:, si:]

    half = gp // 2
    lower = lax.broadcasted_iota(jnp.int32, (SUBLANES, half), 0) < nbatch
    halves = lambda v: jnp.where(lower, jnp.broadcast_to(v[:, :half], (SUBLANES, half)),
                                 jnp.broadcast_to(v[:, half:], (SUBLANES, half)))
    ar, ai = halves(ar_ref[...]), halves(ai_ref[...])
    swap = lambda v: pltpu.roll(v, nbatch, 0)

    def step(i, st):
        hr, hi = st
        rows8 = pl.ds(pl.multiple_of(i * SUBLANES, SUBLANES), SUBLANES)
        new = []
        for part in range(2):
            a = hbuf[rows8, part * gp:part * gp + half]
            b = hbuf[rows8, part * gp + half:(part + 1) * gp]
            new.append((jnp.where(lower, a, swap(b)), jnp.where(lower, swap(a), b)))
        (x0r, x1r), (x0i, x1i) = new
        n0r = ar * hr - ai * hi + x0r
        n0i = ar * hi + ai * hr + x0i
        n1r = ar * n0r - ai * n0i + x1r
        n1i = ar * n0i + ai * n0r + x1i
        for part, (n0, n1) in enumerate(((n0r, n1r), (n0i, n1i))):
            hbuf[rows8, part * gp:part * gp + half] = jnp.where(lower, n0, swap(n1))
            hbuf[rows8, part * gp + half:(part + 1) * gp] = jnp.where(lower, swap(n0), n1)
        return n1r, n1i

    hr, hi = lax.fori_loop(0, rows // SUBLANES, step, (carry[:, :half], carry[:, half:]))
    carry[:, :half] = hr
    carry[:, half:] = hi
    hlast_ref[:, :half] = hr
    hlast_ref[:, half:] = hi
    so = gp // S5_OUT_TILES
    ch = jnp.concatenate(
        [_dot(hbuf[:, o * so:(o + 1) * so].astype(BF16), cc_ref[o, 0])
         + _dot(hbuf[:, gp + o * so:gp + (o + 1) * so].astype(BF16), cc_ref[o, 1]) for o in range(S5_OUT_TILES)],
        axis=1)
    y = _s5_output(ch, u, d_ref, wg_ref, bg_ref)
    for k in range(tiles):
        tbuf[k] = y[:, k * LANES:(k + 1) * LANES]
    for b in range(nbatch):
        for k in range(tiles):
            y_ref[:, b * d_ssm + k * LANES:b * d_ssm + (k + 1) * LANES] = tbuf[k, pl.ds(b, steps, stride=nbatch), :]


def _s5_prompt(u_tb, nbatch, ar, ai, bmat, cmat, d, w_glu, b_glu, *, t_chunk):
    l, width = u_tb.shape
    d_ssm = width // nbatch
    gp = ar.shape[1]
    bmat, cmat = _s5_compact(bmat, cmat)
    rows = t_chunk * nbatch
    assert 2 * nbatch == SUBLANES and l % t_chunk == 0 and d_ssm % LANES == 0
    return pl.pallas_call(
        functools.partial(_s5_prompt_body, nbatch=nbatch),
        grid=(l // t_chunk,),
        in_specs=[pl.BlockSpec((t_chunk, width), lambda i: (i, 0)), _full((1, gp)), _full((1, gp)),
                  _full(bmat.shape), _full(cmat.shape), _full((1, d_ssm)), _full(w_glu.shape),
                  _full((1, d_ssm))],
        out_specs=[pl.BlockSpec((t_chunk, width), lambda i: (i, 0)), _full((SUBLANES, gp))],
        out_shape=[jax.ShapeDtypeStruct((l, width), F32),
                   jax.ShapeDtypeStruct((SUBLANES, gp), F32)],
        scratch_shapes=[pltpu.VMEM((rows, 2 * gp), F32), pltpu.VMEM((SUBLANES, gp), F32),
                        pltpu.VMEM((d_ssm // LANES, rows, LANES), F32)],
        compiler_params=_params("arbitrary"),
        name="s5_prompt",
    )(u_tb, ar, ai, bmat, cmat, d, w_glu, b_glu)


def _s5_sample_body(u_ref, h0r_ref, h0i_ref, ar_ref, ai_ref, bmat_ref, cmat_ref, d_ref, wg_ref, bg_ref,
                    y_ref, hr_ref, hi_ref):
    gp = ar_ref.shape[1]
    u = u_ref[...]
    bu = _dot(u.astype(BF16), bmat_ref[...])
    ar, ai = ar_ref[...], ai_ref[...]
    h0r, h0i = h0r_ref[...], h0i_ref[...]
    hr = bu[:, :gp] + ar * h0r - ai * h0i
    hi = bu[:, gp:] + ar * h0i + ai * h0r
    hr_ref[...] = hr
    hi_ref[...] = hi
    h_bf = jnp.concatenate([hr, hi], axis=1).astype(BF16)
    y_ref[...] = _s5_output(_dot(h_bf, cmat_ref[...]), u, d_ref, wg_ref, bg_ref)


def _s5_sample(u, h0r, h0i, ar, ai, bmat, cmat, d, w_glu, b_glu):
    n, d_ssm = u.shape
    gp = ar.shape[1]
    args = (u, h0r, h0i, ar, ai, bmat, cmat, d, w_glu, b_glu)
    return pl.pallas_call(
        _s5_sample_body,
        grid=(1,),
        in_specs=[_full(a.shape) for a in args],
        out_specs=[_full((n, d_ssm)), _full((n, gp)), _full((n, gp))],
        out_shape=[jax.ShapeDtypeStruct((n, d_ssm), F32), jax.ShapeDtypeStruct((n, gp), F32),
                   jax.ShapeDtypeStruct((n, gp), F32)],
        compiler_params=_params("arbitrary"),
        name="s5_sample",
    )(*args)


def _top_blocks(gate, brow, n_valid):
    bf = brow.astype(F32)
    g = jnp.where(brow < n_valid, gate, NEG_INF)
    sel = jnp.zeros(gate.shape, jnp.bool_)
    for _ in range(MOBA_TOPK):
        mx = jnp.max(g, axis=0, keepdims=True)
        first = jnp.min(jnp.where(g == mx, bf, float(gate.shape[0])), axis=0, keepdims=True)
        pick = bf == first
        sel = jnp.logical_or(sel, pick)
        g = jnp.where(pick, -jnp.inf, g)
    return jnp.logical_and(sel, brow < n_valid)


def _moba_prompt_body(q_ref, kaug_ref, vtb_ref, sums_ref, kn_ref, o_ref, qaug, acc, s_scr, p_scr, *, n_heads):
    j = pl.program_id(1)
    blk = MOBA_BLOCK
    dh = ATTN_HEAD_DIM
    ka = dh + AUX_LANES
    nb = sums_ref.shape[1]
    t0 = pl.multiple_of(j * blk, blk)
    scale = dh ** -0.5
    qt = q_ref[0].T
    brow = lax.broadcasted_iota(jnp.int32, (MAX_BLOCKS, blk), 0)
    crow = lax.broadcasted_iota(jnp.int32, (AUX_BLOCK, blk), 0)
    t0f = jnp.broadcast_to(t0.astype(F32), (AUX_BLOCK, blk))
    kmax2 = jnp.max(kn_ref[0], axis=0, keepdims=True)
    hlane = lax.broadcasted_iota(jnp.int32, kmax2.shape, 1)
    tq = lax.broadcasted_iota(jnp.int32, (1, blk), 1).astype(F32)
    shifts, spreads = [], []
    for h in range(n_heads):
        slope = _alibi_slope(h, n_heads)
        qth = qt[h * dh:(h + 1) * dh]
        means = sums_ref[0, :, h * dh:(h + 1) * dh] * (1.0 / blk)
        gate = jnp.dot(means, qth, precision=HIGHEST, preferred_element_type=F32)
        if nb < MAX_BLOCKS:
            gate = jnp.concatenate([gate, jnp.full((MAX_BLOCKS - nb, blk), NEG_INF, F32)], axis=0)
        keep = jnp.logical_or(_top_blocks(gate, brow, j), brow == j)
        mask = jnp.where(keep, 0.0, NEG_INF)
        hi, mid, lo = _bf16_split3((-slope * LOG2E) * t0f)
        const = jnp.where(crow < AUX_ONE, 1.0, jnp.where(crow == AUX_ONE, hi, jnp.where(
            crow == AUX_ONE + 1, mid, jnp.where(crow == AUX_ONE + 2, lo, 0.0))))
        rest = jnp.zeros((AUX_LANES - AUX_BLOCK - MAX_BLOCKS, blk), F32)
        qaug[h] = jnp.concatenate([qth * (scale * LOG2E), const, mask, rest], axis=0).astype(BF16)
        k2 = jnp.max(jnp.where(hlane == h, kmax2, 0.0), axis=1, keepdims=True)
        bound = jnp.sqrt(jnp.sum(qth * qth, axis=0, keepdims=True) * k2) * (scale * BOUND_SLACK)
        shifts.append((bound + slope * tq) * LOG2E)
        spreads.append(bound)
    shift = jnp.concatenate(shifts, axis=0)
    safe = 2.0 * jnp.max(jnp.concatenate(spreads, axis=0)) < MAX_SPREAD

    def scores(n, h):
        r = pl.multiple_of(n * blk, blk)
        return _dot(kaug_ref[0, pl.ds(r, blk), h * ka:(h + 1) * ka], qaug[h])

    causal = (lax.broadcasted_iota(jnp.int32, (blk, blk), 0) <= lax.broadcasted_iota(jnp.int32, (blk, blk), 1))

    def finish(l_all):
        out_t = jnp.concatenate([acc[h * dh:(h + 1) * dh, :] / l_all[h:h + 1] for h in range(n_heads)], axis=0)
        o_ref[0] = out_t.T

    @pl.when(safe)
    def _():
        def consume(n, l_all, get_scores, own=False):
            ls = []
            for h in range(n_heads):
                s = get_scores(h)
                if own:
                    s = jnp.where(causal, s, NEG_INF)
                p = jnp.exp2(s - shift[h:h + 1])
                p_scr[h] = p.astype(BF16)
                ls.append(jnp.sum(p, axis=0, keepdims=True))
            for h in range(n_heads):
                hs = slice(h * dh, (h + 1) * dh)
                pv = _dot(vtb_ref[0, h, n], p_scr[h])
                acc[hs, :] = pv if own else acc[hs, :] + pv
            l_new = jnp.concatenate(ls, axis=0)
            return l_new if own else l_all + l_new

        l_all = consume(j, None, lambda h: scores(j, h), own=True)

        def group(i, l_all):
            for r in range(PAST_UNROLL):
                n = i * PAST_UNROLL + r
                l_all = consume(n, l_all, lambda h: scores(n, h))
            return l_all

        done = (j // PAST_UNROLL) * PAST_UNROLL
        l_all = lax.fori_loop(0, j // PAST_UNROLL, group, l_all)
        finish(lax.fori_loop(done, j, lambda n, l_all: consume(n, l_all, lambda h: scores(n, h)), l_all))

    @pl.when(jnp.logical_not(safe))
    def _():
        def tile(n, st, own):
            ms, ls, alphas = [], [], []
            for h in range(n_heads):
                s_scr[h] = scores(n, h)
            for h in range(n_heads):
                s = s_scr[h]
                if own:
                    s = jnp.where(causal, s, NEG_INF)
                    m_new = jnp.max(s, axis=0, keepdims=True)
                else:
                    m_new = jnp.maximum(st[0][h:h + 1], jnp.max(s, axis=0, keepdims=True))
                    alphas.append(jnp.exp2(st[0][h:h + 1] - m_new))
                p = jnp.exp2(s - m_new)
                p_scr[h] = p.astype(BF16)
                ms.append(m_new)
                l_new = jnp.sum(p, axis=0, keepdims=True)
                ls.append(l_new if own else alphas[h] * st[1][h:h + 1] + l_new)
            for h in range(n_heads):
                hs = slice(h * dh, (h + 1) * dh)
                pv = _dot(vtb_ref[0, h, n], p_scr[h])
                acc[hs, :] = pv if own else alphas[h] * acc[hs, :] + pv
            return jnp.concatenate(ms, axis=0), jnp.concatenate(ls, axis=0)

        finish(lax.fori_loop(0, j, lambda n, st: tile(n, st, False), tile(j, None, True))[1])


def _moba_prompt(q, kaug, vtb, sums, knorm):
    b, l, d = q.shape
    nb = l // MOBA_BLOCK
    n_heads = d // ATTN_HEAD_DIM
    ka = ATTN_HEAD_DIM + AUX_LANES
    once = pl.Buffered(1)
    return pl.pallas_call(
        functools.partial(_moba_prompt_body, n_heads=n_heads),
        grid=(b, nb),
        in_specs=[pl.BlockSpec((1, MOBA_BLOCK, d), lambda i, j: (i, j, 0)),
                  pl.BlockSpec((1, l, n_heads * ka), lambda i, j: (i, 0, 0), pipeline_mode=once),
                  pl.BlockSpec((1, n_heads, nb, ATTN_HEAD_DIM, MOBA_BLOCK), lambda i, j: (i, 0, 0, 0, 0),
                               pipeline_mode=once),
                  pl.BlockSpec((1, nb, d), lambda i, j: (i, 0, 0)),
                  pl.BlockSpec((1,) + knorm.shape[1:], lambda i, j: (i, 0, 0))],
        out_specs=pl.BlockSpec((1, MOBA_BLOCK, d), lambda i, j: (i, j, 0)),
        out_shape=jax.ShapeDtypeStruct((b, l, d), F32),
        scratch_shapes=[pltpu.VMEM((n_heads, ka, MOBA_BLOCK), BF16), pltpu.VMEM((d, MOBA_BLOCK), F32),
                        pltpu.VMEM((n_heads, MOBA_BLOCK, MOBA_BLOCK), F32),
                        pltpu.VMEM((n_heads, MOBA_BLOCK, MOBA_BLOCK), BF16)],
        compiler_params=_params("parallel", "arbitrary"),
        name="moba_prompt",
    )(q, kaug, vtb, sums, knorm)


def _moba_select_body(ps_ref, idx_ref, *, pages_per_block, n_blocks):
    ps = ps_ref[...]
    lanes = ps.shape[1]
    lane = lax.broadcasted_iota(jnp.int32, (1, lanes), 1)
    g = ps
    for i in range(1, pages_per_block):
        g = g + pltpu.roll(ps, lanes - i, 1)
    g = g * (1.0 / MOBA_BLOCK)
    is_block = jnp.logical_and(lane % pages_per_block == 0, lane < n_blocks * pages_per_block)
    g = jnp.where(is_block, g, -jnp.inf)
    lanef = lane.astype(F32)
    out = jnp.zeros(idx_ref.shape, F32)
    olane = lax.broadcasted_iota(jnp.int32, idx_ref.shape, 1)
    for r in range(MOBA_TOPK):
        mx = jnp.max(g, axis=1, keepdims=True)
        first = jnp.min(jnp.where(g == mx, lanef, float(lanes)), axis=1, keepdims=True)
        out = jnp.where(olane == r, first * (1.0 / pages_per_block), out)
        g = jnp.where(lanef == first, -jnp.inf, g)
    idx_ref[...] = out.astype(jnp.int32)


def _moba_select(page_sums2, *, pages_per_block, n_blocks):
    rows, lanes = page_sums2.shape
    return pl.pallas_call(
        functools.partial(_moba_select_body, pages_per_block=pages_per_block, n_blocks=n_blocks),
        grid=(1,),
        in_specs=[_full((rows, lanes))],
        out_specs=_full((rows, LANES)),
        out_shape=jax.ShapeDtypeStruct((rows, LANES), jnp.int32),
        compiler_params=_params("arbitrary"),
        name="moba_sample_select",
    )(page_sums2)


def _moba_attend_body(blk_ref, pg_ref, q_ref, kn_ref, vn_ref, *refs, pages_per_block, n_heads, past_len):
    del pg_ref
    n_lg = MOBA_TOPK
    n_v = MOBA_TOPK * pages_per_block
    lg_refs, v_refs, o_ref = refs[:n_heads * n_lg], refs[n_heads * n_lg:n_heads * (n_lg + n_v)], refs[-1]
    b = pl.program_id(0)
    dh = ATTN_HEAD_DIM
    blk = MOBA_BLOCK
    page = blk // pages_per_block
    scale = dh ** -0.5
    lane = lax.broadcasted_iota(jnp.int32, (1, blk), 1)
    qk = q_ref[0] * kn_ref[0]
    outs = []
    for h in range(n_heads):
        hs = slice(h * dh, (h + 1) * dh)
        slope = _alibi_slope(h, n_heads)
        s_own = jnp.sum(qk[:, hs], axis=1, keepdims=True) * scale
        ss = []
        for i in range(n_lg):
            pos = blk_ref[b, h, i] * blk + lane
            ss.append(lg_refs[h * n_lg + i][0, 0, h:h + 1, :] * scale - slope * (past_len - pos).astype(F32))
        m = s_own
        for s in ss:
            m = jnp.maximum(m, jnp.max(s, axis=1, keepdims=True))
        p_own = jnp.exp(s_own - m)
        l = p_own
        acc = p_own * vn_ref[0][:, hs]
        for i in range(n_lg):
            p = jnp.exp(ss[i] - m)
            l = l + jnp.sum(p, axis=1, keepdims=True)
            for c in range(pages_per_block):
                vt = v_refs[h * n_v + i * pages_per_block + c][0, 0, 0].astype(BF16)
                acc = acc + _dot_t(p[:, c * page:(c + 1) * page].astype(BF16), vt)
        outs.append(acc / l)
    o_ref[0] = jnp.concatenate(outs, axis=1)


def _moba_attend(blocks, pages_idx, q3, kn3, vn3, logits, cache_vt, *, pages_per_block, past_len):
    n, _, d = q3.shape
    _, _, n_heads, dh, page = cache_vt.shape
    row = pl.BlockSpec((1, 1, d), lambda b, bl, pg: (b, 0, 0))
    lg_spec = lambda h, i: pl.BlockSpec(
        (1, 1, n_heads, MOBA_BLOCK), lambda b, bl, pg: (b, bl[b, h, i], 0, 0))
    v_spec = lambda h, i: pl.BlockSpec(
        (1, 1, 1, dh, page), lambda b, bl, pg: (0, pg[b, h, i], h, 0, 0))
    n_v = MOBA_TOPK * pages_per_block
    lg_specs = [lg_spec(h, i) for h in range(n_heads) for i in range(MOBA_TOPK)]
    v_specs = [v_spec(h, i) for h in range(n_heads) for i in range(n_v)]
    grid_spec = pltpu.PrefetchScalarGridSpec(
        num_scalar_prefetch=2,
        grid=(n,),
        in_specs=[row, row, row] + lg_specs + v_specs,
        out_specs=row)
    return pl.pallas_call(
        functools.partial(_moba_attend_body, pages_per_block=pages_per_block, n_heads=n_heads,
                          past_len=past_len),
        grid_spec=grid_spec,
        out_shape=jax.ShapeDtypeStruct((n, 1, d), F32),
        compiler_params=_params("parallel"),
        name="moba_sample_attend",
    )(blocks, pages_idx, q3, kn3, vn3, *([logits] * len(lg_specs)), *([cache_vt] * len(v_specs)))


def _moba_sample(q, k_new, v_new, logits, page_sums, cache_vt, page_table):
    n, d = q.shape
    _, _, n_heads, dh, page = cache_vt.shape
    n_pages = page_table.shape[1]
    past_len = n_pages * page
    pages_per_block = MOBA_BLOCK // page
    n_blocks = n_pages // pages_per_block
    assert n_blocks >= MOBA_TOPK and n_pages % LANES == 0
    q3 = q.reshape(n, 1, d)
    blocks = _moba_select(page_sums.reshape(n * n_heads, n_pages), pages_per_block=pages_per_block,
                          n_blocks=n_blocks)[:, :MOBA_TOPK].reshape(n, n_heads, MOBA_TOPK)
    logical = (blocks[..., None] * pages_per_block + jnp.arange(pages_per_block)).reshape(n, n_heads, -1)
    pages_idx = jnp.take_along_axis(page_table[:, None, :], logical, axis=2)
    y = _moba_attend(blocks, pages_idx, q3, k_new.reshape(n, 1, d), v_new.reshape(n, 1, d),
                     logits, cache_vt,
                     pages_per_block=pages_per_block, past_len=past_len)
    return y.reshape(n, d)


def _memkv_body(m_ref, g_ref, wk_ref, wv_ref, k_ref, v_ref):
    h = _rms(m_ref[...], g_ref[...]).astype(BF16)
    k_ref[...] = _dot(h, wk_ref[...])
    v_ref[...] = _dot(h, wv_ref[...])


def _memkv(mem, g, wk, wv, *, tm):
    n, d = mem.shape
    row = pl.BlockSpec((tm, d), lambda i: (i, 0))
    return pl.pallas_call(
        _memkv_body,
        grid=(n // tm,),
        in_specs=[row, _full((1, d)), _full(wk.shape), _full(wv.shape)],
        out_specs=[row, row],
        out_shape=[jax.ShapeDtypeStruct((n, d), F32)] * 2,
        compiler_params=_params("parallel"),
        name="mem_kv",
    )(mem, g, wk, wv)


def _mix_out(x, ys, ya, gs, ga, w_out_ref, d_ssm):
    ysn = _rms(ys, gs).astype(BF16)
    yan = _rms(ya, ga).astype(BF16)
    return x + _dot(ysn, w_out_ref[:d_ssm, :]) + _dot(yan, w_out_ref[d_ssm:, :])


def _xattn_heads(q, mk_ref, mv_ref):
    d = q.shape[1]
    xd = d // N_XHEADS
    outs = []
    for h in range(N_XHEADS):
        hs = slice(h * xd, (h + 1) * xd)
        s = _dot_t(q[:, hs].astype(BF16), mk_ref[0, :, hs].astype(BF16)) * (xd ** -0.5)
        m = jnp.max(s, axis=1, keepdims=True)
        p = jnp.exp(s - m)
        l = jnp.sum(p, axis=1, keepdims=True)
        outs.append(_dot(p.astype(BF16), mv_ref[0, :, hs].astype(BF16)) / l)
    return jnp.concatenate(outs, axis=1)


def _merge_prompt_body(x_ref, ys_ref, ya_ref, gs_ref, ga_ref, wout_ref, gx_ref, wq_ref, mk_ref, mv_ref,
                       wo_ref, o_ref, *, d_ssm):
    x2 = _mix_out(x_ref[0], ys_ref[...], ya_ref[0], gs_ref[...], ga_ref[...], wout_ref, d_ssm)
    q = _dot(_rms(x2, gx_ref[...]).astype(BF16), wq_ref[...])
    o = _xattn_heads(q, mk_ref, mv_ref)
    o_ref[0] = x2 + _dot(o.astype(BF16), wo_ref[...])


def _merge_prompt(x, ys, ya, gs, ga, w_out, gx, wq, mk, mv, wo, *, tm):
    b, l, d = x.shape
    d_ssm = ys.shape[1] // b
    d_attn = ya.shape[2]
    n_mem = mk.shape[1]
    row = lambda w: pl.BlockSpec((1, tm, w), lambda i, j: (i, j, 0))
    mem = pl.BlockSpec((1, n_mem, d), lambda i, j: (i, 0, 0))
    return pl.pallas_call(
        functools.partial(_merge_prompt_body, d_ssm=d_ssm),
        grid=(b, l // tm),
        in_specs=[row(d), pl.BlockSpec((tm, d_ssm), lambda i, j: (j, i)), row(d_attn), _full((1, d_ssm)),
                  _full((1, d_attn)), _full(w_out.shape), _full((1, d)), _full(wq.shape), mem, mem,
                  _full(wo.shape)],
        out_specs=row(d),
        out_shape=jax.ShapeDtypeStruct((b, l, d), F32),
        compiler_params=_params("parallel", "parallel"),
        name="merge_prompt",
    )(x, ys, ya, gs, ga, w_out, gx, wq, mk, mv, wo)


def _merge_pre_body(x_ref, ys_ref, ya_ref, gs_ref, ga_ref, wout_ref, gx_ref, wq_ref, x2_ref, q_ref, *, d_ssm):
    x2 = _mix_out(x_ref[...], ys_ref[...], ya_ref[...], gs_ref[...], ga_ref[...], wout_ref, d_ssm)
    x2_ref[...] = x2
    q_ref[...] = _dot(_rms(x2, gx_ref[...]).astype(BF16), wq_ref[...])


def _merge_pre(x, ys, ya, gs, ga, w_out, gx, wq):
    n, d = x.shape
    args = (x, ys, ya, gs, ga, w_out, gx, wq)
    return pl.pallas_call(
        functools.partial(_merge_pre_body, d_ssm=ys.shape[1]),
        grid=(1,),
        in_specs=[_full(a.shape) for a in args],
        out_specs=[_full((n, d)), _full((n, d))],
        out_shape=[jax.ShapeDtypeStruct((n, d), F32)] * 2,
        compiler_params=_params("arbitrary"),
        name="merge_sample_pre",
    )(*args)


def _xattn_sample_body(q_ref, mk_ref, mv_ref, o_ref):
    q = q_ref[0]
    n_mem, n_heads, xd = mk_ref.shape[1:]
    q4 = jnp.concatenate([q[:, h * xd:(h + 1) * xd] for h in range(n_heads)], axis=0).astype(BF16)
    k2 = mk_ref[0].reshape(n_mem * n_heads, xd).astype(BF16)
    v2 = mv_ref[0].reshape(n_mem * n_heads, xd).astype(BF16)
    s = _dot_t(q4, k2) * (xd ** -0.5)
    own = lax.broadcasted_iota(jnp.int32, s.shape, 1) % n_heads == lax.broadcasted_iota(jnp.int32, s.shape, 0)
    s = jnp.where(own, s, NEG_INF)
    m = jnp.max(s, axis=1, keepdims=True)
    p = jnp.exp(s - m)
    o4 = _dot(p.astype(BF16), v2) / jnp.sum(p, axis=1, keepdims=True)
    o_ref[0] = jnp.concatenate([o4[h:h + 1] for h in range(n_heads)], axis=1)


def _xattn_sample(q3, mk, mv):
    n, _, d = q3.shape
    row = pl.BlockSpec((1, 1, d), lambda i: (i, 0, 0))
    mem = pl.BlockSpec((1,) + mk.shape[1:], lambda i: (i, 0, 0, 0))
    return pl.pallas_call(
        _xattn_sample_body,
        grid=(n,),
        in_specs=[row, mem, mem],
        out_specs=row,
        out_shape=jax.ShapeDtypeStruct((n, 1, d), F32),
        compiler_params=_params("parallel"),
        name="xattn_sample",
    )(q3, mk, mv)


def _merge_post_body(x_ref, o_ref, wo_ref, y_ref):
    y_ref[...] = x_ref[...] + _dot(o_ref[...].astype(BF16), wo_ref[...])


def _merge_post(x2, o, wo):
    n, d = x2.shape
    return pl.pallas_call(
        _merge_post_body,
        grid=(1,),
        in_specs=[_full((n, d)), _full((n, d)), _full(wo.shape)],
        out_specs=_full((n, d)),
        out_shape=jax.ShapeDtypeStruct((n, d), F32),
        compiler_params=_params("arbitrary"),
        name="merge_sample_post",
    )(x2, o, wo)


def kernel(x_prompt, x_sample, mem_prompt, cache_k, cache_v, page_table, state_ssm_re, state_ssm_im, cache_mem_k, cache_mem_v, g_ffn1, w1_ffn1, w3_ffn1, w2_ffn1, g_mix, w_in, ssm_a_re, ssm_a_im, ssm_log_dt, ssm_b_re, ssm_b_im, ssm_c_re, ssm_c_im, ssm_d, w_glu, b_glu, g_out_ssm, g_out_attn, w_out, g_xattn, g_mem, wq_x, wk_x, wv_x, wo_x, g_ffn2, w1_ffn2, w3_ffn2, w2_ffn2, g_final):
    depth = g_ffn1.shape[0]
    assert depth == 1
    b, l, d = x_prompt.shape
    ns, ls, _ = x_sample.shape
    assert ls == 1
    n_groups, n_state = ssm_a_re.shape[1:]
    gp = n_groups * n_state
    d_ssm = n_groups * SSM_GROUP
    d_attn = (w_in.shape[2] - d_ssm) // 3
    n_heads = d_attn // ATTN_HEAD_DIM
    n_mem = mem_prompt.shape[1]
    xd = d // N_XHEADS

    vec = lambda a: a[0].reshape(1, -1)
    wb = lambda a: a[0].astype(BF16)
    w1a, w3a, w2a = wb(w1_ffn1), wb(w3_ffn1), wb(w2_ffn1)
    w1b, w3b, w2b = wb(w1_ffn2), wb(w3_ffn2), wb(w2_ffn2)
    w_in_b, w_out_b, w_glu_b = wb(w_in), wb(w_out), wb(w_glu)
    wq_b, wk_b, wv_b, wo_b = wb(wq_x), wb(wk_x), wb(wv_x), wb(wo_x)
    gfin = g_final.reshape(1, -1)
    ar, ai, bmat, cmat = _s5_matrices(ssm_a_re[0], ssm_a_im[0], ssm_log_dt[0], ssm_b_re[0], ssm_b_im[0],
                                      ssm_c_re[0], ssm_c_im[0])
    s5_w = (ar, ai, bmat, cmat, vec(ssm_d), w_glu_b, vec(b_glu))

    xs = x_sample.reshape(ns, d)
    xs1 = _ffn(xs, vec(g_ffn1), w1_ffn1[0], w3_ffn1[0], w2_ffn1[0], gfin, final_norm=False, tm=ns)
    us, qs, ks, vs = _proj(xs1, vec(g_mix), w_in[0], d_ssm=d_ssm, tm=ns)
    q_t = qs.reshape(ns, n_heads, ATTN_HEAD_DIM).swapaxes(1, 2)
    cache_kt = cache_k.transpose(0, 1, 3, 4, 2)
    cache_vt = cache_v.transpose(0, 1, 3, 4, 2)
    half_pages = page_table.shape[1] // 2

    tm = 512 if l % 512 == 0 else l
    xp = x_prompt.reshape(b * l, d)
    mem_k, mem_v = _memkv(mem_prompt.reshape(b * n_mem, d), vec(g_mem), wk_b, wv_b,
                          tm=min(256, b * n_mem))
    x1, lg_a, ps_a = _ffn_kpass(xp, vec(g_ffn1), w1a, w3a, w2a, gfin, page_table, q_t, cache_kt,
                                final_norm=False, tm=tm, first_page=0, n_call_pages=half_pages)
    x1 = x1.reshape(b, l, d)
    u, q, k_t, v_t, kaug, vtb, sums, knorm = _proj_prompt(x1, vec(g_mix), w_in_b, d_ssm=d_ssm, tm=tm)
    t_chunk = 128 if l % 128 == 0 else l
    ys, h_last = _s5_prompt(u, b, *s5_w, t_chunk=t_chunk)
    ya = _moba_prompt(q, kaug, vtb, sums.reshape(b, l // MOBA_BLOCK, d_attn), knorm.reshape(b, l // tm, LANES))
    x3 = _merge_prompt(x1, ys, ya, vec(g_out_ssm), vec(g_out_attn), w_out_b, vec(g_xattn),
                       wq_b, mem_k.reshape(b, n_mem, d), mem_v.reshape(b, n_mem, d), wo_b, tm=tm)
    y_prompt, lg_b, ps_b = _ffn_kpass(x3.reshape(b * l, d), vec(g_ffn2), w1b, w3b, w2b, gfin, page_table, q_t,
                                      cache_kt, final_norm=True, tm=tm, first_page=half_pages,
                                      n_call_pages=half_pages)
    h_last = h_last.reshape(2, b, 2, n_groups // 2, n_state).transpose(2, 1, 0, 3, 4).reshape(2, b, gp)
    k = k_t.transpose(0, 3, 1, 2)
    v = v_t.transpose(0, 3, 1, 2)
    heads = (1, b, l, n_heads, ATTN_HEAD_DIM)
    state = (1, b, n_groups, n_state)
    memkv = (1, b, n_mem, N_XHEADS, xd)

    yss, hrs, his = _s5_sample(us, state_ssm_re[0].reshape(ns, gp), state_ssm_im[0].reshape(ns, gp), *s5_w)
    yas = _moba_sample(qs, ks, vs, jnp.concatenate([lg_a, lg_b], axis=1), ps_a + ps_b, cache_vt, page_table)
    xs2, qx = _merge_pre(xs1, yss, yas, vec(g_out_ssm), vec(g_out_attn), w_out_b, vec(g_xattn), wq_b)
    ox = _xattn_sample(qx.reshape(ns, 1, d), cache_mem_k[0], cache_mem_v[0])
    xs3 = _merge_post(xs2, ox.reshape(ns, d), wo_b)
    y_sample = _ffn(xs3, vec(g_ffn2), w1b, w3b, w2b, gfin, final_norm=True, tm=ns)
    sheads = (1, ns, 1, n_heads, ATTN_HEAD_DIM)
    sstate = (1, ns, n_groups, n_state)

    return (y_prompt.reshape(b, l, d), y_sample.reshape(ns, 1, d),
            k.reshape(heads), v.reshape(heads),
            h_last[0].reshape(state), h_last[1].reshape(state),
            mem_k.reshape(memkv), mem_v.reshape(memkv),
            ks.reshape(sheads), vs.reshape(sheads),
            hrs.reshape(sstate), his.reshape(sstate))
```

```python
import functools
import math

import jax
import jax.numpy as jnp
from jax import lax
from jax.experimental import pallas as pl
from jax.experimental.pallas import tpu as pltpu

RMS_EPS = 1e-6
NEG_INF = -1e30
SSM_GROUP = 16
ATTN_HEAD_DIM = 64
MOBA_BLOCK = 256
MOBA_TOPK = 3
N_XHEADS = 4
LANES = 128
SUBLANES = 8
VMEM_LIMIT = 56 * 1024 * 1024

F32 = jnp.float32
BF16 = jnp.bfloat16
HIGHEST = lax.Precision.HIGHEST


def _params(*sem):
    return pltpu.CompilerParams(dimension_semantics=sem, vmem_limit_bytes=VMEM_LIMIT)


def _rms(x, g):
    ms = jnp.mean(x * x, axis=-1, keepdims=True)
    return x * lax.rsqrt(ms + RMS_EPS) * g


def _dot(a, b):
    return jnp.dot(a, b, preferred_element_type=F32)


def _dot_t(a, b, precision=None):
    return lax.dot_general(a, b, (((1,), (1,)), ((), ())), preferred_element_type=F32,
                           precision=precision)


def _full(shape):
    n = len(shape)
    return pl.BlockSpec(shape, lambda *_: (0,) * n)


def _const(shape):
    n = len(shape)
    return pl.BlockSpec(shape, lambda *_: (0,) * n, pipeline_mode=pl.Buffered(1))


def _weight_dot(w_dtype):
    if w_dtype == BF16:
        return (lambda v: v.astype(BF16)), _dot
    return (lambda v: v), functools.partial(jnp.dot, precision=HIGHEST, preferred_element_type=F32)


def _ffn_body(x_ref, g_ref, w1_ref, w3_ref, w2_ref, gf_ref, o_ref, *, f_chunk, final_norm):
    lhs, dot = _weight_dot(w1_ref.dtype)
    x = x_ref[...]
    h = lhs(_rms(x, g_ref[...]))
    acc = jnp.zeros(x.shape, F32)
    for c in range(w1_ref.shape[1] // f_chunk):
        cs = slice(c * f_chunk, (c + 1) * f_chunk)
        a = dot(h, w1_ref[:, cs])
        b = dot(h, w3_ref[:, cs])
        act = lhs(a * jax.nn.sigmoid(a) * b)
        acc = acc + dot(act, w2_ref[cs, :])
    y = x + 0.5 * acc
    if final_norm:
        y = _rms(y, gf_ref[...])
    o_ref[...] = y


def _kpass_pages(q_ref, k_refs, lg_ref, ps_ref, *, first, lane0):
    n_heads, dh, page = k_refs[0].shape[2:]
    ppb = MOBA_BLOCK // page
    qcols = q_ref[0]
    lanes = lambda i: slice((i % ppb) * page, (i % ppb + 1) * page)
    for h in range(n_heads):
        qb = jnp.broadcast_to(qcols[:, h:h + 1], (dh, page))
        for i, k_ref in enumerate(k_refs):
            lg_ref[0, i // ppb, h:h + 1, lanes(i)] = jnp.sum(k_ref[0, 0, h] * qb, axis=0, keepdims=True)
    plane = lax.broadcasted_iota(jnp.int32, (n_heads, ps_ref.shape[2]), 1)
    psum = jnp.where(first, 0.0, ps_ref[0])
    for i in range(len(k_refs)):
        psum = jnp.where(plane == lane0 + i, jnp.sum(lg_ref[0, i // ppb, :, lanes(i)], axis=1, keepdims=True), psum)
    ps_ref[0] = psum


def _ffn_kpass_body(pt_ref, x_ref, g_ref, w1_ref, w3_ref, w2_ref, gf_ref, q_ref, *refs,
                    f_chunk, final_norm, pages, first_page, steps_per_seq):
    del pt_ref
    k_refs, (o_ref, lg_ref, ps_ref) = refs[:pages], refs[pages:]
    _ffn_body(x_ref, g_ref, w1_ref, w3_ref, w2_ref, gf_ref, o_ref, f_chunk=f_chunk, final_norm=final_norm)
    part = pl.program_id(0) % steps_per_seq
    _kpass_pages(q_ref, k_refs, lg_ref, ps_ref, first=part == 0, lane0=first_page + part * pages)


def _ffn_kpass(x, g, w1, w3, w2, g_final, page_table, q_t, cache_kt, *, final_norm, tm, first_page, n_call_pages):
    n, d = x.shape
    f = w1.shape[1]
    f_chunk = 256 if f % 256 == 0 else f
    ns, dh, n_heads = q_t.shape
    n_pages = page_table.shape[1]
    page = cache_kt.shape[4]
    ppb = MOBA_BLOCK // page
    steps = n // tm
    steps_per_seq = steps // ns
    pages = n_call_pages // steps_per_seq
    assert steps % ns == 0 and n_call_pages % steps_per_seq == 0 and pages % ppb == 0
    once = pl.Buffered(1)
    const = lambda shape: pl.BlockSpec(shape, lambda i, pt: (0,) * len(shape), pipeline_mode=once)
    k_spec = lambda ip: pl.BlockSpec(
        (1, 1, n_heads, dh, page),
        lambda i, pt: (0, pt[i // steps_per_seq, first_page + (i % steps_per_seq) * pages + ip], 0, 0, 0))
    grid_spec = pltpu.PrefetchScalarGridSpec(
        num_scalar_prefetch=1,
        grid=(steps,),
        in_specs=[pl.BlockSpec((tm, d), lambda i, pt: (i, 0)), const((1, d)), const((d, f)), const((d, f)),
                  const((f, d)), const((1, d)),
                  pl.BlockSpec((1, dh, n_heads), lambda i, pt: (i // steps_per_seq, 0, 0))]
                 + [k_spec(ip) for ip in range(pages)],
        out_specs=[pl.BlockSpec((tm, d), lambda i, pt: (i, 0)),
                   pl.BlockSpec((1, pages // ppb, n_heads, MOBA_BLOCK),
                                lambda i, pt: (i // steps_per_seq, i % steps_per_seq, 0, 0)),
                   pl.BlockSpec((1, n_heads, n_pages), lambda i, pt: (i // steps_per_seq, 0, 0))])
    return pl.pallas_call(
        functools.partial(_ffn_kpass_body, f_chunk=f_chunk, final_norm=final_norm, pages=pages,
                          first_page=first_page, steps_per_seq=steps_per_seq),
        grid_spec=grid_spec,
        out_shape=[jax.ShapeDtypeStruct((n, d), F32),
                   jax.ShapeDtypeStruct((ns, n_call_pages // ppb, n_heads, MOBA_BLOCK), F32),
                   jax.ShapeDtypeStruct((ns, n_heads, n_pages), F32)],
        compiler_params=_params("arbitrary"),
        name="ffn_final_kpass" if final_norm else "ffn_kpass",
    )(page_table, x, g, w1, w3, w2, g_final, q_t, *([cache_kt] * pages))


def _ffn(x, g, w1, w3, w2, g_final, *, final_norm, tm):
    n, d = x.shape
    f = w1.shape[1]
    f_chunk = 256 if f % 256 == 0 else f
    return pl.pallas_call(
        functools.partial(_ffn_body, f_chunk=f_chunk, final_norm=final_norm),
        grid=(n // tm,),
        in_specs=[pl.BlockSpec((tm, d), lambda i: (i, 0)), _const((1, d)), _const((d, f)), _const((d, f)),
                  _const((f, d)), _const((1, d))],
        out_specs=pl.BlockSpec((tm, d), lambda i: (i, 0)),
        out_shape=jax.ShapeDtypeStruct((n, d), F32),
        compiler_params=_params("parallel"),
        name="ffn_final" if final_norm else "ffn",
    )(x, g, w1, w3, w2, g_final)


def _proj_body(x_ref, g_ref, w_ref, u_ref, q_ref, k_ref, v_ref, *, d_ssm, d_attn):
    lhs, dot = _weight_dot(w_ref.dtype)
    p = dot(lhs(_rms(x_ref[...], g_ref[...])), w_ref[...])
    u_ref[...] = p[:, :d_ssm]
    q_ref[...] = p[:, d_ssm:d_ssm + d_attn]
    k_ref[...] = p[:, d_ssm + d_attn:d_ssm + 2 * d_attn]
    v_ref[...] = p[:, d_ssm + 2 * d_attn:]


def _proj(x, g, w_in, *, d_ssm, tm):
    n, d = x.shape
    d_attn = (w_in.shape[1] - d_ssm) // 3
    row = lambda w: pl.BlockSpec((tm, w), lambda i: (i, 0))
    return pl.pallas_call(
        functools.partial(_proj_body, d_ssm=d_ssm, d_attn=d_attn),
        grid=(n // tm,),
        in_specs=[row(d), _const((1, d)), _const(w_in.shape)],
        out_specs=[row(d_ssm), row(d_attn), row(d_attn), row(d_attn)],
        out_shape=[jax.ShapeDtypeStruct((n, d_ssm), F32)] + [jax.ShapeDtypeStruct((n, d_attn), F32)] * 3,
        compiler_params=_params("parallel"),
        name="mix_proj",
    )(x, g, w_in)


AUX_BIAS, AUX_ONE, AUX_BLOCK, AUX_LANES, MAX_BLOCKS = 0, 3, 16, 64, 32
LOG2E = math.log2(math.e)
BOUND_SLACK = 1.02
MAX_SPREAD = 60.0
PAST_UNROLL = 4


def _alibi_slope(h, n_heads):
    return 2.0 ** (-8.0 * (h + 1) / n_heads)


def _bf16_split3(x):
    hi = x.astype(BF16).astype(F32)
    r = x - hi
    mid = r.astype(BF16).astype(F32)
    lo = (r - mid).astype(BF16).astype(F32)
    return hi, mid, lo


def _proj_prompt_body(x_ref, g_ref, w_ref, u_ref, q_ref, kt_ref, vt_ref, kaug_ref, vtb_ref, sums_ref, kn_ref,
                      *, d_ssm, d_attn, n_heads):
    tm = x_ref.shape[1]
    dh = ATTN_HEAD_DIM
    h = _rms(x_ref[0], g_ref[...]).astype(BF16)
    p = _dot(h, w_ref[...])
    u_ref[...] = p[:, :d_ssm]
    q_ref[0] = p[:, d_ssm:d_ssm + d_attn]
    k = p[:, d_ssm + d_attn:d_ssm + 2 * d_attn]
    v = p[:, d_ssm + 2 * d_attn:]
    kt_ref[0] = k.T.reshape(n_heads, dh, tm)
    vt = v.T.reshape(n_heads, dh, tm)
    vt_ref[0] = vt
    for c in range(tm // MOBA_BLOCK):
        cs = slice(c * MOBA_BLOCK, (c + 1) * MOBA_BLOCK)
        vtb_ref[0, :, c] = vt[:, :, cs].astype(BF16)
        sums_ref[0, c:c + 1, :] = jnp.sum(k[cs], axis=0, keepdims=True)
    pos = pl.program_id(1) * tm + lax.broadcasted_iota(jnp.int32, (tm, AUX_LANES), 0)
    lane = lax.broadcasted_iota(jnp.int32, (tm, AUX_LANES), 1)
    block_hot = (lane - AUX_BLOCK == pos // MOBA_BLOCK).astype(F32)
    posf = pos.astype(F32)
    head_of = (lax.broadcasted_iota(jnp.int32, (d_attn, LANES), 0) // dh
               == lax.broadcasted_iota(jnp.int32, (d_attn, LANES), 1))
    kn_ref[0] = jnp.max(_dot((k * k).astype(BF16), head_of.astype(BF16)), axis=0, keepdims=True)
    pieces = []
    for hh in range(n_heads):
        hi, mid, lo = _bf16_split3((_alibi_slope(hh, n_heads) * LOG2E) * posf)
        aux = jnp.where(lane == AUX_BIAS, hi, jnp.where(lane == AUX_BIAS + 1, mid, jnp.where(
            lane == AUX_BIAS + 2, lo, jnp.where(lane < AUX_ONE + 3, 1.0, block_hot))))
        pieces += [k[:, hh * dh:(hh + 1) * dh], aux]
    kaug_ref[0] = jnp.concatenate(pieces, axis=1).astype(BF16)


def _proj_prompt(x, g, w_in, *, d_ssm, tm):
    b, l, d = x.shape
    d_attn = (w_in.shape[1] - d_ssm) // 3
    dh = ATTN_HEAD_DIM
    n_heads = d_attn // dh
    nb = l // MOBA_BLOCK
    tb = tm // MOBA_BLOCK
    assert nb <= MAX_BLOCKS and tm % MOBA_BLOCK == 0
    row = lambda w: pl.BlockSpec((1, tm, w), lambda i, j: (i, j, 0))
    tr = pl.BlockSpec((1, n_heads, dh, tm), lambda i, j: (i, 0, 0, j))
    return pl.pallas_call(
        functools.partial(_proj_prompt_body, d_ssm=d_ssm, d_attn=d_attn, n_heads=n_heads),
        grid=(b, l // tm),
        in_specs=[row(d), _full((1, d)), _full(w_in.shape)],
        out_specs=[pl.BlockSpec((tm, d_ssm), lambda i, j: (j, i)),
                   row(d_attn), tr, tr, row(n_heads * (dh + AUX_LANES)),
                   pl.BlockSpec((1, n_heads, tb, dh, MOBA_BLOCK), lambda i, j: (i, 0, j, 0, 0)),
                   pl.BlockSpec((1, tb, d_attn), lambda i, j: (i * (l // tm) + j, 0, 0)),
                   pl.BlockSpec((1, 1, LANES), lambda i, j: (i * (l // tm) + j, 0, 0))],
        out_shape=[jax.ShapeDtypeStruct((l, b * d_ssm), F32), jax.ShapeDtypeStruct((b, l, d_attn), F32),
                   jax.ShapeDtypeStruct((b, n_heads, dh, l), F32), jax.ShapeDtypeStruct((b, n_heads, dh, l), F32),
                   jax.ShapeDtypeStruct((b, l, n_heads * (dh + AUX_LANES)), BF16),
                   jax.ShapeDtypeStruct((b, n_heads, nb, dh, MOBA_BLOCK), BF16),
                   jax.ShapeDtypeStruct((b * l // tm, tb, d_attn), F32),
                   jax.ShapeDtypeStruct((b * l // tm, 1, LANES), F32)],
        compiler_params=_params("parallel", "parallel"),
        name="mix_proj_prompt",
    )(x, g, w_in)


def _s5_discretise(a_re, a_im, log_dt, b_re, b_im):
    dt = jnp.exp(log_dt)[:, None]
    mag = jnp.exp(a_re * dt)
    abar_re = mag * jnp.cos(a_im * dt)
    abar_im = mag * jnp.sin(a_im * dt)
    den = a_re * a_re + a_im * a_im
    nr = abar_re - 1.0
    f_re = (nr * a_re + abar_im * a_im) / den
    f_im = (abar_im * a_re - nr * a_im) / den
    bbar_re = f_re[..., None] * b_re - f_im[..., None] * b_im
    bbar_im = f_re[..., None] * b_im + f_im[..., None] * b_re
    return abar_re, abar_im, bbar_re, bbar_im


def _s5_matrices(a_re, a_im, log_dt, b_re, b_im, c_re, c_im):
    g, p, c = b_re.shape
    abar_re, abar_im, bbar_re, bbar_im = _s5_discretise(a_re, a_im, log_dt, b_re, b_im)
    eye = jnp.eye(g, dtype=F32)
    bm_re = jnp.einsum('gpc,gh->gchp', bbar_re, eye).reshape(g * c, g * p)
    bm_im = jnp.einsum('gpc,gh->gchp', bbar_im, eye).reshape(g * c, g * p)
    bmat = jnp.concatenate([bm_re, bm_im], axis=1).astype(BF16)
    cm_re = jnp.einsum('gcp,gh->gphc', c_re, eye).reshape(g * p, g * c)
    cm_im = jnp.einsum('gcp,gh->gphc', c_im, eye).reshape(g * p, g * c)
    cmat = jnp.concatenate([cm_re, -cm_im], axis=0).astype(BF16)
    return abar_re.reshape(1, g * p), abar_im.reshape(1, g * p), bmat, cmat


S5_IN_TILES = 8
S5_OUT_TILES = 2


def _s5_compact(bmat, cmat):
    d_ssm, gp2 = bmat.shape
    gp = gp2 // 2
    ci, si = d_ssm // S5_IN_TILES, gp // S5_IN_TILES
    bc = jnp.stack([jnp.concatenate([bmat[c * ci:(c + 1) * ci, c * si:(c + 1) * si],
                                     bmat[c * ci:(c + 1) * ci, gp + c * si:gp + (c + 1) * si]], axis=1)
                    for c in range(S5_IN_TILES)])
    co, so = d_ssm // S5_OUT_TILES, gp // S5_OUT_TILES
    cc = jnp.stack([jnp.stack([cmat[part * gp + o * so:part * gp + (o + 1) * so, o * co:(o + 1) * co]
                               for part in range(2)]) for o in range(S5_OUT_TILES)])
    return bc, cc


def _s5_output(ch, u, d_ref, wg_ref, bg_ref):
    y = ch + d_ref[...] * u
    z = jax.nn.gelu(y, approximate=True)
    return z * jax.nn.sigmoid(_dot(z.astype(BF16), wg_ref[...]) + bg_ref[...])


def _s5_prompt_body(u_ref, ar_ref, ai_ref, bc_ref, cc_ref, d_ref, wg_ref, bg_ref,
                    y_ref, hlast_ref, hbuf, carry, tbuf, *, nbatch):
    gp = ar_ref.shape[1]
    steps = u_ref.shape[0]
    rows = steps * nbatch
    d_ssm = u_ref.shape[1] // nbatch
    tiles = d_ssm // LANES

    @pl.when(pl.program_id(0) == 0)
    def _():
        carry[...] = jnp.zeros(carry.shape, F32)

    for b in range(nbatch):
        for k in range(tiles):
            tbuf[k, pl.ds(b, steps, stride=nbatch), :] = u_ref[:, b * d_ssm + k * LANES:b * d_ssm + (k + 1) * LANES]
    u = jnp.concatenate([tbuf[k] for k in range(tiles)], axis=1)
    ub = u.astype(BF16)
    ci, si = u.shape[1] // S5_IN_TILES, gp // S5_IN_TILES
    for c in range(S5_IN_TILES):
        bu = _dot(ub[:, c * ci:(c + 1) * ci], bc_ref[c])
        hbuf[:, c * si:(c + 1) * si] = bu[:, :si]
        hbuf[:, gp + c * si:gp + (c + 1) * si] = bu[:, si:]

    half = gp // 2
    lower = lax.broadcasted_iota(jnp.int32, (SUBLANES, half), 0) < nbatch
    halves = lambda v: jnp.where(lower, jnp.broadcast_to(v[:, :half], (SUBLANES, half)),
                                 jnp.broadcast_to(v[:, half:], (SUBLANES, half)))
    ar, ai = halves(ar_ref[...]), halves(ai_ref[...])
    swap = lambda v: pltpu.roll(v, nbatch, 0)

    def step(i, st):
        hr, hi = st
        rows8 = pl.ds(pl.multiple_of(i * SUBLANES, SUBLANES), SUBLANES)
        new = []
        for part in range(2):
            a = hbuf[rows8, part * gp:part * gp + half]
            b = hbuf[rows8, part * gp + half:(part + 1) * gp]
            new.append((jnp.where(lower, a, swap(b)), jnp.where(lower, swap(a), b)))
        (x0r, x1r), (x0i, x1i) = new
        n0r = ar * hr - ai * hi + x0r
        n0i = ar * hi + ai * hr + x0i
        n1r = ar * n0r - ai * n0i + x1r
        n1i = ar * n0i + ai * n0r + x1i
        for part, (n0, n1) in enumerate(((n0r, n1r), (n0i, n1i))):
            hbuf[rows8, part * gp:part * gp + half] = jnp.where(lower, n0, swap(n1))
            hbuf[rows8, part * gp + half:(part + 1) * gp] = jnp.where(lower, swap(n0), n1)
        return n1r, n1i

    hr, hi = lax.fori_loop(0, rows // SUBLANES, step, (carry[:, :half], carry[:, half:]))
    carry[:, :half] = hr
    carry[:, half:] = hi
    hlast_ref[:, :half] = hr
    hlast_ref[:, half:] = hi
    so = gp // S5_OUT_TILES
    ch = jnp.concatenate(
        [_dot(hbuf[:, o * so:(o + 1) * so].astype(BF16), cc_ref[o, 0])
         + _dot(hbuf[:, gp + o * so:gp + (o + 1) * so].astype(BF16), cc_ref[o, 1]) for o in range(S5_OUT_TILES)],
        axis=1)
    y = _s5_output(ch, u, d_ref, wg_ref, bg_ref)
    for k in range(tiles):
        tbuf[k] = y[:, k * LANES:(k + 1) * LANES]
    for b in range(nbatch):
        for k in range(tiles):
            y_ref[:, b * d_ssm + k * LANES:b * d_ssm + (k + 1) * LANES] = tbuf[k, pl.ds(b, steps, stride=nbatch), :]


def _s5_prompt(u_tb, nbatch, ar, ai, bmat, cmat, d, w_glu, b_glu, *, t_chunk):
    l, width = u_tb.shape
    d_ssm = width // nbatch
    gp = ar.shape[1]
    bmat, cmat = _s5_compact(bmat, cmat)
    rows = t_chunk * nbatch
    assert 2 * nbatch == SUBLANES and l % t_chunk == 0 and d_ssm % LANES == 0
    return pl.pallas_call(
        functools.partial(_s5_prompt_body, nbatch=nbatch),
        grid=(l // t_chunk,),
        in_specs=[pl.BlockSpec((t_chunk, width), lambda i: (i, 0)), _full((1, gp)), _full((1, gp)),
                  _full(bmat.shape), _full(cmat.shape), _full((1, d_ssm)), _full(w_glu.shape),
                  _full((1, d_ssm))],
        out_specs=[pl.BlockSpec((t_chunk, width), lambda i: (i, 0)), _full((SUBLANES, gp))],
        out_shape=[jax.ShapeDtypeStruct((l, width), F32),
                   jax.ShapeDtypeStruct((SUBLANES, gp), F32)],
        scratch_shapes=[pltpu.VMEM((rows, 2 * gp), F32), pltpu.VMEM((SUBLANES, gp), F32),
                        pltpu.VMEM((d_ssm // LANES, rows, LANES), F32)],
        compiler_params=_params("arbitrary"),
        name="s5_prompt",
    )(u_tb, ar, ai, bmat, cmat, d, w_glu, b_glu)


def _s5_sample_body(u_ref, h0r_ref, h0i_ref, ar_ref, ai_ref, bmat_ref, cmat_ref, d_ref, wg_ref, bg_ref,
                    y_ref, hr_ref, hi_ref):
    gp = ar_ref.shape[1]
    u = u_ref[...]
    bu = _dot(u.astype(BF16), bmat_ref[...])
    ar, ai = ar_ref[...], ai_ref[...]
    h0r, h0i = h0r_ref[...], h0i_ref[...]
    hr = bu[:, :gp] + ar * h0r - ai * h0i
    hi = bu[:, gp:] + ar * h0i + ai * h0r
    hr_ref[...] = hr
    hi_ref[...] = hi
    h_bf = jnp.concatenate([hr, hi], axis=1).astype(BF16)
    y_ref[...] = _s5_output(_dot(h_bf, cmat_ref[...]), u, d_ref, wg_ref, bg_ref)


def _s5_sample(u, h0r, h0i, ar, ai, bmat, cmat, d, w_glu, b_glu):
    n, d_ssm = u.shape
    gp = ar.shape[1]
    args = (u, h0r, h0i, ar, ai, bmat, cmat, d, w_glu, b_glu)
    return pl.pallas_call(
        _s5_sample_body,
        grid=(1,),
        in_specs=[_full(a.shape) for a in args],
        out_specs=[_full((n, d_ssm)), _full((n, gp)), _full((n, gp))],
        out_shape=[jax.ShapeDtypeStruct((n, d_ssm), F32), jax.ShapeDtypeStruct((n, gp), F32),
                   jax.ShapeDtypeStruct((n, gp), F32)],
        compiler_params=_params("arbitrary"),
        name="s5_sample",
    )(*args)


def _top_blocks(gate, brow, n_valid):
    bf = brow.astype(F32)
    g = jnp.where(brow < n_valid, gate, NEG_INF)
    sel = jnp.zeros(gate.shape, jnp.bool_)
    for _ in range(MOBA_TOPK):
        mx = jnp.max(g, axis=0, keepdims=True)
        first = jnp.min(jnp.where(g == mx, bf, float(gate.shape[0])), axis=0, keepdims=True)
        pick = bf == first
        sel = jnp.logical_or(sel, pick)
        g = jnp.where(pick, -jnp.inf, g)
    return jnp.logical_and(sel, brow < n_valid)


def _moba_prompt_body(q_ref, kaug_ref, vtb_ref, sums_ref, kn_ref, o_ref, qaug, acc, s_scr, p_scr, *, n_heads):
    j = pl.program_id(1)
    blk = MOBA_BLOCK
    dh = ATTN_HEAD_DIM
    ka = dh + AUX_LANES
    nb = sums_ref.shape[1]
    t0 = pl.multiple_of(j * blk, blk)
    scale = dh ** -0.5
    qt = q_ref[0].T
    brow = lax.broadcasted_iota(jnp.int32, (MAX_BLOCKS, blk), 0)
    crow = lax.broadcasted_iota(jnp.int32, (AUX_BLOCK, blk), 0)
    t0f = jnp.broadcast_to(t0.astype(F32), (AUX_BLOCK, blk))
    kmax2 = jnp.max(kn_ref[0], axis=0, keepdims=True)
    hlane = lax.broadcasted_iota(jnp.int32, kmax2.shape, 1)
    tq = lax.broadcasted_iota(jnp.int32, (1, blk), 1).astype(F32)
    shifts, spreads = [], []
    for h in range(n_heads):
        slope = _alibi_slope(h, n_heads)
        qth = qt[h * dh:(h + 1) * dh]
        means = sums_ref[0, :, h * dh:(h + 1) * dh] * (1.0 / blk)
        gate = jnp.dot(means, qth, precision=HIGHEST, preferred_element_type=F32)
        if nb < MAX_BLOCKS:
            gate = jnp.concatenate([gate, jnp.full((MAX_BLOCKS - nb, blk), NEG_INF, F32)], axis=0)
        keep = jnp.logical_or(_top_blocks(gate, brow, j), brow == j)
        mask = jnp.where(keep, 0.0, NEG_INF)
        hi, mid, lo = _bf16_split3((-slope * LOG2E) * t0f)
        const = jnp.where(crow < AUX_ONE, 1.0, jnp.where(crow == AUX_ONE, hi, jnp.where(
            crow == AUX_ONE + 1, mid, jnp.where(crow == AUX_ONE + 2, lo, 0.0))))
        rest = jnp.zeros((AUX_LANES - AUX_BLOCK - MAX_BLOCKS, blk), F32)
        qaug[h] = jnp.concatenate([qth * (scale * LOG2E), const, mask, rest], axis=0).astype(BF16)
        k2 = jnp.max(jnp.where(hlane == h, kmax2, 0.0), axis=1, keepdims=True)
        bound = jnp.sqrt(jnp.sum(qth * qth, axis=0, keepdims=True) * k2) * (scale * BOUND_SLACK)
        shifts.append((bound + slope * tq) * LOG2E)
        spreads.append(bound)
    shift = jnp.concatenate(shifts, axis=0)
    safe = 2.0 * jnp.max(jnp.concatenate(spreads, axis=0)) < MAX_SPREAD

    def scores(n, h):
        r = pl.multiple_of(n * blk, blk)
        return _dot(kaug_ref[0, pl.ds(r, blk), h * ka:(h + 1) * ka], qaug[h])

    causal = (lax.broadcasted_iota(jnp.int32, (blk, blk), 0) <= lax.broadcasted_iota(jnp.int32, (blk, blk), 1))

    def finish(l_all):
        out_t = jnp.concatenate([acc[h * dh:(h + 1) * dh, :] / l_all[h:h + 1] for h in range(n_heads)], axis=0)
        o_ref[0] = out_t.T

    @pl.when(safe)
    def _():
        def consume(n, l_all, get_scores, own=False):
            ls = []
            for h in range(n_heads):
                s = get_scores(h)
                if own:
                    s = jnp.where(causal, s, NEG_INF)
                p = jnp.exp2(s - shift[h:h + 1])
                p_scr[h] = p.astype(BF16)
                ls.append(jnp.sum(p, axis=0, keepdims=True))
            for h in range(n_heads):
                hs = slice(h * dh, (h + 1) * dh)
                pv = _dot(vtb_ref[0, h, n], p_scr[h])
                acc[hs, :] = pv if own else acc[hs, :] + pv
            l_new = jnp.concatenate(ls, axis=0)
            return l_new if own else l_all + l_new

        l_all = consume(j, None, lambda h: scores(j, h), own=True)

        def group(i, l_all):
            for r in range(PAST_UNROLL):
                n = i * PAST_UNROLL + r
                l_all = consume(n, l_all, lambda h: scores(n, h))
            return l_all

        done = (j // PAST_UNROLL) * PAST_UNROLL
        l_all = lax.fori_loop(0, j // PAST_UNROLL, group, l_all)
        finish(lax.fori_loop(done, j, lambda n, l_all: consume(n, l_all, lambda h: scores(n, h)), l_all))

    @pl.when(jnp.logical_not(safe))
    def _():
        def tile(n, st, own):
            ms, ls, alphas = [], [], []
            for h in range(n_heads):
                s_scr[h] = scores(n, h)
            for h in range(n_heads):
                s = s_scr[h]
                if own:
                    s = jnp.where(causal, s, NEG_INF)
                    m_new = jnp.max(s, axis=0, keepdims=True)
                else:
                    m_new = jnp.maximum(st[0][h:h + 1], jnp.max(s, axis=0, keepdims=True))
                    alphas.append(jnp.exp2(st[0][h:h + 1] - m_new))
                p = jnp.exp2(s - m_new)
                p_scr[h] = p.astype(BF16)
                ms.append(m_new)
                l_new = jnp.sum(p, axis=0, keepdims=True)
                ls.append(l_new if own else alphas[h] * st[1][h:h + 1] + l_new)
            for h in range(n_heads):
                hs = slice(h * dh, (h + 1) * dh)
                pv = _dot(vtb_ref[0, h, n], p_scr[h])
                acc[hs, :] = pv if own else alphas[h] * acc[hs, :] + pv
            return jnp.concatenate(ms, axis=0), jnp.concatenate(ls, axis=0)

        finish(lax.fori_loop(0, j, lambda n, st: tile(n, st, False), tile(j, None, True))[1])


def _moba_prompt(q, kaug, vtb, sums, knorm):
    b, l, d = q.shape
    nb = l // MOBA_BLOCK
    n_heads = d // ATTN_HEAD_DIM
    ka = ATTN_HEAD_DIM + AUX_LANES
    once = pl.Buffered(1)
    return pl.pallas_call(
        functools.partial(_moba_prompt_body, n_heads=n_heads),
        grid=(b, nb),
        in_specs=[pl.BlockSpec((1, MOBA_BLOCK, d), lambda i, j: (i, j, 0)),
                  pl.BlockSpec((1, l, n_heads * ka), lambda i, j: (i, 0, 0), pipeline_mode=once),
                  pl.BlockSpec((1, n_heads, nb, ATTN_HEAD_DIM, MOBA_BLOCK), lambda i, j: (i, 0, 0, 0, 0),
                               pipeline_mode=once),
                  pl.BlockSpec((1, nb, d), lambda i, j: (i, 0, 0)),
                  pl.BlockSpec((1,) + knorm.shape[1:], lambda i, j: (i, 0, 0))],
        out_specs=pl.BlockSpec((1, MOBA_BLOCK, d), lambda i, j: (i, j, 0)),
        out_shape=jax.ShapeDtypeStruct((b, l, d), F32),
        scratch_shapes=[pltpu.VMEM((n_heads, ka, MOBA_BLOCK), BF16), pltpu.VMEM((d, MOBA_BLOCK), F32),
                        pltpu.VMEM((n_heads, MOBA_BLOCK, MOBA_BLOCK), F32),
                        pltpu.VMEM((n_heads, MOBA_BLOCK, MOBA_BLOCK), BF16)],
        compiler_params=_params("parallel", "arbitrary"),
        name="moba_prompt",
    )(q, kaug, vtb, sums, knorm)


def _moba_select_body(ps_ref, idx_ref, *, pages_per_block, n_blocks):
    ps = ps_ref[...]
    lanes = ps.shape[1]
    lane = lax.broadcasted_iota(jnp.int32, (1, lanes), 1)
    g = ps
    for i in range(1, pages_per_block):
        g = g + pltpu.roll(ps, lanes - i, 1)
    g = g * (1.0 / MOBA_BLOCK)
    is_block = jnp.logical_and(lane % pages_per_block == 0, lane < n_blocks * pages_per_block)
    g = jnp.where(is_block, g, -jnp.inf)
    lanef = lane.astype(F32)
    out = jnp.zeros(idx_ref.shape, F32)
    olane = lax.broadcasted_iota(jnp.int32, idx_ref.shape, 1)
    for r in range(MOBA_TOPK):
        mx = jnp.max(g, axis=1, keepdims=True)
        first = jnp.min(jnp.where(g == mx, lanef, float(lanes)), axis=1, keepdims=True)
        out = jnp.where(olane == r, first * (1.0 / pages_per_block), out)
        g = jnp.where(lanef == first, -jnp.inf, g)
    idx_ref[...] = out.astype(jnp.int32)


def _moba_select(page_sums2, *, pages_per_block, n_blocks):
    rows, lanes = page_sums2.shape
    return pl.pallas_call(
        functools.partial(_moba_select_body, pages_per_block=pages_per_block, n_blocks=n_blocks),
        grid=(1,),
        in_specs=[_full((rows, lanes))],
        out_specs=_full((rows, LANES)),
        out_shape=jax.ShapeDtypeStruct((rows, LANES), jnp.int32),
        compiler_params=_params("arbitrary"),
        name="moba_sample_select",
    )(page_sums2)


def _moba_attend_body(blk_ref, pg_ref, q_ref, kn_ref, vn_ref, *refs, pages_per_block, n_heads, past_len):
    del pg_ref
    n_lg = MOBA_TOPK
    n_v = MOBA_TOPK * pages_per_block
    lg_refs, v_refs, o_ref = refs[:n_heads * n_lg], refs[n_heads * n_lg:n_heads * (n_lg + n_v)], refs[-1]
    b = pl.program_id(0)
    dh = ATTN_HEAD_DIM
    blk = MOBA_BLOCK
    page = blk // pages_per_block
    scale = dh ** -0.5
    lane = lax.broadcasted_iota(jnp.int32, (1, blk), 1)
    qk = q_ref[0] * kn_ref[0]
    outs = []
    for h in range(n_heads):
        hs = slice(h * dh, (h + 1) * dh)
        slope = _alibi_slope(h, n_heads)
        s_own = jnp.sum(qk[:, hs], axis=1, keepdims=True) * scale
        ss = []
        for i in range(n_lg):
            pos = blk_ref[b, h, i] * blk + lane
            ss.append(lg_refs[h * n_lg + i][0, 0, h:h + 1, :] * scale - slope * (past_len - pos).astype(F32))
        m = s_own
        for s in ss:
            m = jnp.maximum(m, jnp.max(s, axis=1, keepdims=True))
        p_own = jnp.exp(s_own - m)
        l = p_own
        acc = p_own * vn_ref[0][:, hs]
        for i in range(n_lg):
            p = jnp.exp(ss[i] - m)
            l = l + jnp.sum(p, axis=1, keepdims=True)
            for c in range(pages_per_block):
                vt = v_refs[h * n_v + i * pages_per_block + c][0, 0, 0].astype(BF16)
                acc = acc + _dot_t(p[:, c * page:(c + 1) * page].astype(BF16), vt)
        outs.append(acc / l)
    o_ref[0] = jnp.concatenate(outs, axis=1)


def _moba_attend(blocks, pages_idx, q3, kn3, vn3, logits, cache_vt, *, pages_per_block, past_len):
    n, _, d = q3.shape
    _, _, n_heads, dh, page = cache_vt.shape
    row = pl.BlockSpec((1, 1, d), lambda b, bl, pg: (b, 0, 0))
    lg_spec = lambda h, i: pl.BlockSpec(
        (1, 1, n_heads, MOBA_BLOCK), lambda b, bl, pg: (b, bl[b, h, i], 0, 0))
    v_spec = lambda h, i: pl.BlockSpec(
        (1, 1, 1, dh, page), lambda b, bl, pg: (0, pg[b, h, i], h, 0, 0))
    n_v = MOBA_TOPK * pages_per_block
    lg_specs = [lg_spec(h, i) for h in range(n_heads) for i in range(MOBA_TOPK)]
    v_specs = [v_spec(h, i) for h in range(n_heads) for i in range(n_v)]
    grid_spec = pltpu.PrefetchScalarGridSpec(
        num_scalar_prefetch=2,
        grid=(n,),
        in_specs=[row, row, row] + lg_specs + v_specs,
        out_specs=row)
    return pl.pallas_call(
        functools.partial(_moba_attend_body, pages_per_block=pages_per_block, n_heads=n_heads,
                          past_len=past_len),
        grid_spec=grid_spec,
        out_shape=jax.ShapeDtypeStruct((n, 1, d), F32),
        compiler_params=_params("parallel"),
        name="moba_sample_attend",
    )(blocks, pages_idx, q3, kn3, vn3, *([logits] * len(lg_specs)), *([cache_vt] * len(v_specs)))


def _moba_sample(q, k_new, v_new, logits, page_sums, cache_vt, page_table):
    n, d = q.shape
    _, _, n_heads, dh, page = cache_vt.shape
    n_pages = page_table.shape[1]
    past_len = n_pages * page
    pages_per_block = MOBA_BLOCK // page
    n_blocks = n_pages // pages_per_block
    assert n_blocks >= MOBA_TOPK and n_pages % LANES == 0
    q3 = q.reshape(n, 1, d)
    blocks = _moba_select(page_sums.reshape(n * n_heads, n_pages), pages_per_block=pages_per_block,
                          n_blocks=n_blocks)[:, :MOBA_TOPK].reshape(n, n_heads, MOBA_TOPK)
    logical = (blocks[..., None] * pages_per_block + jnp.arange(pages_per_block)).reshape(n, n_heads, -1)
    pages_idx = jnp.take_along_axis(page_table[:, None, :], logical, axis=2)
    y = _moba_attend(blocks, pages_idx, q3, k_new.reshape(n, 1, d), v_new.reshape(n, 1, d),
                     logits, cache_vt,
                     pages_per_block=pages_per_block, past_len=past_len)
    return y.reshape(n, d)


def _memkv_body(m_ref, g_ref, wk_ref, wv_ref, k_ref, v_ref):
    h = _rms(m_ref[...], g_ref[...]).astype(BF16)
    k_ref[...] = _dot(h, wk_ref[...])
    v_ref[...] = _dot(h, wv_ref[...])


def _memkv(mem, g, wk, wv, *, tm):
    n, d = mem.shape
    row = pl.BlockSpec((tm, d), lambda i: (i, 0))
    return pl.pallas_call(
        _memkv_body,
        grid=(n // tm,),
        in_specs=[row, _full((1, d)), _full(wk.shape), _full(wv.shape)],
        out_specs=[row, row],
        out_shape=[jax.ShapeDtypeStruct((n, d), F32)] * 2,
        compiler_params=_params("parallel"),
        name="mem_kv",
    )(mem, g, wk, wv)


def _mix_out(x, ys, ya, gs, ga, w_out_ref, d_ssm):
    ysn = _rms(ys, gs).astype(BF16)
    yan = _rms(ya, ga).astype(BF16)
    return x + _dot(ysn, w_out_ref[:d_ssm, :]) + _dot(yan, w_out_ref[d_ssm:, :])


def _xattn_heads(q, mk_ref, mv_ref):
    d = q.shape[1]
    xd = d // N_XHEADS
    outs = []
    for h in range(N_XHEADS):
        hs = slice(h * xd, (h + 1) * xd)
        s = _dot_t(q[:, hs].astype(BF16), mk_ref[0, :, hs].astype(BF16)) * (xd ** -0.5)
        m = jnp.max(s, axis=1, keepdims=True)
        p = jnp.exp(s - m)
        l = jnp.sum(p, axis=1, keepdims=True)
        outs.append(_dot(p.astype(BF16), mv_ref[0, :, hs].astype(BF16)) / l)
    return jnp.concatenate(outs, axis=1)


def _merge_prompt_body(x_ref, ys_ref, ya_ref, gs_ref, ga_ref, wout_ref, gx_ref, wq_ref, mk_ref, mv_ref,
                       wo_ref, o_ref, *, d_ssm):
    x2 = _mix_out(x_ref[0], ys_ref[...], ya_ref[0], gs_ref[...], ga_ref[...], wout_ref, d_ssm)
    q = _dot(_rms(x2, gx_ref[...]).astype(BF16), wq_ref[...])
    o = _xattn_heads(q, mk_ref, mv_ref)
    o_ref[0] = x2 + _dot(o.astype(BF16), wo_ref[...])


def _merge_prompt(x, ys, ya, gs, ga, w_out, gx, wq, mk, mv, wo, *, tm):
    b, l, d = x.shape
    d_ssm = ys.shape[1] // b
    d_attn = ya.shape[2]
    n_mem = mk.shape[1]
    row = lambda w: pl.BlockSpec((1, tm, w), lambda i, j: (i, j, 0))
    mem = pl.BlockSpec((1, n_mem, d), lambda i, j: (i, 0, 0))
    return pl.pallas_call(
        functools.partial(_merge_prompt_body, d_ssm=d_ssm),
        grid=(b, l // tm),
        in_specs=[row(d), pl.BlockSpec((tm, d_ssm), lambda i, j: (j, i)), row(d_attn), _full((1, d_ssm)),
                  _full((1, d_attn)), _full(w_out.shape), _full((1, d)), _full(wq.shape), mem, mem,
                  _full(wo.shape)],
        out_specs=row(d),
        out_shape=jax.ShapeDtypeStruct((b, l, d), F32),
        compiler_params=_params("parallel", "parallel"),
        name="merge_prompt",
    )(x, ys, ya, gs, ga, w_out, gx, wq, mk, mv, wo)


def _merge_pre_body(x_ref, ys_ref, ya_ref, gs_ref, ga_ref, wout_ref, gx_ref, wq_ref, x2_ref, q_ref, *, d_ssm):
    x2 = _mix_out(x_ref[...], ys_ref[...], ya_ref[...], gs_ref[...], ga_ref[...], wout_ref, d_ssm)
    x2_ref[...] = x2
    q_ref[...] = _dot(_rms(x2, gx_ref[...]).astype(BF16), wq_ref[...])


def _merge_pre(x, ys, ya, gs, ga, w_out, gx, wq):
    n, d = x.shape
    args = (x, ys, ya, gs, ga, w_out, gx, wq)
    return pl.pallas_call(
        functools.partial(_merge_pre_body, d_ssm=ys.shape[1]),
        grid=(1,),
        in_specs=[_full(a.shape) for a in args],
        out_specs=[_full((n, d)), _full((n, d))],
        out_shape=[jax.ShapeDtypeStruct((n, d), F32)] * 2,
        compiler_params=_params("arbitrary"),
        name="merge_sample_pre",
    )(*args)


def _xattn_sample_body(q_ref, mk_ref, mv_ref, o_ref):
    q = q_ref[0]
    n_mem, n_heads, xd = mk_ref.shape[1:]
    q4 = jnp.concatenate([q[:, h * xd:(h + 1) * xd] for h in range(n_heads)], axis=0).astype(BF16)
    k2 = mk_ref[0].reshape(n_mem * n_heads, xd).astype(BF16)
    v2 = mv_ref[0].reshape(n_mem * n_heads, xd).astype(BF16)
    s = _dot_t(q4, k2) * (xd ** -0.5)
    own = lax.broadcasted_iota(jnp.int32, s.shape, 1) % n_heads == lax.broadcasted_iota(jnp.int32, s.shape, 0)
    s = jnp.where(own, s, NEG_INF)
    m = jnp.max(s, axis=1, keepdims=True)
    p = jnp.exp(s - m)
    o4 = _dot(p.astype(BF16), v2) / jnp.sum(p, axis=1, keepdims=True)
    o_ref[0] = jnp.concatenate([o4[h:h + 1] for h in range(n_heads)], axis=1)


def _xattn_sample(q3, mk, mv):
    n, _, d = q3.shape
    row = pl.BlockSpec((1, 1, d), lambda i: (i, 0, 0))
    mem = pl.BlockSpec((1,) + mk.shape[1:], lambda i: (i, 0, 0, 0))
    return pl.pallas_call(
        _xattn_sample_body,
        grid=(n,),
        in_specs=[row, mem, mem],
        out_specs=row,
        out_shape=jax.ShapeDtypeStruct((n, 1, d), F32),
        compiler_params=_params("parallel"),
        name="xattn_sample",
    )(q3, mk, mv)


def _merge_post_body(x_ref, o_ref, wo_ref, y_ref):
    y_ref[...] = x_ref[...] + _dot(o_ref[...].astype(BF16), wo_ref[...])


def _merge_post(x2, o, wo):
    n, d = x2.shape
    return pl.pallas_call(
        _merge_post_body,
        grid=(1,),
        in_specs=[_full((n, d)), _full((n, d)), _full(wo.shape)],
        out_specs=_full((n, d)),
        out_shape=jax.ShapeDtypeStruct((n, d), F32),
        compiler_params=_params("arbitrary"),
        name="merge_sample_post",
    )(x2, o, wo)


def kernel(x_prompt, x_sample, mem_prompt, cache_k, cache_v, page_table, state_ssm_re, state_ssm_im, cache_mem_k, cache_mem_v, g_ffn1, w1_ffn1, w3_ffn1, w2_ffn1, g_mix, w_in, ssm_a_re, ssm_a_im, ssm_log_dt, ssm_b_re, ssm_b_im, ssm_c_re, ssm_c_im, ssm_d, w_glu, b_glu, g_out_ssm, g_out_attn, w_out, g_xattn, g_mem, wq_x, wk_x, wv_x, wo_x, g_ffn2, w1_ffn2, w3_ffn2, w2_ffn2, g_final):
    depth = g_ffn1.shape[0]
    assert depth == 1
    b, l, d = x_prompt.shape
    ns, ls, _ = x_sample.shape
    assert ls == 1
    n_groups, n_state = ssm_a_re.shape[1:]
    gp = n_groups * n_state
    d_ssm = n_groups * SSM_GROUP
    d_attn = (w_in.shape[2] - d_ssm) // 3
    n_heads = d_attn // ATTN_HEAD_DIM
    n_mem = mem_prompt.shape[1]
    xd = d // N_XHEADS

    vec = lambda a: a[0].reshape(1, -1)
    wb = lambda a: a[0].astype(BF16)
    w1a, w3a, w2a = wb(w1_ffn1), wb(w3_ffn1), wb(w2_ffn1)
    w1b, w3b, w2b = wb(w1_ffn2), wb(w3_ffn2), wb(w2_ffn2)
    w_in_b, w_out_b, w_glu_b = wb(w_in), wb(w_out), wb(w_glu)
    wq_b, wk_b, wv_b, wo_b = wb(wq_x), wb(wk_x), wb(wv_x), wb(wo_x)
    gfin = g_final.reshape(1, -1)
    ar, ai, bmat, cmat = _s5_matrices(ssm_a_re[0], ssm_a_im[0], ssm_log_dt[0], ssm_b_re[0], ssm_b_im[0],
                                      ssm_c_re[0], ssm_c_im[0])
    s5_w = (ar, ai, bmat, cmat, vec(ssm_d), w_glu_b, vec(b_glu))

    xs = x_sample.reshape(ns, d)
    xs1 = _ffn(xs, vec(g_ffn1), w1_ffn1[0], w3_ffn1[0], w2_ffn1[0], gfin, final_norm=False, tm=ns)
    us, qs, ks, vs = _proj(xs1, vec(g_mix), w_in[0], d_ssm=d_ssm, tm=ns)
    q_t = qs.reshape(ns, n_heads, ATTN_HEAD_DIM).swapaxes(1, 2)
    cache_kt = cache_k.transpose(0, 1, 3, 4, 2)
    cache_vt = cache_v.transpose(0, 1, 3, 4, 2)
    half_pages = page_table.shape[1] // 2

    tm = 512 if l % 512 == 0 else l
    xp = x_prompt.reshape(b * l, d)
    mem_k, mem_v = _memkv(mem_prompt.reshape(b * n_mem, d), vec(g_mem), wk_b, wv_b,
                          tm=min(256, b * n_mem))
    x1, lg_a, ps_a = _ffn_kpass(xp, vec(g_ffn1), w1a, w3a, w2a, gfin, page_table, q_t, cache_kt,
                                final_norm=False, tm=tm, first_page=0, n_call_pages=half_pages)
    x1 = x1.reshape(b, l, d)
    tmw = 1024 if l % 1024 == 0 else tm
    u, q, k_t, v_t, kaug, vtb, sums, knorm = _proj_prompt(x1, vec(g_mix), w_in_b, d_ssm=d_ssm, tm=tmw)
    t_chunk = 256 if l % 256 == 0 else l
    ys, h_last = _s5_prompt(u, b, *s5_w, t_chunk=t_chunk)
    ya = _moba_prompt(q, kaug, vtb, sums.reshape(b, l // MOBA_BLOCK, d_attn), knorm.reshape(b, l // tmw, LANES))
    x3 = _merge_prompt(x1, ys, ya, vec(g_out_ssm), vec(g_out_attn), w_out_b, vec(g_xattn),
                       wq_b, mem_k.reshape(b, n_mem, d), mem_v.reshape(b, n_mem, d), wo_b, tm=tmw)
    y_prompt, lg_b, ps_b = _ffn_kpass(x3.reshape(b * l, d), vec(g_ffn2), w1b, w3b, w2b, gfin, page_table, q_t,
                                      cache_kt, final_norm=True, tm=tm, first_page=half_pages,
                                      n_call_pages=half_pages)
    h_last = h_last.reshape(2, b, 2, n_groups // 2, n_state).transpose(2, 1, 0, 3, 4).reshape(2, b, gp)
    k = k_t.transpose(0, 3, 1, 2)
    v = v_t.transpose(0, 3, 1, 2)
    heads = (1, b, l, n_heads, ATTN_HEAD_DIM)
    state = (1, b, n_groups, n_state)
    memkv = (1, b, n_mem, N_XHEADS, xd)

    yss, hrs, his = _s5_sample(us, state_ssm_re[0].reshape(ns, gp), state_ssm_im[0].reshape(ns, gp), *s5_w)
    yas = _moba_sample(qs, ks, vs, jnp.concatenate([lg_a, lg_b], axis=1), ps_a + ps_b, cache_vt, page_table)
    xs2, qx = _merge_pre(xs1, yss, yas, vec(g_out_ssm), vec(g_out_attn), w_out_b, vec(g_xattn), wq_b)
    ox = _xattn_sample(qx.reshape(ns, 1, d), cache_mem_k[0], cache_mem_v[0])
    xs3 = _merge_post(xs2, ox.reshape(ns, d), wo_b)
    y_sample = _ffn(xs3, vec(g_ffn2), w1b, w3b, w2b, gfin, final_norm=True, tm=ns)
    sheads = (1, ns, 1, n_heads, ATTN_HEAD_DIM)
    sstate = (1, ns, n_groups, n_state)

    return (y_prompt.reshape(b, l, d), y_sample.reshape(ns, 1, d),
            k.reshape(heads), v.reshape(heads),
            h_last[0].reshape(state), h_last[1].reshape(state),
            mem_k.reshape(memkv), mem_v.reshape(memkv),
            ks.reshape(sheads), vs.reshape(sheads),
            hrs.reshape(sstate), his.reshape(sstate))
```

```python
import functools
import math

import jax
import jax.numpy as jnp
from jax import lax
from jax.experimental import pallas as pl
from jax.experimental.pallas import tpu as pltpu

RMS_EPS = 1e-6
NEG_INF = -1e30
SSM_GROUP = 16
ATTN_HEAD_DIM = 64
MOBA_BLOCK = 256
MOBA_TOPK = 3
N_XHEADS = 4
LANES = 128
SUBLANES = 8
VMEM_LIMIT = 56 * 1024 * 1024

F32 = jnp.float32
BF16 = jnp.bfloat16
HIGHEST = lax.Precision.HIGHEST


def _params(*sem):
    return pltpu.CompilerParams(dimension_semantics=sem, vmem_limit_bytes=VMEM_LIMIT)


def _rms(x, g):
    ms = jnp.mean(x * x, axis=-1, keepdims=True)
    return x * lax.rsqrt(ms + RMS_EPS) * g


def _dot(a, b):
    return jnp.dot(a, b, preferred_element_type=F32)


def _dot_t(a, b, precision=None):
    return lax.dot_general(a, b, (((1,), (1,)), ((), ())), preferred_element_type=F32,
                           precision=precision)


def _full(shape):
    n = len(shape)
    return pl.BlockSpec(shape, lambda *_: (0,) * n)


def _const(shape):
    n = len(shape)
    return pl.BlockSpec(shape, lambda *_: (0,) * n, pipeline_mode=pl.Buffered(1))


def _weight_dot(w_dtype):
    if w_dtype == BF16:
        return (lambda v: v.astype(BF16)), _dot
    return (lambda v: v), functools.partial(jnp.dot, precision=HIGHEST, preferred_element_type=F32)


def _ffn_body(x_ref, g_ref, w1_ref, w3_ref, w2_ref, gf_ref, o_ref, *, f_chunk, final_norm):
    lhs, dot = _weight_dot(w1_ref.dtype)
    x = x_ref[...]
    h = lhs(_rms(x, g_ref[...]))
    acc = jnp.zeros(x.shape, F32)
    for c in range(w1_ref.shape[1] // f_chunk):
        cs = slice(c * f_chunk, (c + 1) * f_chunk)
        a = dot(h, w1_ref[:, cs])
        b = dot(h, w3_ref[:, cs])
        act = lhs(a * jax.nn.sigmoid(a) * b)
        acc = acc + dot(act, w2_ref[cs, :])
    y = x + 0.5 * acc
    if final_norm:
        y = _rms(y, gf_ref[...])
    o_ref[...] = y


def _kpass_pages(q_ref, k_refs, lg_ref, ps_ref, *, first, lane0):
    n_heads, dh, page = k_refs[0].shape[2:]
    ppb = MOBA_BLOCK // page
    qcols = q_ref[0]
    lanes = lambda i: slice((i % ppb) * page, (i % ppb + 1) * page)
    for h in range(n_heads):
        qb = jnp.broadcast_to(qcols[:, h:h + 1], (dh, page))
        for i, k_ref in enumerate(k_refs):
            lg_ref[0, i // ppb, h:h + 1, lanes(i)] = jnp.sum(k_ref[0, 0, h] * qb, axis=0, keepdims=True)
    plane = lax.broadcasted_iota(jnp.int32, (n_heads, ps_ref.shape[2]), 1)
    psum = jnp.where(first, 0.0, ps_ref[0])
    for i in range(len(k_refs)):
        psum = jnp.where(plane == lane0 + i, jnp.sum(lg_ref[0, i // ppb, :, lanes(i)], axis=1, keepdims=True), psum)
    ps_ref[0] = psum


def _ffn_kpass_body(pt_ref, x_ref, g_ref, w1_ref, w3_ref, w2_ref, gf_ref, q_ref, *refs,
                    f_chunk, final_norm, pages, first_page, steps_per_seq):
    del pt_ref
    k_refs, (o_ref, lg_ref, ps_ref) = refs[:pages], refs[pages:]
    _ffn_body(x_ref, g_ref, w1_ref, w3_ref, w2_ref, gf_ref, o_ref, f_chunk=f_chunk, final_norm=final_norm)
    part = pl.program_id(0) % steps_per_seq
    _kpass_pages(q_ref, k_refs, lg_ref, ps_ref, first=part == 0, lane0=first_page + part * pages)


def _ffn_kpass(x, g, w1, w3, w2, g_final, page_table, q_t, cache_kt, *, final_norm, tm, first_page, n_call_pages):
    n, d = x.shape
    f = w1.shape[1]
    f_chunk = 256 if f % 256 == 0 else f
    ns, dh, n_heads = q_t.shape
    n_pages = page_table.shape[1]
    page = cache_kt.shape[4]
    ppb = MOBA_BLOCK // page
    steps = n // tm
    steps_per_seq = steps // ns
    pages = n_call_pages // steps_per_seq
    assert steps % ns == 0 and n_call_pages % steps_per_seq == 0 and pages % ppb == 0
    once = pl.Buffered(1)
    const = lambda shape: pl.BlockSpec(shape, lambda i, pt: (0,) * len(shape), pipeline_mode=once)
    step_pages = page_table[:, first_page:first_page + n_call_pages].reshape(steps, pages)
    k_spec = lambda ip: pl.BlockSpec((1, 1, n_heads, dh, page), lambda i, pt: (0, pt[i, ip], 0, 0, 0))
    grid_spec = pltpu.PrefetchScalarGridSpec(
        num_scalar_prefetch=1,
        grid=(steps,),
        in_specs=[pl.BlockSpec((tm, d), lambda i, pt: (i, 0)), const((1, d)), const((d, f)), const((d, f)),
                  const((f, d)), const((1, d)),
                  pl.BlockSpec((1, dh, n_heads), lambda i, pt: (i // steps_per_seq, 0, 0))]
                 + [k_spec(ip) for ip in range(pages)],
        out_specs=[pl.BlockSpec((tm, d), lambda i, pt: (i, 0)),
                   pl.BlockSpec((1, pages // ppb, n_heads, MOBA_BLOCK),
                                lambda i, pt: (i // steps_per_seq, i % steps_per_seq, 0, 0)),
                   pl.BlockSpec((1, n_heads, n_pages), lambda i, pt: (i // steps_per_seq, 0, 0))])
    return pl.pallas_call(
        functools.partial(_ffn_kpass_body, f_chunk=f_chunk, final_norm=final_norm, pages=pages,
                          first_page=first_page, steps_per_seq=steps_per_seq),
        grid_spec=grid_spec,
        out_shape=[jax.ShapeDtypeStruct((n, d), F32),
                   jax.ShapeDtypeStruct((ns, n_call_pages // ppb, n_heads, MOBA_BLOCK), F32),
                   jax.ShapeDtypeStruct((ns, n_heads, n_pages), F32)],
        compiler_params=_params("arbitrary"),
        name="ffn_final_kpass" if final_norm else "ffn_kpass",
    )(step_pages, x, g, w1, w3, w2, g_final, q_t, *([cache_kt] * pages))


def _ffn(x, g, w1, w3, w2, g_final, *, final_norm, tm):
    n, d = x.shape
    f = w1.shape[1]
    f_chunk = 256 if f % 256 == 0 else f
    return pl.pallas_call(
        functools.partial(_ffn_body, f_chunk=f_chunk, final_norm=final_norm),
        grid=(n // tm,),
        in_specs=[pl.BlockSpec((tm, d), lambda i: (i, 0)), _const((1, d)), _const((d, f)), _const((d, f)),
                  _const((f, d)), _const((1, d))],
        out_specs=pl.BlockSpec((tm, d), lambda i: (i, 0)),
        out_shape=jax.ShapeDtypeStruct((n, d), F32),
        compiler_params=_params("parallel"),
        name="ffn_final" if final_norm else "ffn",
    )(x, g, w1, w3, w2, g_final)


def _proj_body(x_ref, g_ref, w_ref, u_ref, q_ref, k_ref, v_ref, *, d_ssm, d_attn):
    lhs, dot = _weight_dot(w_ref.dtype)
    p = dot(lhs(_rms(x_ref[...], g_ref[...])), w_ref[...])
    u_ref[...] = p[:, :d_ssm]
    q_ref[...] = p[:, d_ssm:d_ssm + d_attn]
    k_ref[...] = p[:, d_ssm + d_attn:d_ssm + 2 * d_attn]
    v_ref[...] = p[:, d_ssm + 2 * d_attn:]


def _proj(x, g, w_in, *, d_ssm, tm):
    n, d = x.shape
    d_attn = (w_in.shape[1] - d_ssm) // 3
    row = lambda w: pl.BlockSpec((tm, w), lambda i: (i, 0))
    return pl.pallas_call(
        functools.partial(_proj_body, d_ssm=d_ssm, d_attn=d_attn),
        grid=(n // tm,),
        in_specs=[row(d), _const((1, d)), _const(w_in.shape)],
        out_specs=[row(d_ssm), row(d_attn), row(d_attn), row(d_attn)],
        out_shape=[jax.ShapeDtypeStruct((n, d_ssm), F32)] + [jax.ShapeDtypeStruct((n, d_attn), F32)] * 3,
        compiler_params=_params("parallel"),
        name="mix_proj",
    )(x, g, w_in)


AUX_BIAS, AUX_ONE, AUX_BLOCK, AUX_LANES, MAX_BLOCKS = 0, 3, 16, 64, 32
LOG2E = math.log2(math.e)
BOUND_SLACK = 1.02
MAX_SPREAD = 60.0
PAST_UNROLL = (8, 4, 1)


def _alibi_slope(h, n_heads):
    return 2.0 ** (-8.0 * (h + 1) / n_heads)


def _bf16_split3(x):
    hi = x.astype(BF16).astype(F32)
    r = x - hi
    mid = r.astype(BF16).astype(F32)
    lo = (r - mid).astype(BF16).astype(F32)
    return hi, mid, lo


def _proj_prompt_body(x_ref, g_ref, w_ref, u_ref, q_ref, kt_ref, vt_ref, kaug_ref, vtb_ref, sums_ref, kn_ref,
                      *, d_ssm, d_attn, n_heads):
    tm = x_ref.shape[1]
    dh = ATTN_HEAD_DIM
    h = _rms(x_ref[0], g_ref[...]).astype(BF16)
    p = _dot(h, w_ref[...])
    u_ref[...] = p[:, :d_ssm]
    q_ref[0] = p[:, d_ssm:d_ssm + d_attn]
    k = p[:, d_ssm + d_attn:d_ssm + 2 * d_attn]
    v = p[:, d_ssm + 2 * d_attn:]
    kt_ref[0] = k.T.reshape(n_heads, dh, tm)
    vt = v.T.reshape(n_heads, dh, tm)
    vt_ref[0] = vt
    for c in range(tm // MOBA_BLOCK):
        cs = slice(c * MOBA_BLOCK, (c + 1) * MOBA_BLOCK)
        vtb_ref[0, :, c] = vt[:, :, cs].astype(BF16)
        sums_ref[0, c:c + 1, :] = jnp.sum(k[cs], axis=0, keepdims=True)
    pos = pl.program_id(1) * tm + lax.broadcasted_iota(jnp.int32, (tm, AUX_LANES), 0)
    lane = lax.broadcasted_iota(jnp.int32, (tm, AUX_LANES), 1)
    block_hot = (lane - AUX_BLOCK == pos // MOBA_BLOCK).astype(F32)
    posf = pos.astype(F32)
    head_of = (lax.broadcasted_iota(jnp.int32, (d_attn, LANES), 0) // dh
               == lax.broadcasted_iota(jnp.int32, (d_attn, LANES), 1))
    kn_ref[0] = jnp.max(_dot((k * k).astype(BF16), head_of.astype(BF16)), axis=0, keepdims=True)
    pieces = []
    for hh in range(n_heads):
        hi, mid, lo = _bf16_split3((_alibi_slope(hh, n_heads) * LOG2E) * posf)
        aux = jnp.where(lane == AUX_BIAS, hi, jnp.where(lane == AUX_BIAS + 1, mid, jnp.where(
            lane == AUX_BIAS + 2, lo, jnp.where(lane < AUX_ONE + 3, 1.0, block_hot))))
        pieces += [k[:, hh * dh:(hh + 1) * dh], aux]
    kaug_ref[0] = jnp.concatenate(pieces, axis=1).astype(BF16)


def _proj_prompt(x, g, w_in, *, d_ssm, tm):
    b, l, d = x.shape
    d_attn = (w_in.shape[1] - d_ssm) // 3
    dh = ATTN_HEAD_DIM
    n_heads = d_attn // dh
    nb = l // MOBA_BLOCK
    tb = tm // MOBA_BLOCK
    assert nb <= MAX_BLOCKS and tm % MOBA_BLOCK == 0
    row = lambda w: pl.BlockSpec((1, tm, w), lambda i, j: (i, j, 0))
    tr = pl.BlockSpec((1, n_heads, dh, tm), lambda i, j: (i, 0, 0, j))
    return pl.pallas_call(
        functools.partial(_proj_prompt_body, d_ssm=d_ssm, d_attn=d_attn, n_heads=n_heads),
        grid=(b, l // tm),
        in_specs=[row(d), _full((1, d)), _full(w_in.shape)],
        out_specs=[pl.BlockSpec((tm, d_ssm), lambda i, j: (j, i)),
                   row(d_attn), tr, tr, row(n_heads * (dh + AUX_LANES)),
                   pl.BlockSpec((1, n_heads, tb, dh, MOBA_BLOCK), lambda i, j: (i, 0, j, 0, 0)),
                   pl.BlockSpec((1, tb, d_attn), lambda i, j: (i * (l // tm) + j, 0, 0)),
                   pl.BlockSpec((1, 1, LANES), lambda i, j: (i * (l // tm) + j, 0, 0))],
        out_shape=[jax.ShapeDtypeStruct((l, b * d_ssm), F32), jax.ShapeDtypeStruct((b, l, d_attn), F32),
                   jax.ShapeDtypeStruct((b, n_heads, dh, l), F32), jax.ShapeDtypeStruct((b, n_heads, dh, l), F32),
                   jax.ShapeDtypeStruct((b, l, n_heads * (dh + AUX_LANES)), BF16),
                   jax.ShapeDtypeStruct((b, n_heads, nb, dh, MOBA_BLOCK), BF16),
                   jax.ShapeDtypeStruct((b * l // tm, tb, d_attn), F32),
                   jax.ShapeDtypeStruct((b * l // tm, 1, LANES), F32)],
        compiler_params=_params("parallel", "parallel"),
        name="mix_proj_prompt",
    )(x, g, w_in)


def _s5_discretise(a_re, a_im, log_dt, b_re, b_im):
    dt = jnp.exp(log_dt)[:, None]
    mag = jnp.exp(a_re * dt)
    abar_re = mag * jnp.cos(a_im * dt)
    abar_im = mag * jnp.sin(a_im * dt)
    den = a_re * a_re + a_im * a_im
    nr = abar_re - 1.0
    f_re = (nr * a_re + abar_im * a_im) / den
    f_im = (abar_im * a_re - nr * a_im) / den
    bbar_re = f_re[..., None] * b_re - f_im[..., None] * b_im
    bbar_im = f_re[..., None] * b_im + f_im[..., None] * b_re
    return abar_re, abar_im, bbar_re, bbar_im


def _s5_matrices(a_re, a_im, log_dt, b_re, b_im, c_re, c_im):
    g, p, c = b_re.shape
    abar_re, abar_im, bbar_re, bbar_im = _s5_discretise(a_re, a_im, log_dt, b_re, b_im)
    eye = jnp.eye(g, dtype=F32)
    bm_re = jnp.einsum('gpc,gh->gchp', bbar_re, eye).reshape(g * c, g * p)
    bm_im = jnp.einsum('gpc,gh->gchp', bbar_im, eye).reshape(g * c, g * p)
    bmat = jnp.concatenate([bm_re, bm_im], axis=1).astype(BF16)
    cm_re = jnp.einsum('gcp,gh->gphc', c_re, eye).reshape(g * p, g * c)
    cm_im = jnp.einsum('gcp,gh->gphc', c_im, eye).reshape(g * p, g * c)
    cmat = jnp.concatenate([cm_re, -cm_im], axis=0).astype(BF16)
    return abar_re.reshape(1, g * p), abar_im.reshape(1, g * p), bmat, cmat


S5_IN_TILES = 8
S5_OUT_TILES = 2


def _s5_compact(bmat, cmat):
    d_ssm, gp2 = bmat.shape
    gp = gp2 // 2
    ci, si = d_ssm // S5_IN_TILES, gp // S5_IN_TILES
    bc = jnp.stack([jnp.concatenate([bmat[c * ci:(c + 1) * ci, c * si:(c + 1) * si],
                                     bmat[c * ci:(c + 1) * ci, gp + c * si:gp + (c + 1) * si]], axis=1)
                    for c in range(S5_IN_TILES)])
    co, so = d_ssm // S5_OUT_TILES, gp // S5_OUT_TILES
    cc = jnp.stack([jnp.stack([cmat[part * gp + o * so:part * gp + (o + 1) * so, o * co:(o + 1) * co]
                               for part in range(2)]) for o in range(S5_OUT_TILES)])
    return bc, cc


def _s5_output(ch, u, d_ref, wg_ref, bg_ref):
    y = ch + d_ref[...] * u
    z = jax.nn.gelu(y, approximate=True)
    return z * jax.nn.sigmoid(_dot(z.astype(BF16), wg_ref[...]) + bg_ref[...])


def _s5_prompt_body(u_ref, ar_ref, ai_ref, bc_ref, cc_ref, d_ref, wg_ref, bg_ref,
                    y_ref, hlast_ref, hbuf, carry, tbuf, *, nbatch):
    gp = ar_ref.shape[1]
    steps = u_ref.shape[0]
    rows = steps * nbatch
    d_ssm = u_ref.shape[1] // nbatch
    tiles = d_ssm // LANES

    @pl.when(pl.program_id(0) == 0)
    def _():
        carry[...] = jnp.zeros(carry.shape, F32)

    for b in range(nbatch):
        for k in range(tiles):
            tbuf[k, pl.ds(b, steps, stride=nbatch), :] = u_ref[:, b * d_ssm + k * LANES:b * d_ssm + (k + 1) * LANES]
    u = jnp.concatenate([tbuf[k] for k in range(tiles)], axis=1)
    ub = u.astype(BF16)
    ci, si = u.shape[1] // S5_IN_TILES, gp // S5_IN_TILES
    for c in range(S5_IN_TILES):
        bu = _dot(ub[:, c * ci:(c + 1) * ci], bc_ref[c])
        hbuf[:, c * si:(c + 1) * si] = bu[:, :si]
        hbuf[:, gp + c * si:gp + (c + 1) * si] = bu[:, si:]

    half = gp // 2
    lower = lax.broadcasted_iota(jnp.int32, (SUBLANES, half), 0) < nbatch
    halves = lambda v: jnp.where(lower, jnp.broadcast_to(v[:, :half], (SUBLANES, half)),
                                 jnp.broadcast_to(v[:, half:], (SUBLANES, half)))
    ar, ai = halves(ar_ref[...]), halves(ai_ref[...])
    swap = lambda v: pltpu.roll(v, nbatch, 0)

    def step(i, st):
        hr, hi = st
        rows8 = pl.ds(pl.multiple_of(i * SUBLANES, SUBLANES), SUBLANES)
        new = []
        for part in range(2):
            a = hbuf[rows8, part * gp:part * gp + half]
            b = hbuf[rows8, part * gp + half:(part + 1) * gp]
            new.append((jnp.where(lower, a, swap(b)), jnp.where(lower, swap(a), b)))
        (x0r, x1r), (x0i, x1i) = new
        n0r = ar * hr - ai * hi + x0r
        n0i = ar * hi + ai * hr + x0i
        n1r = ar * n0r - ai * n0i + x1r
        n1i = ar * n0i + ai * n0r + x1i
        for part, (n0, n1) in enumerate(((n0r, n1r), (n0i, n1i))):
            hbuf[rows8, part * gp:part * gp + half] = jnp.where(lower, n0, swap(n1))
            hbuf[rows8, part * gp + half:(part + 1) * gp] = jnp.where(lower, swap(n0), n1)
        return n1r, n1i

    hr, hi = lax.fori_loop(0, rows // SUBLANES, step, (carry[:, :half], carry[:, half:]))
    carry[:, :half] = hr
    carry[:, half:] = hi
    hlast_ref[:, :half] = hr
    hlast_ref[:, half:] = hi
    so = gp // S5_OUT_TILES
    ch = jnp.concatenate(
        [_dot(hbuf[:, o * so:(o + 1) * so].astype(BF16), cc_ref[o, 0])
         + _dot(hbuf[:, gp + o * so:gp + (o + 1) * so].astype(BF16), cc_ref[o, 1]) for o in range(S5_OUT_TILES)],
        axis=1)
    y = _s5_output(ch, u, d_ref, wg_ref, bg_ref)
    for k in range(tiles):
        tbuf[k] = y[:, k * LANES:(k + 1) * LANES]
    for b in range(nbatch):
        for k in range(tiles):
            y_ref[:, b * d_ssm + k * LANES:b * d_ssm + (k + 1) * LANES] = tbuf[k, pl.ds(b, steps, stride=nbatch), :]


def _s5_prompt(u_tb, nbatch, ar, ai, bmat, cmat, d, w_glu, b_glu, *, t_chunk):
    l, width = u_tb.shape
    d_ssm = width // nbatch
    gp = ar.shape[1]
    bmat, cmat = _s5_compact(bmat, cmat)
    rows = t_chunk * nbatch
    assert 2 * nbatch == SUBLANES and l % t_chunk == 0 and d_ssm % LANES == 0
    return pl.pallas_call(
        functools.partial(_s5_prompt_body, nbatch=nbatch),
        grid=(l // t_chunk,),
        in_specs=[pl.BlockSpec((t_chunk, width), lambda i: (i, 0)), _full((1, gp)), _full((1, gp)),
                  _full(bmat.shape), _full(cmat.shape), _full((1, d_ssm)), _full(w_glu.shape),
                  _full((1, d_ssm))],
        out_specs=[pl.BlockSpec((t_chunk, width), lambda i: (i, 0)), _full((SUBLANES, gp))],
        out_shape=[jax.ShapeDtypeStruct((l, width), F32),
                   jax.ShapeDtypeStruct((SUBLANES, gp), F32)],
        scratch_shapes=[pltpu.VMEM((rows, 2 * gp), F32), pltpu.VMEM((SUBLANES, gp), F32),
                        pltpu.VMEM((d_ssm // LANES, rows, LANES), F32)],
        compiler_params=_params("arbitrary"),
        name="s5_prompt",
    )(u_tb, ar, ai, bmat, cmat, d, w_glu, b_glu)


def _s5_sample_body(u_ref, h0r_ref, h0i_ref, ar_ref, ai_ref, bmat_ref, cmat_ref, d_ref, wg_ref, bg_ref,
                    y_ref, hr_ref, hi_ref):
    gp = ar_ref.shape[1]
    u = u_ref[...]
    bu = _dot(u.astype(BF16), bmat_ref[...])
    ar, ai = ar_ref[...], ai_ref[...]
    h0r, h0i = h0r_ref[...], h0i_ref[...]
    hr = bu[:, :gp] + ar * h0r - ai * h0i
    hi = bu[:, gp:] + ar * h0i + ai * h0r
    hr_ref[...] = hr
    hi_ref[...] = hi
    h_bf = jnp.concatenate([hr, hi], axis=1).astype(BF16)
    y_ref[...] = _s5_output(_dot(h_bf, cmat_ref[...]), u, d_ref, wg_ref, bg_ref)


def _s5_sample(u, h0r, h0i, ar, ai, bmat, cmat, d, w_glu, b_glu):
    n, d_ssm = u.shape
    gp = ar.shape[1]
    args = (u, h0r, h0i, ar, ai, bmat, cmat, d, w_glu, b_glu)
    return pl.pallas_call(
        _s5_sample_body,
        grid=(1,),
        in_specs=[_full(a.shape) for a in args],
        out_specs=[_full((n, d_ssm)), _full((n, gp)), _full((n, gp))],
        out_shape=[jax.ShapeDtypeStruct((n, d_ssm), F32), jax.ShapeDtypeStruct((n, gp), F32),
                   jax.ShapeDtypeStruct((n, gp), F32)],
        compiler_params=_params("arbitrary"),
        name="s5_sample",
    )(*args)


def _top_blocks(gate, brow, n_valid):
    bf = brow.astype(F32)
    g = jnp.where(brow < n_valid, gate, NEG_INF)
    sel = jnp.zeros(gate.shape, jnp.bool_)
    for _ in range(MOBA_TOPK):
        mx = jnp.max(g, axis=0, keepdims=True)
        first = jnp.min(jnp.where(g == mx, bf, float(gate.shape[0])), axis=0, keepdims=True)
        pick = bf == first
        sel = jnp.logical_or(sel, pick)
        g = jnp.where(pick, -jnp.inf, g)
    return jnp.logical_and(sel, brow < n_valid)


def _moba_prompt_body(q_ref, kaug_ref, vtb_ref, sums_ref, kn_ref, o_ref, qaug, acc, s_scr, p_scr, *, n_heads):
    j = pl.program_id(1)
    blk = MOBA_BLOCK
    dh = ATTN_HEAD_DIM
    ka = dh + AUX_LANES
    nb = sums_ref.shape[1]
    t0 = pl.multiple_of(j * blk, blk)
    scale = dh ** -0.5
    qt = q_ref[0].T
    brow = lax.broadcasted_iota(jnp.int32, (MAX_BLOCKS, blk), 0)
    crow = lax.broadcasted_iota(jnp.int32, (AUX_BLOCK, blk), 0)
    t0f = jnp.broadcast_to(t0.astype(F32), (AUX_BLOCK, blk))
    kmax2 = jnp.max(kn_ref[0], axis=0, keepdims=True)
    hlane = lax.broadcasted_iota(jnp.int32, kmax2.shape, 1)
    tq = lax.broadcasted_iota(jnp.int32, (1, blk), 1).astype(F32)
    shifts, spreads = [], []
    for h in range(n_heads):
        slope = _alibi_slope(h, n_heads)
        qth = qt[h * dh:(h + 1) * dh]
        means = sums_ref[0, :, h * dh:(h + 1) * dh] * (1.0 / blk)
        gate = jnp.dot(means, qth, precision=HIGHEST, preferred_element_type=F32)
        if nb < MAX_BLOCKS:
            gate = jnp.concatenate([gate, jnp.full((MAX_BLOCKS - nb, blk), NEG_INF, F32)], axis=0)
        keep = jnp.logical_or(_top_blocks(gate, brow, j), brow == j)
        mask = jnp.where(keep, 0.0, NEG_INF)
        hi, mid, lo = _bf16_split3((-slope * LOG2E) * t0f)
        const = jnp.where(crow < AUX_ONE, 1.0, jnp.where(crow == AUX_ONE, hi, jnp.where(
            crow == AUX_ONE + 1, mid, jnp.where(crow == AUX_ONE + 2, lo, 0.0))))
        rest = jnp.zeros((AUX_LANES - AUX_BLOCK - MAX_BLOCKS, blk), F32)
        qaug[h] = jnp.concatenate([qth * (scale * LOG2E), const, mask, rest], axis=0).astype(BF16)
        k2 = jnp.max(jnp.where(hlane == h, kmax2, 0.0), axis=1, keepdims=True)
        bound = jnp.sqrt(jnp.sum(qth * qth, axis=0, keepdims=True) * k2) * (scale * BOUND_SLACK)
        shifts.append((bound + slope * tq) * LOG2E)
        spreads.append(bound)
    shift = jnp.concatenate(shifts, axis=0)
    safe = 2.0 * jnp.max(jnp.concatenate(spreads, axis=0)) < MAX_SPREAD

    def scores(n, h):
        r = pl.multiple_of(n * blk, blk)
        return _dot(kaug_ref[0, pl.ds(r, blk), h * ka:(h + 1) * ka], qaug[h])

    causal = (lax.broadcasted_iota(jnp.int32, (blk, blk), 0) <= lax.broadcasted_iota(jnp.int32, (blk, blk), 1))

    def finish(l_all):
        out_t = jnp.concatenate([acc[h * dh:(h + 1) * dh, :] / l_all[h:h + 1] for h in range(n_heads)], axis=0)
        o_ref[0] = out_t.T

    @pl.when(safe)
    def _():
        def consume(n, l_all, get_scores, own=False):
            ls = []
            for h in range(n_heads):
                s = get_scores(h)
                if own:
                    s = jnp.where(causal, s, NEG_INF)
                p = jnp.exp2(s - shift[h:h + 1])
                p_scr[h] = p.astype(BF16)
                ls.append(jnp.sum(p, axis=0, keepdims=True))
            for h in range(n_heads):
                hs = slice(h * dh, (h + 1) * dh)
                pv = _dot(vtb_ref[0, h, n], p_scr[h])
                acc[hs, :] = pv if own else acc[hs, :] + pv
            l_new = jnp.concatenate(ls, axis=0)
            return l_new if own else l_all + l_new

        l_all = consume(j, None, lambda h: scores(j, h), own=True)

        def group(first, width, l_all):
            for r in range(width):
                l_all = consume(first + r, l_all, lambda h, n=first + r: scores(n, h))
            return l_all

        done = 0
        for width in PAST_UNROLL:
            trips = (j - done) // width
            l_all = lax.fori_loop(0, trips, lambda i, l, d=done, w=width: group(d + i * w, w, l), l_all)
            done = done + trips * width
        finish(l_all)

    @pl.when(jnp.logical_not(safe))
    def _():
        def tile(n, st, own):
            ms, ls, alphas = [], [], []
            for h in range(n_heads):
                s_scr[h] = scores(n, h)
            for h in range(n_heads):
                s = s_scr[h]
                if own:
                    s = jnp.where(causal, s, NEG_INF)
                    m_new = jnp.max(s, axis=0, keepdims=True)
                else:
                    m_new = jnp.maximum(st[0][h:h + 1], jnp.max(s, axis=0, keepdims=True))
                    alphas.append(jnp.exp2(st[0][h:h + 1] - m_new))
                p = jnp.exp2(s - m_new)
                p_scr[h] = p.astype(BF16)
                ms.append(m_new)
                l_new = jnp.sum(p, axis=0, keepdims=True)
                ls.append(l_new if own else alphas[h] * st[1][h:h + 1] + l_new)
            for h in range(n_heads):
                hs = slice(h * dh, (h + 1) * dh)
                pv = _dot(vtb_ref[0, h, n], p_scr[h])
                acc[hs, :] = pv if own else alphas[h] * acc[hs, :] + pv
            return jnp.concatenate(ms, axis=0), jnp.concatenate(ls, axis=0)

        finish(lax.fori_loop(0, j, lambda n, st: tile(n, st, False), tile(j, None, True))[1])


def _moba_prompt(q, kaug, vtb, sums, knorm):
    b, l, d = q.shape
    nb = l // MOBA_BLOCK
    n_heads = d // ATTN_HEAD_DIM
    ka = ATTN_HEAD_DIM + AUX_LANES
    once = pl.Buffered(1)
    return pl.pallas_call(
        functools.partial(_moba_prompt_body, n_heads=n_heads),
        grid=(b, nb),
        in_specs=[pl.BlockSpec((1, MOBA_BLOCK, d), lambda i, j: (i, j, 0)),
                  pl.BlockSpec((1, l, n_heads * ka), lambda i, j: (i, 0, 0), pipeline_mode=once),
                  pl.BlockSpec((1, n_heads, nb, ATTN_HEAD_DIM, MOBA_BLOCK), lambda i, j: (i, 0, 0, 0, 0),
                               pipeline_mode=once),
                  pl.BlockSpec((1, nb, d), lambda i, j: (i, 0, 0)),
                  pl.BlockSpec((1,) + knorm.shape[1:], lambda i, j: (i, 0, 0))],
        out_specs=pl.BlockSpec((1, MOBA_BLOCK, d), lambda i, j: (i, j, 0)),
        out_shape=jax.ShapeDtypeStruct((b, l, d), F32),
        scratch_shapes=[pltpu.VMEM((n_heads, ka, MOBA_BLOCK), BF16), pltpu.VMEM((d, MOBA_BLOCK), F32),
                        pltpu.VMEM((n_heads, MOBA_BLOCK, MOBA_BLOCK), F32),
                        pltpu.VMEM((n_heads, MOBA_BLOCK, MOBA_BLOCK), BF16)],
        compiler_params=_params("parallel", "arbitrary"),
        name="moba_prompt",
    )(q, kaug, vtb, sums, knorm)


def _moba_select_body(ps_ref, idx_ref, *, pages_per_block, n_blocks):
    ps = ps_ref[...]
    lanes = ps.shape[1]
    lane = lax.broadcasted_iota(jnp.int32, (1, lanes), 1)
    g = ps
    for i in range(1, pages_per_block):
        g = g + pltpu.roll(ps, lanes - i, 1)
    g = g * (1.0 / MOBA_BLOCK)
    is_block = jnp.logical_and(lane % pages_per_block == 0, lane < n_blocks * pages_per_block)
    g = jnp.where(is_block, g, -jnp.inf)
    lanef = lane.astype(F32)
    out = jnp.zeros(idx_ref.shape, F32)
    olane = lax.broadcasted_iota(jnp.int32, idx_ref.shape, 1)
    for r in range(MOBA_TOPK):
        mx = jnp.max(g, axis=1, keepdims=True)
        first = jnp.min(jnp.where(g == mx, lanef, float(lanes)), axis=1, keepdims=True)
        out = jnp.where(olane == r, first * (1.0 / pages_per_block), out)
        g = jnp.where(lanef == first, -jnp.inf, g)
    idx_ref[...] = out.astype(jnp.int32)


def _moba_select(page_sums2, *, pages_per_block, n_blocks):
    rows, lanes = page_sums2.shape
    return pl.pallas_call(
        functools.partial(_moba_select_body, pages_per_block=pages_per_block, n_blocks=n_blocks),
        grid=(1,),
        in_specs=[_full((rows, lanes))],
        out_specs=_full((rows, LANES)),
        out_shape=jax.ShapeDtypeStruct((rows, LANES), jnp.int32),
        compiler_params=_params("arbitrary"),
        name="moba_sample_select",
    )(page_sums2)


def _moba_attend_body(blk_ref, pg_ref, q_ref, kn_ref, vn_ref, *refs, pages_per_block, n_heads, past_len):
    del pg_ref
    n_lg = MOBA_TOPK
    n_v = MOBA_TOPK * pages_per_block
    lg_refs, v_refs, o_ref = refs[:n_heads * n_lg], refs[n_heads * n_lg:n_heads * (n_lg + n_v)], refs[-1]
    b = pl.program_id(0)
    dh = ATTN_HEAD_DIM
    blk = MOBA_BLOCK
    scale = dh ** -0.5
    lane = lax.broadcasted_iota(jnp.int32, (1, blk), 1)
    qk = q_ref[0] * kn_ref[0]
    outs = []
    for h in range(n_heads):
        hs = slice(h * dh, (h + 1) * dh)
        slope = _alibi_slope(h, n_heads)
        s_own = jnp.sum(qk[:, hs], axis=1, keepdims=True) * scale
        ss = []
        for i in range(n_lg):
            pos = blk_ref[b, h, i] * blk + lane
            ss.append(lg_refs[h * n_lg + i][0, 0, h:h + 1, :] * scale - slope * (past_len - pos).astype(F32))
        s = jnp.concatenate(ss, axis=1)
        m = jnp.maximum(s_own, jnp.max(s, axis=1, keepdims=True))
        p_own = jnp.exp(s_own - m)
        p = jnp.exp(s - m)
        l = p_own + jnp.sum(p, axis=1, keepdims=True)
        vt = jnp.concatenate([v_refs[h * n_v + c][0, 0, 0] for c in range(n_v)], axis=1)
        acc = p_own * vn_ref[0][:, hs] + _dot_t(p.astype(BF16), vt.astype(BF16))
        outs.append(acc / l)
    o_ref[0] = jnp.concatenate(outs, axis=1)


def _moba_attend(blocks, pages_idx, q3, kn3, vn3, logits, cache_vt, *, pages_per_block, past_len):
    n, _, d = q3.shape
    _, _, n_heads, dh, page = cache_vt.shape
    row = pl.BlockSpec((1, 1, d), lambda b, bl, pg: (b, 0, 0))
    lg_spec = lambda h, i: pl.BlockSpec(
        (1, 1, n_heads, MOBA_BLOCK), lambda b, bl, pg: (b, bl[b, h, i], 0, 0))
    v_spec = lambda h, i: pl.BlockSpec(
        (1, 1, 1, dh, page), lambda b, bl, pg: (0, pg[b, h, i], h, 0, 0))
    n_v = MOBA_TOPK * pages_per_block
    lg_specs = [lg_spec(h, i) for h in range(n_heads) for i in range(MOBA_TOPK)]
    v_specs = [v_spec(h, i) for h in range(n_heads) for i in range(n_v)]
    grid_spec = pltpu.PrefetchScalarGridSpec(
        num_scalar_prefetch=2,
        grid=(n,),
        in_specs=[row, row, row] + lg_specs + v_specs,
        out_specs=row)
    return pl.pallas_call(
        functools.partial(_moba_attend_body, pages_per_block=pages_per_block, n_heads=n_heads,
                          past_len=past_len),
        grid_spec=grid_spec,
        out_shape=jax.ShapeDtypeStruct((n, 1, d), F32),
        compiler_params=_params("parallel"),
        name="moba_sample_attend",
    )(blocks, pages_idx, q3, kn3, vn3, *([logits] * len(lg_specs)), *([cache_vt] * len(v_specs)))


def _moba_sample(q, k_new, v_new, logits, page_sums, cache_vt, page_table):
    n, d = q.shape
    _, _, n_heads, dh, page = cache_vt.shape
    n_pages = page_table.shape[1]
    past_len = n_pages * page
    pages_per_block = MOBA_BLOCK // page
    n_blocks = n_pages // pages_per_block
    assert n_blocks >= MOBA_TOPK and n_pages % LANES == 0
    q3 = q.reshape(n, 1, d)
    blocks = _moba_select(page_sums.reshape(n * n_heads, n_pages), pages_per_block=pages_per_block,
                          n_blocks=n_blocks)[:, :MOBA_TOPK].reshape(n, n_heads, MOBA_TOPK)
    logical = (blocks[..., None] * pages_per_block + jnp.arange(pages_per_block)).reshape(n, n_heads, -1)
    pages_idx = jnp.take_along_axis(page_table[:, None, :], logical, axis=2)
    y = _moba_attend(blocks, pages_idx, q3, k_new.reshape(n, 1, d), v_new.reshape(n, 1, d),
                     logits, cache_vt,
                     pages_per_block=pages_per_block, past_len=past_len)
    return y.reshape(n, d)


def _memkv_body(m_ref, g_ref, wk_ref, wv_ref, k_ref, v_ref):
    h = _rms(m_ref[...], g_ref[...]).astype(BF16)
    k_ref[...] = _dot(h, wk_ref[...])
    v_ref[...] = _dot(h, wv_ref[...])


def _memkv(mem, g, wk, wv, *, tm):
    n, d = mem.shape
    row = pl.BlockSpec((tm, d), lambda i: (i, 0))
    return pl.pallas_call(
        _memkv_body,
        grid=(n // tm,),
        in_specs=[row, _full((1, d)), _full(wk.shape), _full(wv.shape)],
        out_specs=[row, row],
        out_shape=[jax.ShapeDtypeStruct((n, d), F32)] * 2,
        compiler_params=_params("parallel"),
        name="mem_kv",
    )(mem, g, wk, wv)


def _mix_out(x, ys, ya, gs, ga, w_out_ref, d_ssm):
    ysn = _rms(ys, gs).astype(BF16)
    yan = _rms(ya, ga).astype(BF16)
    return x + _dot(ysn, w_out_ref[:d_ssm, :]) + _dot(yan, w_out_ref[d_ssm:, :])


def _xattn_heads(q, mk_ref, mv_ref):
    d = q.shape[1]
    xd = d // N_XHEADS
    outs = []
    for h in range(N_XHEADS):
        hs = slice(h * xd, (h + 1) * xd)
        s = _dot_t(q[:, hs].astype(BF16), mk_ref[0, :, hs].astype(BF16)) * (xd ** -0.5)
        m = jnp.max(s, axis=1, keepdims=True)
        p = jnp.exp(s - m)
        l = jnp.sum(p, axis=1, keepdims=True)
        outs.append(_dot(p.astype(BF16), mv_ref[0, :, hs].astype(BF16)) / l)
    return jnp.concatenate(outs, axis=1)


def _merge_prompt_body(x_ref, ys_ref, ya_ref, gs_ref, ga_ref, wout_ref, gx_ref, wq_ref, mk_ref, mv_ref,
                       wo_ref, o_ref, *, d_ssm):
    x2 = _mix_out(x_ref[0], ys_ref[...], ya_ref[0], gs_ref[...], ga_ref[...], wout_ref, d_ssm)
    q = _dot(_rms(x2, gx_ref[...]).astype(BF16), wq_ref[...])
    o = _xattn_heads(q, mk_ref, mv_ref)
    o_ref[0] = x2 + _dot(o.astype(BF16), wo_ref[...])


def _merge_prompt(x, ys, ya, gs, ga, w_out, gx, wq, mk, mv, wo, *, tm):
    b, l, d = x.shape
    d_ssm = ys.shape[1] // b
    d_attn = ya.shape[2]
    n_mem = mk.shape[1]
    row = lambda w: pl.BlockSpec((1, tm, w), lambda i, j: (i, j, 0))
    mem = pl.BlockSpec((1, n_mem, d), lambda i, j: (i, 0, 0))
    return pl.pallas_call(
        functools.partial(_merge_prompt_body, d_ssm=d_ssm),
        grid=(b, l // tm),
        in_specs=[row(d), pl.BlockSpec((tm, d_ssm), lambda i, j: (j, i)), row(d_attn), _full((1, d_ssm)),
                  _full((1, d_attn)), _full(w_out.shape), _full((1, d)), _full(wq.shape), mem, mem,
                  _full(wo.shape)],
        out_specs=row(d),
        out_shape=jax.ShapeDtypeStruct((b, l, d), F32),
        compiler_params=_params("parallel", "parallel"),
        name="merge_prompt",
    )(x, ys, ya, gs, ga, w_out, gx, wq, mk, mv, wo)


def _merge_pre_body(x_ref, ys_ref, ya_ref, gs_ref, ga_ref, wout_ref, gx_ref, wq_ref, x2_ref, q_ref, *, d_ssm):
    x2 = _mix_out(x_ref[...], ys_ref[...], ya_ref[...], gs_ref[...], ga_ref[...], wout_ref, d_ssm)
    x2_ref[...] = x2
    q_ref[...] = _dot(_rms(x2, gx_ref[...]).astype(BF16), wq_ref[...])


def _merge_pre(x, ys, ya, gs, ga, w_out, gx, wq):
    n, d = x.shape
    args = (x, ys, ya, gs, ga, w_out, gx, wq)
    return pl.pallas_call(
        functools.partial(_merge_pre_body, d_ssm=ys.shape[1]),
        grid=(1,),
        in_specs=[_full(a.shape) for a in args],
        out_specs=[_full((n, d)), _full((n, d))],
        out_shape=[jax.ShapeDtypeStruct((n, d), F32)] * 2,
        compiler_params=_params("arbitrary"),
        name="merge_sample_pre",
    )(*args)


def _xattn_sample_body(q_ref, mk_ref, mv_ref, o_ref):
    q = q_ref[0]
    n_mem, n_heads, xd = mk_ref.shape[1:]
    q4 = jnp.concatenate([q[:, h * xd:(h + 1) * xd] for h in range(n_heads)], axis=0).astype(BF16)
    k2 = mk_ref[0].reshape(n_mem * n_heads, xd).astype(BF16)
    v2 = mv_ref[0].reshape(n_mem * n_heads, xd).astype(BF16)
    s = _dot_t(q4, k2) * (xd ** -0.5)
    own = lax.broadcasted_iota(jnp.int32, s.shape, 1) % n_heads == lax.broadcasted_iota(jnp.int32, s.shape, 0)
    s = jnp.where(own, s, NEG_INF)
    m = jnp.max(s, axis=1, keepdims=True)
    p = jnp.exp(s - m)
    o4 = _dot(p.astype(BF16), v2) / jnp.sum(p, axis=1, keepdims=True)
    o_ref[0] = jnp.concatenate([o4[h:h + 1] for h in range(n_heads)], axis=1)


def _xattn_sample(q3, mk, mv):
    n, _, d = q3.shape
    row = pl.BlockSpec((1, 1, d), lambda i: (i, 0, 0))
    mem = pl.BlockSpec((1,) + mk.shape[1:], lambda i: (i, 0, 0, 0))
    return pl.pallas_call(
        _xattn_sample_body,
        grid=(n,),
        in_specs=[row, mem, mem],
        out_specs=row,
        out_shape=jax.ShapeDtypeStruct((n, 1, d), F32),
        compiler_params=_params("parallel"),
        name="xattn_sample",
    )(q3, mk, mv)


def _merge_post_body(x_ref, o_ref, wo_ref, y_ref):
    y_ref[...] = x_ref[...] + _dot(o_ref[...].astype(BF16), wo_ref[...])


def _merge_post(x2, o, wo):
    n, d = x2.shape
    return pl.pallas_call(
        _merge_post_body,
        grid=(1,),
        in_specs=[_full((n, d)), _full((n, d)), _full(wo.shape)],
        out_specs=_full((n, d)),
        out_shape=jax.ShapeDtypeStruct((n, d), F32),
        compiler_params=_params("arbitrary"),
        name="merge_sample_post",
    )(x2, o, wo)


def kernel(x_prompt, x_sample, mem_prompt, cache_k, cache_v, page_table, state_ssm_re, state_ssm_im, cache_mem_k, cache_mem_v, g_ffn1, w1_ffn1, w3_ffn1, w2_ffn1, g_mix, w_in, ssm_a_re, ssm_a_im, ssm_log_dt, ssm_b_re, ssm_b_im, ssm_c_re, ssm_c_im, ssm_d, w_glu, b_glu, g_out_ssm, g_out_attn, w_out, g_xattn, g_mem, wq_x, wk_x, wv_x, wo_x, g_ffn2, w1_ffn2, w3_ffn2, w2_ffn2, g_final):
    depth = g_ffn1.shape[0]
    assert depth == 1
    b, l, d = x_prompt.shape
    ns, ls, _ = x_sample.shape
    assert ls == 1
    n_groups, n_state = ssm_a_re.shape[1:]
    gp = n_groups * n_state
    d_ssm = n_groups * SSM_GROUP
    d_attn = (w_in.shape[2] - d_ssm) // 3
    n_heads = d_attn // ATTN_HEAD_DIM
    n_mem = mem_prompt.shape[1]
    xd = d // N_XHEADS

    vec = lambda a: a[0].reshape(1, -1)
    wb = lambda a: a[0].astype(BF16)
    w1a, w3a, w2a = wb(w1_ffn1), wb(w3_ffn1), wb(w2_ffn1)
    w1b, w3b, w2b = wb(w1_ffn2), wb(w3_ffn2), wb(w2_ffn2)
    w_in_b, w_out_b, w_glu_b = wb(w_in), wb(w_out), wb(w_glu)
    wq_b, wk_b, wv_b, wo_b = wb(wq_x), wb(wk_x), wb(wv_x), wb(wo_x)
    gfin = g_final.reshape(1, -1)
    ar, ai, bmat, cmat = _s5_matrices(ssm_a_re[0], ssm_a_im[0], ssm_log_dt[0], ssm_b_re[0], ssm_b_im[0],
                                      ssm_c_re[0], ssm_c_im[0])
    s5_w = (ar, ai, bmat, cmat, vec(ssm_d), w_glu_b, vec(b_glu))

    xs = x_sample.reshape(ns, d)
    xs1 = _ffn(xs, vec(g_ffn1), w1_ffn1[0], w3_ffn1[0], w2_ffn1[0], gfin, final_norm=False, tm=ns)
    us, qs, ks, vs = _proj(xs1, vec(g_mix), w_in[0], d_ssm=d_ssm, tm=ns)
    q_t = qs.reshape(ns, n_heads, ATTN_HEAD_DIM).swapaxes(1, 2)
    cache_kt = cache_k.transpose(0, 1, 3, 4, 2)
    cache_vt = cache_v.transpose(0, 1, 3, 4, 2)
    half_pages = page_table.shape[1] // 2

    tm = 512 if l % 512 == 0 else l
    xp = x_prompt.reshape(b * l, d)
    mem_k, mem_v = _memkv(mem_prompt.reshape(b * n_mem, d), vec(g_mem), wk_b, wv_b,
                          tm=min(256, b * n_mem))
    x1, lg_a, ps_a = _ffn_kpass(xp, vec(g_ffn1), w1a, w3a, w2a, gfin, page_table, q_t, cache_kt,
                                final_norm=False, tm=tm, first_page=0, n_call_pages=half_pages)
    x1 = x1.reshape(b, l, d)
    tmw = 1024 if l % 1024 == 0 else tm
    u, q, k_t, v_t, kaug, vtb, sums, knorm = _proj_prompt(x1, vec(g_mix), w_in_b, d_ssm=d_ssm, tm=tmw)
    t_chunk = 256 if l % 256 == 0 else l
    ys, h_last = _s5_prompt(u, b, *s5_w, t_chunk=t_chunk)
    ya = _moba_prompt(q, kaug, vtb, sums.reshape(b, l // MOBA_BLOCK, d_attn), knorm.reshape(b, l // tmw, LANES))
    x3 = _merge_prompt(x1, ys, ya, vec(g_out_ssm), vec(g_out_attn), w_out_b, vec(g_xattn),
                       wq_b, mem_k.reshape(b, n_mem, d), mem_v.reshape(b, n_mem, d), wo_b, tm=tmw)
    y_prompt, lg_b, ps_b = _ffn_kpass(x3.reshape(b * l, d), vec(g_ffn2), w1b, w3b, w2b, gfin, page_table, q_t,
                                      cache_kt, final_norm=True, tm=tm, first_page=half_pages,
                                      n_call_pages=half_pages)
    h_last = h_last.reshape(2, b, 2, n_groups // 2, n_state).transpose(2, 1, 0, 3, 4).reshape(2, b, gp)
    k = k_t.transpose(0, 3, 1, 2)
    v = v_t.transpose(0, 3, 1, 2)
    heads = (1, b, l, n_heads, ATTN_HEAD_DIM)
    state = (1, b, n_groups, n_state)
    memkv = (1, b, n_mem, N_XHEADS, xd)

    yss, hrs, his = _s5_sample(us, state_ssm_re[0].reshape(ns, gp), state_ssm_im[0].reshape(ns, gp), *s5_w)
    yas = _moba_sample(qs, ks, vs, jnp.concatenate([lg_a, lg_b], axis=1), ps_a + ps_b, cache_vt, page_table)
    xs2, qx = _merge_pre(xs1, yss, yas, vec(g_out_ssm), vec(g_out_attn), w_out_b, vec(g_xattn), wq_b)
    ox = _xattn_sample(qx.reshape(ns, 1, d), cache_mem_k[0], cache_mem_v[0])
    xs3 = _merge_post(xs2, ox.reshape(ns, d), wo_b)
    y_sample = _ffn(xs3, vec(g_ffn2), w1b, w3b, w2b, gfin, final_norm=True, tm=ns)
    sheads = (1, ns, 1, n_heads, ATTN_HEAD_DIM)
    sstate = (1, ns, n_groups, n_state)

    return (y_prompt.reshape(b, l, d), y_sample.reshape(ns, 1, d),
            k.reshape(heads), v.reshape(heads),
            h_last[0].reshape(state), h_last[1].reshape(state),
            mem_k.reshape(memkv), mem_v.reshape(memkv),
            ks.reshape(sheads), vs.reshape(sheads),
            hrs.reshape(sstate), his.reshape(sstate))
```
